```python
import jax, jax.numpy as jnp
from jax import lax
import numpy as np

D_MODEL = 2048
BATCH = 2
SEQ = 16384
DEPTH = 2

N_EVEN = (DEPTH + 1) // 2
N_ODD = DEPTH // 2
PLE_DIM = 256
CHUNK = 64
EPS = 1e-6
A_HEADS = 8
A_DK = 128
A_DV = 128
B_HEADS = 8
B_DK = 128
B_DV = 128
CONV_WIDTH = 4
A_QK_W = A_HEADS * A_DK
A_V_W = A_HEADS * A_DV
B_QK_W = B_HEADS * B_DK
B_V_W = B_HEADS * B_DV
EVEN_MIX_W = A_V_W + B_V_W
EVEN_PROJ_W = 2 * A_QK_W + 2 * A_V_W + 2 * B_QK_W + 2 * B_V_W + 2 * B_HEADS
C_WIDTH = D_MODEL
C_BLOCKS = 8
C_BLOCK_W = C_WIDTH // C_BLOCKS
RGLRU_C = 8.0
FFN_DIM = ((8 * D_MODEL // 3 + 255) // 256) * 256
N_EXPERTS = 8
TOP_K = 2
EXPERT_DIM = FFN_DIM
MOE_BLOCK = 512

kernel_name = "hgrn2_gdn_rglru_moe_hybrid"


def rmsnorm(x, g):
    xf = x.astype(jnp.float32)
    y = xf * lax.rsqrt(jnp.mean(xf * xf, axis=-1, keepdims=True) + EPS)
    return (y * g.astype(jnp.float32)).astype(x.dtype)


def l2norm(x):
    return x * lax.rsqrt(jnp.sum(x * x, axis=-1, keepdims=True) + EPS)


def causal_dwconv(x, w):
    k, c = w.shape
    return lax.conv_general_dilated(x, w[:, None, :].astype(x.dtype), (1,), [(k - 1, 0)],
                                    dimension_numbers=("NWC", "WIO", "NWC"), feature_group_count=c)


def to_chunks(t):
    b, s = t.shape[:2]
    return jnp.moveaxis(t.reshape((b, s // CHUNK, CHUNK) + t.shape[2:]), 3, 1)


def from_chunks(t):
    t = jnp.moveaxis(t, 1, 3)
    return t.reshape((t.shape[0], t.shape[1] * t.shape[2]) + t.shape[3:])


def hgrn2_chunked(q, k, v, logf):
    qc, kc, vc, gc = (to_chunks(t) for t in (q, k, v, logf))
    bcum = jnp.cumsum(gc, axis=3)
    causal = jnp.tril(jnp.ones((CHUNK, CHUNK), bool))

    def step(state, inp):
        qi, ki, vi, bi = inp
        rel = jnp.where(causal[:, :, None], bi[:, :, :, None, :] - bi[:, :, None, :, :], -jnp.inf)
        scores = jnp.einsum("bhtd,bhsd,bhtsd->bhts", qi, ki, jnp.exp(rel))
        blast = bi[:, :, -1:, :]
        o = (jnp.einsum("bhts,bhsv->bhtv", scores, vi)
             + jnp.einsum("bhtd,bhdv->bhtv", qi * jnp.exp(bi), state))
        state = (jnp.exp(blast[:, :, 0, :])[..., None] * state
                 + jnp.einsum("bhsd,bhsv->bhdv", ki * jnp.exp(blast - bi), vi))
        return state, o

    state0 = jnp.zeros(qc.shape[:2] + (qc.shape[-1], vc.shape[-1]), jnp.float32)
    xs = tuple(jnp.moveaxis(t, 2, 0) for t in (qc, kc, vc, bcum))
    _, o = lax.scan(step, state0, xs)
    return from_chunks(jnp.moveaxis(o, 0, 2))


def gated_delta_chunked(q, k, v, g, beta):
    qc, kc, vc = (to_chunks(t) for t in (q, k, v))
    gc, bc = to_chunks(g), to_chunks(beta)
    gam = jnp.cumsum(gc, axis=-1)
    causal = jnp.tril(jnp.ones((CHUNK, CHUNK), bool))
    strict = jnp.tril(jnp.ones((CHUNK, CHUNK), bool), -1)
    decay = jnp.exp(jnp.where(causal, gam[..., :, None] - gam[..., None, :], -jnp.inf))
    kbeta = kc * bc[..., None]
    a_mat = jnp.where(strict, jnp.einsum("bhntd,bhnsd->bhnts", kbeta, kc) * decay, 0.0)
    lower = a_mat + jnp.eye(CHUNK, dtype=jnp.float32)
    u = lax.linalg.triangular_solve(lower, vc * bc[..., None], left_side=True, lower=True,
                                    unit_diagonal=True)
    w = lax.linalg.triangular_solve(lower, kbeta * jnp.exp(gam)[..., None], left_side=True,
                                    lower=True, unit_diagonal=True)
    qk = jnp.einsum("bhntd,bhnsd->bhnts", qc, kc) * decay

    def step(state, inp):
        qi, ki, ui, wi, qki, gi = inp
        v_new = ui - jnp.einsum("bhtd,bhdv->bhtv", wi, state)
        o = (jnp.einsum("bhtd,bhdv->bhtv", qi * jnp.exp(gi)[..., None], state)
             + jnp.einsum("bhts,bhsv->bhtv", qki, v_new))
        glast = gi[..., -1:]
        state = (jnp.exp(glast)[..., None] * state
                 + jnp.einsum("bhsd,bhsv->bhdv", ki * jnp.exp(glast - gi)[..., None], v_new))
        return state, o

    state0 = jnp.zeros(qc.shape[:2] + (qc.shape[-1], vc.shape[-1]), jnp.float32)
    xs = tuple(jnp.moveaxis(t, 2, 0) for t in (qc, kc, u, w, qk, gam))
    _, o = lax.scan(step, state0, xs)
    return from_chunks(jnp.moveaxis(o, 0, 2))


def even_mixer(u, w_in, lb, conv_w, a_log, dt_bias, gn_a, gn_b, w_out):
    bsz, s, _ = u.shape
    f32 = jnp.float32
    proj = (u @ w_in).astype(f32)
    cuts = [int(c) for c in np.cumsum([A_QK_W, A_QK_W, A_V_W, A_V_W, 2 * B_QK_W + B_V_W, B_V_W, B_HEADS])]
    qa, fa, ia, ga, qkv_b, zb, ab, bb = jnp.split(proj, cuts, axis=-1)

    def heads(t, h):
        return t.reshape(bsz, s, h, -1)

    lbf = lb.astype(f32)
    forget = lbf + (1.0 - lbf) * jax.nn.sigmoid(fa)
    o_a = hgrn2_chunked(heads(qa, A_HEADS) * A_DK ** -0.5, heads(1.0 - forget, A_HEADS),
                        heads(ia, A_HEADS), heads(jnp.log(forget), A_HEADS))
    o_a = rmsnorm(o_a, gn_a) * jax.nn.silu(heads(ga, A_HEADS))

    qkv_b = jax.nn.silu(causal_dwconv(qkv_b, conv_w.astype(f32)))
    qb, kb, vb = jnp.split(qkv_b, [B_QK_W, 2 * B_QK_W], axis=-1)
    qb = l2norm(heads(qb, B_HEADS)) * B_DK ** -0.5
    kb = l2norm(heads(kb, B_HEADS))
    beta = jax.nn.sigmoid(bb)
    g = -jnp.exp(a_log.astype(f32)) * jax.nn.softplus(ab + dt_bias.astype(f32))
    o_b = gated_delta_chunked(qb, kb, heads(vb, B_HEADS), g, beta)
    o_b = rmsnorm(o_b, gn_b) * jax.nn.silu(heads(zb, B_HEADS))

    mixed = jnp.concatenate([o_a.reshape(bsz, s, A_V_W), o_b.reshape(bsz, s, B_V_W)], axis=-1)
    return mixed.astype(u.dtype) @ w_out


def _lin_rec_combine(left, right):
    a_l, b_l = left
    a_r, b_r = right
    return a_l * a_r, a_r * b_l + b_r


def odd_mixer(u, w_in, conv_w, conv_b, w_r, b_r, w_i, b_i, lam, w_out):
    bsz, s, _ = u.shape
    f32 = jnp.float32
    y_branch, xr = jnp.split((u @ w_in).astype(f32), [C_WIDTH], axis=-1)
    y_branch = jax.nn.gelu(y_branch)
    xr = causal_dwconv(xr, conv_w.astype(f32)) + conv_b.astype(f32)
    xb = xr.reshape(bsz, s, C_BLOCKS, C_BLOCK_W)
    r = jax.nn.sigmoid(jnp.einsum("bsnc,ncd->bsnd", xb, w_r.astype(f32)).reshape(bsz, s, C_WIDTH)
                       + b_r.astype(f32))
    gi = jax.nn.sigmoid(jnp.einsum("bsnc,ncd->bsnd", xb, w_i.astype(f32)).reshape(bsz, s, C_WIDTH)
                        + b_i.astype(f32))
    log_a = -RGLRU_C * r * jax.nn.softplus(-lam.astype(f32))
    a = jnp.exp(log_a)
    mult = jnp.sqrt(jnp.maximum(-jnp.expm1(2.0 * log_a), 0.0))
    mult = jnp.where((jnp.arange(s) == 0)[None, :, None], 1.0, mult)
    _, h = lax.associative_scan(_lin_rec_combine, (a, mult * gi * xr), axis=1)
    return (h * y_branch).astype(u.dtype) @ w_out


def swiglu(u, wg, wu, wd):
    return (jax.nn.silu(u @ wg) * (u @ wu)) @ wd


def moe_swiglu(u, w_router, w_gate, w_up, w_down):
    bsz, s, d = u.shape
    t = bsz * s
    xs = u.reshape(t, d)
    logits = (xs @ w_router).astype(jnp.float32)
    top_logits, top_idx = lax.top_k(logits, TOP_K)
    gates = jax.nn.softmax(top_logits, axis=-1)
    e_flat = top_idx.reshape(-1)
    tok_flat = jnp.repeat(jnp.arange(t, dtype=jnp.int32), TOP_K)
    g_flat = gates.reshape(-1)
    order = jnp.argsort(e_flat)
    e_sorted, tok_sorted, g_sorted = e_flat[order], tok_flat[order], g_flat[order]
    counts = jnp.bincount(e_flat, length=N_EXPERTS)
    padded = ((counts + MOE_BLOCK - 1) // MOE_BLOCK) * MOE_BLOCK
    start = jnp.cumsum(counts) - counts
    pend = jnp.cumsum(padded)
    pstart = pend - padded
    dest = pstart[e_sorted] + jnp.arange(t * TOP_K) - start[e_sorted]
    n_blocks = -(-(t * TOP_K) // MOE_BLOCK) + N_EXPERTS
    rows = n_blocks * MOE_BLOCK
    row_tok = jnp.full((rows,), t, jnp.int32).at[dest].set(tok_sorted)
    row_gate = jnp.zeros((rows,), jnp.float32).at[dest].set(g_sorted)
    block_expert = jnp.minimum(
        jnp.searchsorted(pend, jnp.arange(n_blocks) * MOE_BLOCK, side="right"), N_EXPERTS - 1)
    xs_pad = jnp.concatenate([xs, jnp.zeros((1, d), xs.dtype)], axis=0)

    def expert_block(args):
        toks, gts, e = args
        xb = xs_pad[toks]
        hb = jax.nn.silu(xb @ w_gate[e]) * (xb @ w_up[e])
        return (hb @ w_down[e]) * gts[:, None].astype(xb.dtype)

    yb = lax.map(expert_block, (row_tok.reshape(n_blocks, MOE_BLOCK),
                                row_gate.reshape(n_blocks, MOE_BLOCK), block_expert))
    y = jnp.zeros((t + 1, d), yb.dtype).at[row_tok].add(yb.reshape(rows, d))[:t]
    return y.reshape(bsz, s, d)


def setup_inputs(seed: int = 0) -> dict:
    key = jax.random.key(seed)
    ks = iter(jax.random.split(key, 48))
    f32 = jnp.float32

    def nrm(shape, scale):
        return jax.random.normal(next(ks), shape, f32) * scale

    def gain(shape):
        return 1.0 + 0.1 * jax.random.normal(next(ks), shape, f32)

    def unif(shape, lo, hi):
        return jax.random.uniform(next(ks), shape, f32, lo, hi)

    x = nrm((BATCH, SEQ, D_MODEL), 1.0)
    p = nrm((DEPTH, BATCH, SEQ, PLE_DIM), 1.0)
    ln_mix = gain((DEPTH, D_MODEL))
    ln_ffn = gain((DEPTH, D_MODEL))
    ln_ple = gain((DEPTH, D_MODEL))
    ln_final = gain((D_MODEL,))
    lb_table = gain((DEPTH + 1, A_QK_W))
    ab_w_in = nrm((N_EVEN, D_MODEL, EVEN_PROJ_W), D_MODEL ** -0.5)
    ab_conv = nrm((N_EVEN, CONV_WIDTH, 2 * B_QK_W + B_V_W), CONV_WIDTH ** -0.5)
    b_a_log = jnp.log(unif((N_EVEN, B_HEADS), 1.0, 16.0))
    dt = jnp.exp(unif((N_EVEN, B_HEADS), float(np.log(1e-3)), float(np.log(1e-1))))
    b_dt_bias = dt + jnp.log(-jnp.expm1(-dt))
    a_gnorm = gain((N_EVEN, A_DV))
    b_gnorm = gain((N_EVEN, B_DV))
    ab_w_out = nrm((N_EVEN, EVEN_MIX_W, D_MODEL), EVEN_MIX_W ** -0.5)
    c_w_in = nrm((N_ODD, D_MODEL, 2 * C_WIDTH), D_MODEL ** -0.5)
    c_conv_w = nrm((N_ODD, CONV_WIDTH, C_WIDTH), CONV_WIDTH ** -0.5)
    c_conv_b = nrm((N_ODD, C_WIDTH), 0.01)
    c_w_r = nrm((N_ODD, C_BLOCKS, C_BLOCK_W, C_BLOCK_W), C_BLOCK_W ** -0.5)
    c_b_r = nrm((N_ODD, C_WIDTH), 0.1)
    c_w_i = nrm((N_ODD, C_BLOCKS, C_BLOCK_W, C_BLOCK_W), C_BLOCK_W ** -0.5)
    c_b_i = nrm((N_ODD, C_WIDTH), 0.1)
    a0 = unif((N_ODD, C_WIDTH), 0.9, 0.999)
    sg = a0 ** (1.0 / RGLRU_C)
    c_lambda = jnp.log(sg) - jnp.log1p(-sg)
    c_w_out = nrm((N_ODD, C_WIDTH, D_MODEL), C_WIDTH ** -0.5)
    ffn_w_gate = nrm((N_EVEN, D_MODEL, FFN_DIM), D_MODEL ** -0.5)
    ffn_w_up = nrm((N_EVEN, D_MODEL, FFN_DIM), D_MODEL ** -0.5)
    ffn_w_down = nrm((N_EVEN, FFN_DIM, D_MODEL), FFN_DIM ** -0.5)
    moe_router = nrm((N_ODD, D_MODEL, N_EXPERTS), D_MODEL ** -0.5)
    moe_w_gate = nrm((N_ODD, N_EXPERTS, D_MODEL, EXPERT_DIM), D_MODEL ** -0.5)
    moe_w_up = nrm((N_ODD, N_EXPERTS, D_MODEL, EXPERT_DIM), D_MODEL ** -0.5)
    moe_w_down = nrm((N_ODD, N_EXPERTS, EXPERT_DIM, D_MODEL), EXPERT_DIM ** -0.5)
    ple_w_proj = nrm((DEPTH, PLE_DIM, D_MODEL), PLE_DIM ** -0.5)
    ple_w_gate = nrm((DEPTH, D_MODEL, D_MODEL), D_MODEL ** -0.5)
    return {"x": x, "p": p, "ln_mix": ln_mix, "ln_ffn": ln_ffn, "ln_ple": ln_ple,
            "ln_final": ln_final, "lb_table": lb_table, "ab_w_in": ab_w_in, "ab_conv": ab_conv,
            "b_a_log": b_a_log, "b_dt_bias": b_dt_bias, "a_gnorm": a_gnorm, "b_gnorm": b_gnorm,
            "ab_w_out": ab_w_out, "c_w_in": c_w_in, "c_conv_w": c_conv_w, "c_conv_b": c_conv_b,
            "c_w_r": c_w_r, "c_b_r": c_b_r, "c_w_i": c_w_i, "c_b_i": c_b_i, "c_lambda": c_lambda,
            "c_w_out": c_w_out, "ffn_w_gate": ffn_w_gate, "ffn_w_up": ffn_w_up,
            "ffn_w_down": ffn_w_down, "moe_router": moe_router, "moe_w_gate": moe_w_gate,
            "moe_w_up": moe_w_up, "moe_w_down": moe_w_down, "ple_w_proj": ple_w_proj,
            "ple_w_gate": ple_w_gate}


def reference(x, p, ln_mix, ln_ffn, ln_ple, ln_final, lb_table, ab_w_in, ab_conv, b_a_log,
              b_dt_bias, a_gnorm, b_gnorm, ab_w_out, c_w_in, c_conv_w, c_conv_b, c_w_r, c_b_r,
              c_w_i, c_b_i, c_lambda, c_w_out, ffn_w_gate, ffn_w_up, ffn_w_down, moe_router,
              moe_w_gate, moe_w_up, moe_w_down, ple_w_proj, ple_w_gate):
    lb_all = jnp.cumsum(jax.nn.softmax(lb_table.astype(jnp.float32), axis=0), axis=0)
    h = x
    for layer in range(DEPTH):
        j = layer // 2
        u = rmsnorm(h, ln_mix[layer])
        if layer % 2 == 0:
            h = h + even_mixer(u, ab_w_in[j], lb_all[layer], ab_conv[j], b_a_log[j], b_dt_bias[j],
                               a_gnorm[j], b_gnorm[j], ab_w_out[j])
            u = rmsnorm(h, ln_ffn[layer])
            h = h + swiglu(u, ffn_w_gate[j], ffn_w_up[j], ffn_w_down[j])
        else:
            h = h + odd_mixer(u, c_w_in[j], c_conv_w[j], c_conv_b[j], c_w_r[j], c_b_r[j], c_w_i[j],
                              c_b_i[j], c_lambda[j], c_w_out[j])
            u = rmsnorm(h, ln_ffn[layer])
            h = h + moe_swiglu(u, moe_router[j], moe_w_gate[j], moe_w_up[j], moe_w_down[j])
        gate = jax.nn.sigmoid(rmsnorm(h, ln_ple[layer]) @ ple_w_gate[layer])
        h = h + gate * (p[layer] @ ple_w_proj[layer])
    return rmsnorm(h, ln_final)
```

```python
import functools

import jax
import jax.numpy as jnp
from jax import lax
from jax.experimental import pallas as pl
from jax.experimental.pallas import tpu as pltpu

F32 = jnp.float32
BF16 = jnp.bfloat16
EPS = 1e-6
CHUNK = 64
SUB = 16
HEAD_DIM = 128
CONV_WIDTH = 4
CONV_TAIL = 8
RGLRU_C = 8.0
TOP_K = 2
VMEM_LIMIT = 56 * 1024 * 1024


def _cparams(sem, vmem=VMEM_LIMIT):
    return pltpu.CompilerParams(dimension_semantics=sem, vmem_limit_bytes=vmem)


def _dot(a, b):
    return jnp.dot(a, b, preferred_element_type=F32)


def _dot_nt(a, b):
    return lax.dot_general(a, b, (((1,), (1,)), ((), ())), preferred_element_type=F32)


def _dot_tn(a, b):
    return lax.dot_general(a, b, (((0,), (0,)), ((), ())), preferred_element_type=F32)


def _split(a):
    hi = a.astype(BF16)
    lo = (a - hi.astype(F32)).astype(BF16)
    return hi, lo


def _dot3(a, b):
    ah, al = _split(a)
    bh, bl = _split(b)
    return _dot(ah, bh) + _dot(ah, bl) + _dot(al, bh)


def _rms(x, g):
    return x * lax.rsqrt(jnp.mean(x * x, axis=-1, keepdims=True) + EPS) * g


def _sigmoid(x):
    return 1.0 / (1.0 + jnp.exp(-x))


def _silu(x):
    return x * _sigmoid(x)


def _norm_proj_kernel(h_ref, g_ref, w_ref, ws_ref, o_ref, os_ref, un_ref):
    j = pl.program_id(1)

    @pl.when(j == 0)
    def _():
        un = _rms(h_ref[...], g_ref[...]).astype(BF16)
        un_ref[...] = un
        os_ref[...] = _dot(un, ws_ref[...])

    o_ref[...] = _dot(un_ref[...], w_ref[...])


def norm_proj(h, g, w, ws, tm=512, tn=512):
    t, d = h.shape
    n = w.shape[1]
    return pl.pallas_call(
        _norm_proj_kernel,
        grid=(t // tm, n // tn),
        in_specs=[pl.BlockSpec((tm, d), lambda i, j: (i, 0)),
                  pl.BlockSpec((1, d), lambda i, j: (0, 0)),
                  pl.BlockSpec((d, tn), lambda i, j: (0, j)),
                  pl.BlockSpec((d, ws.shape[1]), lambda i, j: (0, 0))],
        out_specs=[pl.BlockSpec((tm, tn), lambda i, j: (i, j)),
                   pl.BlockSpec((tm, ws.shape[1]), lambda i, j: (i, 0))],
        out_shape=[jax.ShapeDtypeStruct((t, n), F32), jax.ShapeDtypeStruct((t, ws.shape[1]), F32)],
        scratch_shapes=[pltpu.VMEM((tm, d), BF16)],
        compiler_params=_cparams(("parallel", "arbitrary")),
        name="norm_proj",
    )(h, g.reshape(1, d), w, ws)


def _norm_proj_gelu_kernel(h_ref, g_ref, w_ref, o_ref, y_ref, un_ref, *, n_gelu_tiles):
    j = pl.program_id(1)

    @pl.when(j == 0)
    def _():
        un_ref[...] = _rms(h_ref[...], g_ref[...]).astype(BF16)

    acc = _dot(un_ref[...], w_ref[...])

    @pl.when(j < n_gelu_tiles)
    def _():
        y_ref[...] = jax.nn.gelu(acc).astype(BF16)

    @pl.when(j >= n_gelu_tiles)
    def _():
        o_ref[...] = acc


def norm_proj_gelu(h, g, w, tm=512, tn=512):
    t, d = h.shape
    n = w.shape[1]
    half = n // 2
    nh = half // tn
    return pl.pallas_call(
        functools.partial(_norm_proj_gelu_kernel, n_gelu_tiles=nh),
        grid=(t // tm, n // tn),
        in_specs=[pl.BlockSpec((tm, d), lambda i, j: (i, 0)),
                  pl.BlockSpec((1, d), lambda i, j: (0, 0)),
                  pl.BlockSpec((d, tn), lambda i, j: (0, j))],
        out_specs=[pl.BlockSpec((tm, tn), lambda i, j: (i, jnp.maximum(j - nh, 0))),
                   pl.BlockSpec((tm, tn), lambda i, j: (i, jnp.minimum(j, nh - 1)))],
        out_shape=[jax.ShapeDtypeStruct((t, half), F32), jax.ShapeDtypeStruct((t, half), BF16)],
        scratch_shapes=[pltpu.VMEM((tm, d), BF16)],
        compiler_params=_cparams(("parallel", "arbitrary")),
        name="norm_proj_gelu",
    )(h, g.reshape(1, d), w)


def _hgrn2_kernel(q_ref, f_ref, i_ref, g_ref, lbt_ref, gn_ref, o_ref, st_ref, *, layer, heads):
    c = CHUNK

    @pl.when(pl.program_id(1) == 0)
    def _():
        st_ref[...] = jnp.zeros_like(st_ref)

    lbt = lbt_ref[...]
    e = jnp.exp(lbt - jnp.max(lbt, axis=0, keepdims=True))
    lb_all = jnp.sum(e[:layer + 1], axis=0, keepdims=True) / jnp.sum(e, axis=0, keepdims=True)

    row = lax.broadcasted_iota(jnp.int32, (c, c), 0)
    col = lax.broadcasted_iota(jnp.int32, (c, c), 1)
    tril = jnp.where(row >= col, 1.0, 0.0).astype(BF16)
    rsub = lax.broadcasted_iota(jnp.int32, (c, HEAD_DIM), 0) & (SUB - 1)
    gn = gn_ref[...]

    for h in range(heads):
        sl = slice(h * HEAD_DIM, (h + 1) * HEAD_DIM)
        lb = lb_all[:, sl]
        q = q_ref[:, sl] * (HEAD_DIM ** -0.5)
        forget = lb + (1.0 - lb) * _sigmoid(f_ref[:, sl])
        k = 1.0 - forget
        v = i_ref[:, sl]
        logf = jnp.log(forget)
        lh, ll = _split(logf)
        b = _dot(tril, lh) + _dot(tril, ll)
        st = st_ref[h]

        o = _dot_nt((q * jnp.exp(b)).astype(BF16), st.astype(BF16))
        parts = [jnp.zeros((SUB, HEAD_DIM), F32)]
        for blk in range(1, c // SUB):
            lo = blk * SUB
            bref = b[lo - 1:lo, :]
            qi = (q[lo:lo + SUB] * jnp.exp(b[lo:lo + SUB] - bref)).astype(BF16)
            kj = (k[:lo] * jnp.exp(bref - b[:lo])).astype(BF16)
            s = _dot_nt(qi, kj)
            parts.append(_dot(s.astype(BF16), v[:lo].astype(BF16)))
        o = o + jnp.concatenate(parts, axis=0)
        for d in range(SUB):
            if d == 0:
                w = jnp.sum(q * k, axis=1, keepdims=True)
                o = o + w * v
            else:
                bs = pltpu.roll(b, d, 0)
                ks = pltpu.roll(k, d, 0)
                vs = pltpu.roll(v, d, 0)
                valid = rsub >= d
                dec = jnp.exp(jnp.where(valid, b - bs, -jnp.inf))
                w = jnp.sum(q * ks * dec, axis=1, keepdims=True)
                o = o + w * vs

        blast = b[c - 1:c, :]
        kd = (k * jnp.exp(blast - b)).astype(BF16)
        st_ref[h] = st * jnp.exp(blast) + _dot_tn(v.astype(BF16), kd)

        on = o * lax.rsqrt(jnp.mean(o * o, axis=1, keepdims=True) + EPS) * gn
        o_ref[:, sl] = (on * _silu(g_ref[:, sl])).astype(BF16)


def hgrn2(proj, lb_table, gnorm, *, bsz, seq, heads, layer, col0):
    t = proj.shape[0]
    hw = heads * HEAD_DIM
    nc = seq // CHUNK
    cb = col0 // hw

    def spec(k):
        return pl.BlockSpec((CHUNK, hw), lambda b, s: (b * nc + s, cb + k))

    return pl.pallas_call(
        functools.partial(_hgrn2_kernel, layer=layer, heads=heads),
        grid=(bsz, nc),
        in_specs=[spec(0), spec(1), spec(2), spec(3),
                  pl.BlockSpec(lb_table.shape, lambda b, s: (0, 0)),
                  pl.BlockSpec((1, HEAD_DIM), lambda b, s: (0, 0))],
        out_specs=pl.BlockSpec((CHUNK, hw), lambda b, s: (b * nc + s, 0)),
        out_shape=jax.ShapeDtypeStruct((t, hw), BF16),
        scratch_shapes=[pltpu.VMEM((heads, HEAD_DIM, HEAD_DIM), F32)],
        compiler_params=_cparams(("parallel", "arbitrary")),
        name="hgrn2",
    )(proj, proj, proj, proj, lb_table, gnorm.reshape(1, HEAD_DIM))


def _causal_conv(ext_ref, x, w, first):
    n = x.shape[0]

    @pl.when(first)
    def _():
        ext_ref[0:CONV_TAIL, :] = jnp.zeros((CONV_TAIL, x.shape[1]), F32)

    ext_ref[CONV_TAIL:CONV_TAIL + n, :] = x
    y = x * w[CONV_WIDTH - 1:CONV_WIDTH, :]
    for k in range(CONV_WIDTH - 1):
        off = CONV_TAIL - (CONV_WIDTH - 1) + k
        y = y + ext_ref[off:off + n, :] * w[k:k + 1, :]
    ext_ref[0:CONV_TAIL, :] = ext_ref[n:n + CONV_TAIL, :]
    return y


def _gdn_kernel(q_ref, k_ref, v_ref, z_ref, sm_ref, cw_ref, alog_ref, dtb_ref, gn_ref, o_ref,
                st_ref, eq_ref, ek_ref, ev_ref, *, heads):
    c = CHUNK
    first = pl.program_id(1) == 0

    @pl.when(first)
    def _():
        st_ref[...] = jnp.zeros_like(st_ref)

    hw = heads * HEAD_DIM
    cw = cw_ref[...]
    qc = _silu(_causal_conv(eq_ref, q_ref[...], cw[:, 0:hw], first))
    kc = _silu(_causal_conv(ek_ref, k_ref[...], cw[:, hw:2 * hw], first))
    vc = _silu(_causal_conv(ev_ref, v_ref[...], cw[:, 2 * hw:3 * hw], first))

    row = lax.broadcasted_iota(jnp.int32, (c, c), 0)
    col = lax.broadcasted_iota(jnp.int32, (c, c), 1)
    causal = row >= col
    strict = row > col
    eye = row == col
    eye_f = jnp.where(eye, 1.0, 0.0)
    sm = sm_ref[...]
    gn = gn_ref[...]

    def to_row(colv):
        return jnp.sum(jnp.where(eye, colv, 0.0), axis=0, keepdims=True)

    for h in range(heads):
        sl = slice(h * HEAD_DIM, (h + 1) * HEAD_DIM)
        qh, kh, vh = qc[:, sl], kc[:, sl], vc[:, sl]
        qn = qh * lax.rsqrt(jnp.sum(qh * qh, axis=1, keepdims=True) + EPS) * (HEAD_DIM ** -0.5)
        kn = kh * lax.rsqrt(jnp.sum(kh * kh, axis=1, keepdims=True) + EPS)
        a_col = sm[:, h:h + 1]
        beta = _sigmoid(sm[:, heads + h:heads + h + 1])
        g_col = -jnp.exp(alog_ref[:, h:h + 1]) * jax.nn.softplus(a_col + dtb_ref[:, h:h + 1])
        g_row = to_row(g_col)
        gam_col = jnp.sum(jnp.where(causal, g_row, 0.0), axis=1, keepdims=True)
        gam_row = jnp.sum(jnp.where(strict, 0.0, g_col), axis=0, keepdims=True)
        decay = jnp.exp(jnp.where(causal, gam_col - gam_row, -jnp.inf))

        kbeta = kn * beta
        knb = kn.astype(BF16)
        a_mat = jnp.where(strict, _dot_nt(kbeta.astype(BF16), knb) * decay, 0.0)
        x = eye_f - a_mat
        p = _dot3(a_mat, a_mat)
        n_sq = (c - 1).bit_length() - 1
        for lvl in range(n_sq):
            if lvl < n_sq - 1:
                y = _dot3(jnp.concatenate([x, p], axis=0), p)
                x = x + y[:c]
                p = y[c:]
            else:
                x = x + _dot3(x, p)
        egam = jnp.exp(gam_col)
        rhs = jnp.concatenate([vh * beta, kbeta * egam], axis=1).astype(BF16)
        uw = _dot(x.astype(BF16), rhs)
        u, w = uw[:, :HEAD_DIM], uw[:, HEAD_DIM:]
        qk = _dot_nt(qn.astype(BF16), knb) * decay

        st = st_ref[h]
        stb = st.astype(BF16)
        v_new = u - _dot_nt(w.astype(BF16), stb)
        o = _dot_nt((qn * egam).astype(BF16), stb) + _dot(qk.astype(BF16), v_new.astype(BF16))
        glast = gam_col[c - 1:c, :]
        kd = (kn * jnp.exp(glast - gam_col)).astype(BF16)
        st_ref[h] = st * jnp.exp(glast) + _dot_tn(v_new.astype(BF16), kd)

        on = o * lax.rsqrt(jnp.mean(o * o, axis=1, keepdims=True) + EPS) * gn
        o_ref[:, sl] = (on * _silu(z_ref[:, sl])).astype(BF16)


def gdn(proj, small, conv_w, a_log, dt_bias, gnorm, *, bsz, seq, heads, col0):
    t = proj.shape[0]
    hw = heads * HEAD_DIM
    nc = seq // CHUNK
    cb = col0 // hw

    def spec(k):
        return pl.BlockSpec((CHUNK, hw), lambda b, s: (b * nc + s, cb + k))

    def const(shape):
        return pl.BlockSpec(shape, lambda b, s: (0, 0))

    return pl.pallas_call(
        functools.partial(_gdn_kernel, heads=heads),
        grid=(bsz, nc),
        in_specs=[spec(0), spec(1), spec(2), spec(3),
                  pl.BlockSpec((CHUNK, small.shape[1]), lambda b, s: (b * nc + s, 0)),
                  const(conv_w.shape), const((1, heads)), const((1, heads)), const((1, HEAD_DIM))],
        out_specs=pl.BlockSpec((CHUNK, hw), lambda b, s: (b * nc + s, 0)),
        out_shape=jax.ShapeDtypeStruct((t, hw), BF16),
        scratch_shapes=[pltpu.VMEM((heads, HEAD_DIM, HEAD_DIM), F32)]
        + [pltpu.VMEM((CHUNK + CONV_TAIL, hw), F32)] * 3,
        compiler_params=_cparams(("parallel", "arbitrary")),
        name="gdn",
    )(proj, proj, proj, proj, small, conv_w, a_log.reshape(1, heads), dt_bias.reshape(1, heads),
      gnorm.reshape(1, HEAD_DIM))


def _out_proj2_kernel(a_ref, b_ref, wa_ref, wb_ref, h_ref, o_ref):
    o_ref[...] = h_ref[...] + _dot(a_ref[...], wa_ref[...]) + _dot(b_ref[...], wb_ref[...])


def out_proj2(a, b, w, h, tm=512, tn=512):
    t, ka = a.shape
    kb = b.shape[1]
    n = w.shape[1]
    return pl.pallas_call(
        _out_proj2_kernel,
        grid=(t // tm, n // tn),
        in_specs=[pl.BlockSpec((tm, ka), lambda i, j: (i, 0)),
                  pl.BlockSpec((tm, kb), lambda i, j: (i, 0)),
                  pl.BlockSpec((ka, tn), lambda i, j: (0, j)),
                  pl.BlockSpec((kb, tn), lambda i, j: (ka // kb, j)),
                  pl.BlockSpec((tm, tn), lambda i, j: (i, j))],
        out_specs=pl.BlockSpec((tm, tn), lambda i, j: (i, j)),
        out_shape=jax.ShapeDtypeStruct((t, n), F32),
        compiler_params=_cparams(("parallel", "arbitrary")),
        name="out_proj2",
    )(a, b, w, w, h)


def _out_proj_kernel(a_ref, w_ref, h_ref, o_ref):
    o_ref[...] = h_ref[...] + _dot(a_ref[...], w_ref[...])


def out_proj(a, w, h, tm=512, tn=512):
    t, k = a.shape
    n = w.shape[1]
    return pl.pallas_call(
        _out_proj_kernel,
        grid=(t // tm, n // tn),
        in_specs=[pl.BlockSpec((tm, k), lambda i, j: (i, 0)),
                  pl.BlockSpec((k, tn), lambda i, j: (0, j)),
                  pl.BlockSpec((tm, tn), lambda i, j: (i, j))],
        out_specs=pl.BlockSpec((tm, tn), lambda i, j: (i, j)),
        out_shape=jax.ShapeDtypeStruct((t, n), F32),
        compiler_params=_cparams(("parallel", "arbitrary")),
        name="out_proj",
    )(a, w, h)


def _swiglu_kernel(h_ref, g_ref, wg_ref, wu_ref, wd_ref, o_ref, un_ref, acc_ref):
    f = pl.program_id(1)

    @pl.when(f == 0)
    def _():
        un_ref[...] = _rms(h_ref[...], g_ref[...]).astype(BF16)
        acc_ref[...] = h_ref[...]

    un = un_ref[...]
    hb = (_silu(_dot(un, wg_ref[...])) * _dot(un, wu_ref[...])).astype(BF16)
    acc_ref[...] += _dot(hb, wd_ref[...])

    @pl.when(f == pl.num_programs(1) - 1)
    def _():
        o_ref[...] = acc_ref[...]


def swiglu(h, g, wg, wu, wd, tm=512, tf=512):
    t, d = h.shape
    ff = wg.shape[1]
    return pl.pallas_call(
        _swiglu_kernel,
        grid=(t // tm, ff // tf),
        in_specs=[pl.BlockSpec((tm, d), lambda i, f: (i, 0)),
                  pl.BlockSpec((1, d), lambda i, f: (0, 0)),
                  pl.BlockSpec((d, tf), lambda i, f: (0, f)),
                  pl.BlockSpec((d, tf), lambda i, f: (0, f)),
                  pl.BlockSpec((tf, d), lambda i, f: (f, 0))],
        out_specs=pl.BlockSpec((tm, d), lambda i, f: (i, 0)),
        out_shape=jax.ShapeDtypeStruct((t, d), F32),
        scratch_shapes=[pltpu.VMEM((tm, d), BF16), pltpu.VMEM((tm, d), F32)],
        compiler_params=_cparams(("parallel", "arbitrary")),
        name="swiglu",
    )(h, g.reshape(1, d), wg, wu, wd)


def _ple_kernel(*refs, has_add, has_final):
    h_ref, p_ref, g_ref, wg_ref, wp_ref = refs[:5]
    k = 5
    add_ref = gf_ref = None
    if has_add:
        add_ref = refs[k]
        k += 1
    if has_final:
        gf_ref = refs[k]
        k += 1
    o_ref = refs[k]
    h = h_ref[...]
    if has_add:
        h = h + add_ref[...]
    un = _rms(h, g_ref[...]).astype(BF16)
    gate = _sigmoid(_dot(un, wg_ref[...]))
    out = h + gate * _dot(p_ref[...].astype(BF16), wp_ref[...])
    if has_final:
        out = _rms(out, gf_ref[...])
    o_ref[...] = out


def ple(h, p, g, wg, wp, add=None, g_final=None, tm=256):
    t, d = h.shape
    pd = p.shape[1]
    row = lambda i: (i, 0)
    const = lambda i: (0, 0)
    in_specs = [pl.BlockSpec((tm, d), row), pl.BlockSpec((tm, pd), row), pl.BlockSpec((1, d), const),
                pl.BlockSpec((d, d), const), pl.BlockSpec((pd, d), const)]
    args = [h, p, g.reshape(1, d), wg, wp]
    if add is not None:
        in_specs.append(pl.BlockSpec((tm, d), row))
        args.append(add)
    if g_final is not None:
        in_specs.append(pl.BlockSpec((1, d), const))
        args.append(g_final.reshape(1, d))
    return pl.pallas_call(
        functools.partial(_ple_kernel, has_add=add is not None, has_final=g_final is not None),
        grid=(t // tm,),
        in_specs=in_specs,
        out_specs=pl.BlockSpec((tm, d), row),
        out_shape=jax.ShapeDtypeStruct((t, d), F32),
        compiler_params=_cparams(("parallel",)),
        name="ple",
    )(*args)


def _rglru_kernel(x_ref, y_ref, cw_ref, cb_ref, wr_ref, br_ref, wi_ref, bi_ref, lam_ref, o_ref,
                  ext_ref, hc_ref, *, blocks):
    n = x_ref.shape[0]
    first = pl.program_id(1) == 0

    @pl.when(first)
    def _():
        hc_ref[...] = jnp.zeros_like(hc_ref)

    xc = _causal_conv(ext_ref, x_ref[...], cw_ref[...], first) + cb_ref[...]
    bw = xc.shape[1] // blocks
    rowi = lax.broadcasted_iota(jnp.int32, (n, bw), 0)
    at_start = jnp.logical_and(first, rowi == 0)

    for nb in range(blocks):
        sl = slice(nb * bw, (nb + 1) * bw)
        xb = xc[:, sl]
        xbb = xb.astype(BF16)
        r = _sigmoid(_dot(xbb, wr_ref[nb]) + br_ref[:, sl])
        gi = _sigmoid(_dot(xbb, wi_ref[nb]) + bi_ref[:, sl])
        log_a = -RGLRU_C * r * jax.nn.softplus(-lam_ref[:, sl])
        a = jnp.exp(log_a)
        mult = jnp.sqrt(jnp.maximum(1.0 - a * a, 0.0))
        mult = jnp.where(at_start, 1.0, mult)
        b = mult * gi * xb
        sh = 1
        while sh < n:
            ok = rowi >= sh
            a_prev = jnp.where(ok, pltpu.roll(a, sh, 0), 1.0)
            b_prev = jnp.where(ok, pltpu.roll(b, sh, 0), 0.0)
            b = b + a * b_prev
            a = a * a_prev
            sh *= 2
        hseq = b + a * hc_ref[:, sl]
        hc_ref[:, sl] = hseq[n - 1:n, :]
        o_ref[:, sl] = (hseq * y_ref[:, sl].astype(F32)).astype(BF16)


def rglru(xr, y, conv_w, conv_b, w_r, b_r, w_i, b_i, lam, *, bsz, seq, rows=256):
    t, cwid = xr.shape
    blocks = w_r.shape[0]
    ns = seq // rows
    row = lambda b, s: (b * ns + s, 0)
    c2 = lambda b, s: (0, 0)
    c3 = lambda b, s: (0, 0, 0)
    vec = lambda a: a.reshape(1, cwid)
    return pl.pallas_call(
        functools.partial(_rglru_kernel, blocks=blocks),
        grid=(bsz, ns),
        in_specs=[pl.BlockSpec((rows, cwid), row), pl.BlockSpec((rows, cwid), row),
                  pl.BlockSpec(conv_w.shape, c2), pl.BlockSpec((1, cwid), c2),
                  pl.BlockSpec(w_r.shape, c3), pl.BlockSpec((1, cwid), c2),
                  pl.BlockSpec(w_i.shape, c3), pl.BlockSpec((1, cwid), c2),
                  pl.BlockSpec((1, cwid), c2)],
        out_specs=pl.BlockSpec((rows, cwid), row),
        out_shape=jax.ShapeDtypeStruct((t, cwid), BF16),
        scratch_shapes=[pltpu.VMEM((rows + CONV_TAIL, cwid), F32), pltpu.VMEM((1, cwid), F32)],
        compiler_params=_cparams(("parallel", "arbitrary")),
        name="rglru",
    )(xr, y, conv_w, vec(conv_b), w_r, vec(b_r), w_i, vec(b_i), vec(lam))


def _router_kernel(h_ref, g_ref, wr_ref, un_ref, pos_ref, gate_ref, cnt_ref):
    tm = h_ref.shape[0]
    ne = wr_ref.shape[0]
    un = _rms(h_ref[...], g_ref[...])
    uh, ul = _split(un)
    un_ref[...] = uh
    wh, wl = _split(wr_ref[...])
    logits = _dot_nt(wh, uh) + _dot_nt(wh, ul) + _dot_nt(wl, uh)
    eidx = lax.broadcasted_iota(jnp.int32, (ne, tm), 0).astype(F32)
    m1 = jnp.max(logits, axis=0, keepdims=True)
    i1 = jnp.min(jnp.where(logits == m1, eidx, float(ne)), axis=0, keepdims=True)
    mask1 = eidx == i1
    rest = jnp.where(mask1, -jnp.inf, logits)
    m2 = jnp.max(rest, axis=0, keepdims=True)
    i2 = jnp.min(jnp.where(rest == m2, eidx, float(ne)), axis=0, keepdims=True)
    mask2 = eidx == i2
    e2 = jnp.exp(m2 - m1)
    g1 = 1.0 / (1.0 + e2)
    g2 = e2 / (1.0 + e2)
    gate_ref[...] = jnp.where(mask1, g1, jnp.where(mask2, g2, 0.0))
    sel = jnp.logical_or(mask1, mask2)
    self32 = jnp.where(sel, 1.0, 0.0)
    before = lax.broadcasted_iota(jnp.int32, (tm, tm), 0) < lax.broadcasted_iota(jnp.int32, (tm, tm), 1)
    rank = _dot(self32.astype(BF16), jnp.where(before, 1.0, 0.0).astype(BF16))
    pos_ref[...] = jnp.where(sel, rank, -1.0)
    cnt = jnp.sum(self32, axis=1, keepdims=True).astype(jnp.int32)
    cnt_ref[0] = jnp.broadcast_to(cnt, cnt_ref.shape[1:])


def router(h, g, wr_t, tm):
    t, d = h.shape
    ne = wr_t.shape[0]
    nt = t // tm
    return pl.pallas_call(
        _router_kernel,
        grid=(nt,),
        in_specs=[pl.BlockSpec((tm, d), lambda i: (i, 0)),
                  pl.BlockSpec((1, d), lambda i: (0, 0)),
                  pl.BlockSpec((ne, d), lambda i: (0, 0))],
        out_specs=[pl.BlockSpec((tm, d), lambda i: (i, 0)),
                   pl.BlockSpec((ne, tm), lambda i: (0, i)),
                   pl.BlockSpec((ne, tm), lambda i: (0, i)),
                   pl.BlockSpec((1, ne, 128), lambda i: (i, 0, 0))],
        out_shape=[jax.ShapeDtypeStruct((t, d), BF16), jax.ShapeDtypeStruct((ne, t), F32),
                   jax.ShapeDtypeStruct((ne, t), F32), jax.ShapeDtypeStruct((nt, ne, 128), jnp.int32)],
        compiler_params=_cparams(("parallel",)),
        name="moe_router",
    )(h, g.reshape(1, d), wr_t)


def _experts_kernel(cnt_ref, un_ref, pos_ref, gate_ref, wg_ref, wu_ref, wd_ref, o_ref, xg_ref, acc_ref, *, rs):
    i, e, f = pl.program_id(0), pl.program_id(1), pl.program_id(2)
    ne, nf = pl.num_programs(1), pl.num_programs(2)
    tm = un_ref.shape[0]
    nsub = (cnt_ref[i * ne + e] + rs - 1) // rs
    pos = pos_ref[0]
    slot = lax.broadcasted_iota(jnp.int32, (rs, tm), 0).astype(F32)

    def pick(j):
        return pos == slot + (j * rs).astype(F32)

    @pl.when(jnp.logical_and(e == 0, f == 0))
    def _():
        o_ref[...] = jnp.zeros_like(o_ref)

    @pl.when(f == 0)
    def _():
        def gather(j, carry):
            rows = pl.ds(pl.multiple_of(j * rs, rs), rs)
            sel = jnp.where(pick(j), 1.0, 0.0).astype(BF16)
            xg_ref[rows, :] = _dot(sel, un_ref[...]).astype(BF16)
            acc_ref[rows, :] = jnp.zeros((rs, acc_ref.shape[1]), F32)
            return carry
        lax.fori_loop(0, nsub, gather, 0)

    def ffn(j, carry):
        rows = pl.ds(pl.multiple_of(j * rs, rs), rs)
        x = xg_ref[rows, :]
        hb = (_silu(_dot(x, wg_ref[0])) * _dot(x, wu_ref[0])).astype(BF16)
        acc_ref[rows, :] += _dot(hb, wd_ref[0])
        return carry
    lax.fori_loop(0, nsub, ffn, 0)

    @pl.when(f == nf - 1)
    def _():
        gate = gate_ref[0]

        def scatter(j, carry):
            rows = pl.ds(pl.multiple_of(j * rs, rs), rs)
            hit = pick(j)
            gsub = jnp.sum(jnp.where(hit, gate, 0.0), axis=1, keepdims=True)
            yb = (acc_ref[rows, :] * gsub).astype(BF16)
            o_ref[...] += _dot_tn(jnp.where(hit, 1.0, 0.0).astype(BF16), yb)
            return carry
        lax.fori_loop(0, nsub, scatter, 0)


def experts(un, pos, gate, counts, wg, wu, wd, tm, tf=512, rs=128):
    t, d = un.shape
    ne, _, ff = wg.shape
    nt = t // tm
    grid_spec = pltpu.PrefetchScalarGridSpec(
        num_scalar_prefetch=1,
        grid=(nt, ne, ff // tf),
        in_specs=[pl.BlockSpec((tm, d), lambda i, e, f, c: (i, 0)),
                  pl.BlockSpec((1, 1, tm), lambda i, e, f, c: (e, 0, i)),
                  pl.BlockSpec((1, 1, tm), lambda i, e, f, c: (e, 0, i)),
                  pl.BlockSpec((1, d, tf), lambda i, e, f, c: (e, 0, f)),
                  pl.BlockSpec((1, d, tf), lambda i, e, f, c: (e, 0, f)),
                  pl.BlockSpec((1, tf, d), lambda i, e, f, c: (e, f, 0))],
        out_specs=pl.BlockSpec((tm, d), lambda i, e, f, c: (i, 0)),
        scratch_shapes=[pltpu.VMEM((tm, d), BF16), pltpu.VMEM((tm, d), F32)],
    )
    return pl.pallas_call(
        functools.partial(_experts_kernel, rs=rs),
        grid_spec=grid_spec,
        out_shape=jax.ShapeDtypeStruct((t, d), F32),
        compiler_params=_cparams(("parallel", "arbitrary", "arbitrary")),
        name="moe_experts",
    )(counts, un, pos.reshape(ne, 1, t), gate.reshape(ne, 1, t), wg, wu, wd)


def moe(h, g, w_router, wg, wu, wd, tm=1024):
    t = h.shape[0]
    tm = min(tm, t)
    un, pos, gate, cnt = router(h, g, w_router.T, tm)
    counts = cnt[:, :, 0].reshape(-1)
    return experts(un, pos, gate, counts, wg, wu, wd, tm)


def kernel(x, p, ln_mix, ln_ffn, ln_ple, ln_final, lb_table, ab_w_in, ab_conv, b_a_log, b_dt_bias, a_gnorm, b_gnorm, ab_w_out, c_w_in, c_conv_w, c_conv_b, c_w_r, c_b_r, c_w_i, c_b_i, c_lambda, c_w_out, ffn_w_gate, ffn_w_up, ffn_w_down, moe_router, moe_w_gate, moe_w_up, moe_w_down, ple_w_proj, ple_w_gate):
    bsz, seq, d = x.shape
    t = bsz * seq
    depth = ln_mix.shape[0]
    a_heads = lb_table.shape[1] // HEAD_DIM
    b_heads = b_a_log.shape[1]
    a_w = a_heads * HEAD_DIM
    b_w = b_heads * HEAD_DIM
    main_w = 4 * a_w + 4 * b_w
    bf = lambda a: a.astype(BF16)

    h = x.reshape(t, d)
    for layer in range(depth):
        j = layer // 2
        if layer % 2 == 0:
            w_in = ab_w_in[j]
            w_small = jnp.pad(w_in[:, main_w:], ((0, 0), (0, 128 - 2 * b_heads)))
            proj, small = norm_proj(h, ln_mix[layer], bf(w_in[:, :main_w]), bf(w_small))
            o_a = hgrn2(proj, lb_table, a_gnorm[j], bsz=bsz, seq=seq, heads=a_heads, layer=layer, col0=0)
            o_b = gdn(proj, small, ab_conv[j], b_a_log[j], b_dt_bias[j], b_gnorm[j],
                      bsz=bsz, seq=seq, heads=b_heads, col0=4 * a_w)
            h = out_proj2(o_a, o_b, bf(ab_w_out[j]), h)
            h = swiglu(h, ln_ffn[layer], bf(ffn_w_gate[j]), bf(ffn_w_up[j]), bf(ffn_w_down[j]))
            add = None
        else:
            xr, yb = norm_proj_gelu(h, ln_mix[layer], bf(c_w_in[j]))
            hy = rglru(xr, yb, c_conv_w[j], c_conv_b[j], bf(c_w_r[j]), c_b_r[j], bf(c_w_i[j]), c_b_i[j],
                       c_lambda[j], bsz=bsz, seq=seq)
            h = out_proj(hy, bf(c_w_out[j]), h)
            add = moe(h, ln_ffn[layer], moe_router[j], bf(moe_w_gate[j]), bf(moe_w_up[j]), bf(moe_w_down[j]))
        g_final = ln_final if layer == depth - 1 else None
        h = ple(h, p[layer].reshape(t, -1), ln_ple[layer], bf(ple_w_gate[layer]), bf(ple_w_proj[layer]),
                add=add, g_final=g_final)
    if depth == 0:
        raise ValueError("depth must be positive")
    return h.reshape(bsz, seq, d)
```

```python
import functools

import jax
import jax.numpy as jnp
from jax import lax
from jax.experimental import pallas as pl
from jax.experimental.pallas import tpu as pltpu

F32 = jnp.float32
BF16 = jnp.bfloat16
EPS = 1e-6
CHUNK = 64
SUB = 16
HEAD_DIM = 128
CONV_WIDTH = 4
CONV_TAIL = 8
RGLRU_C = 8.0
TOP_K = 2
VMEM_LIMIT = 56 * 1024 * 1024


def _cparams(sem, vmem=VMEM_LIMIT):
    return pltpu.CompilerParams(dimension_semantics=sem, vmem_limit_bytes=vmem)


def _dot(a, b):
    return jnp.dot(a, b, preferred_element_type=F32)


def _dot_nt(a, b):
    return lax.dot_general(a, b, (((1,), (1,)), ((), ())), preferred_element_type=F32)


def _dot_tn(a, b):
    return lax.dot_general(a, b, (((0,), (0,)), ((), ())), preferred_element_type=F32)


def _split(a):
    hi = a.astype(BF16)
    lo = (a - hi.astype(F32)).astype(BF16)
    return hi, lo


def _dot3(a, b):
    ah, al = _split(a)
    bh, bl = _split(b)
    return _dot(ah, bh) + _dot(ah, bl) + _dot(al, bh)


def _rms(x, g):
    return x * lax.rsqrt(jnp.mean(x * x, axis=-1, keepdims=True) + EPS) * g


def _sigmoid(x):
    return 1.0 / (1.0 + jnp.exp(-x))


def _silu(x):
    return x * _sigmoid(x)


def _norm_proj_kernel(h_ref, g_ref, w_ref, ws_ref, o_ref, os_ref, un_ref):
    j = pl.program_id(1)

    @pl.when(j == 0)
    def _():
        un = _rms(h_ref[...], g_ref[...]).astype(BF16)
        un_ref[...] = un
        os_ref[...] = _dot(un, ws_ref[...])

    o_ref[...] = _dot(un_ref[...], w_ref[...])


def norm_proj(h, g, w, ws, tm=1024, tn=512):
    t, d = h.shape
    tm = min(tm, t)
    n = w.shape[1]
    return pl.pallas_call(
        _norm_proj_kernel,
        grid=(t // tm, n // tn),
        in_specs=[pl.BlockSpec((tm, d), lambda i, j: (i, 0)),
                  pl.BlockSpec((1, d), lambda i, j: (0, 0)),
                  pl.BlockSpec((d, tn), lambda i, j: (0, j)),
                  pl.BlockSpec((d, ws.shape[1]), lambda i, j: (0, 0))],
        out_specs=[pl.BlockSpec((tm, tn), lambda i, j: (i, j)),
                   pl.BlockSpec((tm, ws.shape[1]), lambda i, j: (i, 0))],
        out_shape=[jax.ShapeDtypeStruct((t, n), F32), jax.ShapeDtypeStruct((t, ws.shape[1]), F32)],
        scratch_shapes=[pltpu.VMEM((tm, d), BF16)],
        compiler_params=_cparams(("parallel", "arbitrary")),
        name="norm_proj",
    )(h, g.reshape(1, d), w, ws)


def _norm_proj_gelu_kernel(h_ref, g_ref, w_ref, o_ref, y_ref, un_ref, *, n_gelu_tiles):
    j = pl.program_id(1)

    @pl.when(j == 0)
    def _():
        un_ref[...] = _rms(h_ref[...], g_ref[...]).astype(BF16)

    acc = _dot(un_ref[...], w_ref[...])

    @pl.when(j < n_gelu_tiles)
    def _():
        y_ref[...] = jax.nn.gelu(acc).astype(BF16)

    @pl.when(j >= n_gelu_tiles)
    def _():
        o_ref[...] = acc


def norm_proj_gelu(h, g, w, tm=1024, tn=512):
    t, d = h.shape
    tm = min(tm, t)
    n = w.shape[1]
    half = n // 2
    nh = half // tn
    return pl.pallas_call(
        functools.partial(_norm_proj_gelu_kernel, n_gelu_tiles=nh),
        grid=(t // tm, n // tn),
        in_specs=[pl.BlockSpec((tm, d), lambda i, j: (i, 0)),
                  pl.BlockSpec((1, d), lambda i, j: (0, 0)),
                  pl.BlockSpec((d, tn), lambda i, j: (0, j))],
        out_specs=[pl.BlockSpec((tm, tn), lambda i, j: (i, jnp.maximum(j - nh, 0))),
                   pl.BlockSpec((tm, tn), lambda i, j: (i, jnp.minimum(j, nh - 1)))],
        out_shape=[jax.ShapeDtypeStruct((t, half), F32), jax.ShapeDtypeStruct((t, half), BF16)],
        scratch_shapes=[pltpu.VMEM((tm, d), BF16)],
        compiler_params=_cparams(("parallel", "arbitrary")),
        name="norm_proj_gelu",
    )(h, g.reshape(1, d), w)


def _hgrn2_kernel(q_ref, f_ref, i_ref, g_ref, lbt_ref, gn_ref, o_ref, st_ref, *, layer, heads):
    c = CHUNK

    @pl.when(pl.program_id(1) == 0)
    def _():
        st_ref[...] = jnp.zeros_like(st_ref)

    lbt = lbt_ref[...]
    e = jnp.exp(lbt - jnp.max(lbt, axis=0, keepdims=True))
    lb_all = jnp.sum(e[:layer + 1], axis=0, keepdims=True) / jnp.sum(e, axis=0, keepdims=True)

    row = lax.broadcasted_iota(jnp.int32, (c, c), 0)
    col = lax.broadcasted_iota(jnp.int32, (c, c), 1)
    tril = jnp.where(row >= col, 1.0, 0.0).astype(BF16)
    rsub = lax.broadcasted_iota(jnp.int32, (c, HEAD_DIM), 0) & (SUB - 1)
    gn = gn_ref[...]

    for h in range(heads):
        sl = slice(h * HEAD_DIM, (h + 1) * HEAD_DIM)
        lb = lb_all[:, sl]
        q = q_ref[:, sl] * (HEAD_DIM ** -0.5)
        forget = lb + (1.0 - lb) * _sigmoid(f_ref[:, sl])
        k = 1.0 - forget
        v = i_ref[:, sl]
        logf = jnp.log(forget)
        lh, ll = _split(logf)
        b = _dot(tril, lh) + _dot(tril, ll)
        st = st_ref[h]

        o = _dot_nt((q * jnp.exp(b)).astype(BF16), st.astype(BF16))
        parts = [jnp.zeros((SUB, HEAD_DIM), F32)]
        for blk in range(1, c // SUB):
            lo = blk * SUB
            bref = b[lo - 1:lo, :]
            qi = (q[lo:lo + SUB] * jnp.exp(b[lo:lo + SUB] - bref)).astype(BF16)
            kj = (k[:lo] * jnp.exp(bref - b[:lo])).astype(BF16)
            s = _dot_nt(qi, kj)
            parts.append(_dot(s.astype(BF16), v[:lo].astype(BF16)))
        o = o + jnp.concatenate(parts, axis=0)
        for d in range(SUB):
            if d == 0:
                w = jnp.sum(q * k, axis=1, keepdims=True)
                o = o + w * v
            else:
                bs = pltpu.roll(b, d, 0)
                ks = pltpu.roll(k, d, 0)
                vs = pltpu.roll(v, d, 0)
                valid = rsub >= d
                dec = jnp.exp(jnp.where(valid, b - bs, -jnp.inf))
                w = jnp.sum(q * ks * dec, axis=1, keepdims=True)
                o = o + w * vs

        blast = b[c - 1:c, :]
        kd = (k * jnp.exp(blast - b)).astype(BF16)
        st_ref[h] = st * jnp.exp(blast) + _dot_tn(v.astype(BF16), kd)

        on = o * lax.rsqrt(jnp.mean(o * o, axis=1, keepdims=True) + EPS) * gn
        o_ref[:, sl] = (on * _silu(g_ref[:, sl])).astype(BF16)


def hgrn2(proj, lb_table, gnorm, *, bsz, seq, heads, layer, col0):
    t = proj.shape[0]
    hw = heads * HEAD_DIM
    nc = seq // CHUNK
    cb = col0 // hw

    def spec(k):
        return pl.BlockSpec((CHUNK, hw), lambda b, s: (b * nc + s, cb + k))

    return pl.pallas_call(
        functools.partial(_hgrn2_kernel, layer=layer, heads=heads),
        grid=(bsz, nc),
        in_specs=[spec(0), spec(1), spec(2), spec(3),
                  pl.BlockSpec(lb_table.shape, lambda b, s: (0, 0)),
                  pl.BlockSpec((1, HEAD_DIM), lambda b, s: (0, 0))],
        out_specs=pl.BlockSpec((CHUNK, hw), lambda b, s: (b * nc + s, 0)),
        out_shape=jax.ShapeDtypeStruct((t, hw), BF16),
        scratch_shapes=[pltpu.VMEM((heads, HEAD_DIM, HEAD_DIM), F32)],
        compiler_params=_cparams(("parallel", "arbitrary")),
        name="hgrn2",
    )(proj, proj, proj, proj, lb_table, gnorm.reshape(1, HEAD_DIM))


def _causal_conv(ext_ref, x, w, first):
    n = x.shape[0]

    @pl.when(first)
    def _():
        ext_ref[0:CONV_TAIL, :] = jnp.zeros((CONV_TAIL, x.shape[1]), F32)

    ext_ref[CONV_TAIL:CONV_TAIL + n, :] = x
    y = x * w[CONV_WIDTH - 1:CONV_WIDTH, :]
    for k in range(CONV_WIDTH - 1):
        off = CONV_TAIL - (CONV_WIDTH - 1) + k
        y = y + ext_ref[off:off + n, :] * w[k:k + 1, :]
    ext_ref[0:CONV_TAIL, :] = ext_ref[n:n + CONV_TAIL, :]
    return y


def _gdn_kernel(q_ref, k_ref, v_ref, z_ref, sm_ref, cw_ref, alog_ref, dtb_ref, gn_ref, o_ref,
                st_ref, eq_ref, ek_ref, ev_ref, *, heads):
    c = CHUNK
    first = pl.program_id(1) == 0

    @pl.when(first)
    def _():
        st_ref[...] = jnp.zeros_like(st_ref)

    hw = heads * HEAD_DIM
    cw = cw_ref[...]
    qc = _silu(_causal_conv(eq_ref, q_ref[...], cw[:, 0:hw], first))
    kc = _silu(_causal_conv(ek_ref, k_ref[...], cw[:, hw:2 * hw], first))
    vc = _silu(_causal_conv(ev_ref, v_ref[...], cw[:, 2 * hw:3 * hw], first))

    row = lax.broadcasted_iota(jnp.int32, (c, c), 0)
    col = lax.broadcasted_iota(jnp.int32, (c, c), 1)
    causal = row >= col
    strict = row > col
    eye = row == col
    eye_f = jnp.where(eye, 1.0, 0.0)
    sm = sm_ref[...]
    gn = gn_ref[...]

    def to_row(colv):
        return jnp.sum(jnp.where(eye, colv, 0.0), axis=0, keepdims=True)

    hs = range(heads)
    sls = [slice(h * HEAD_DIM, (h + 1) * HEAD_DIM) for h in hs]
    qn, kn, knb, beta, gam_col, egam, decay, kbeta = [], [], [], [], [], [], [], []
    for h in hs:
        qh, kh = qc[:, sls[h]], kc[:, sls[h]]
        qn.append(qh * lax.rsqrt(jnp.sum(qh * qh, axis=1, keepdims=True) + EPS) * (HEAD_DIM ** -0.5))
        kn.append(kh * lax.rsqrt(jnp.sum(kh * kh, axis=1, keepdims=True) + EPS))
        knb.append(kn[h].astype(BF16))
        beta.append(_sigmoid(sm[:, heads + h:heads + h + 1]))
        g_col = -jnp.exp(alog_ref[:, h:h + 1]) * jax.nn.softplus(sm[:, h:h + 1] + dtb_ref[:, h:h + 1])
        g_row = to_row(g_col)
        gam_col.append(jnp.sum(jnp.where(causal, g_row, 0.0), axis=1, keepdims=True))
        gam_row = jnp.sum(jnp.where(strict, 0.0, g_col), axis=0, keepdims=True)
        decay.append(jnp.exp(jnp.where(causal, gam_col[h] - gam_row, -jnp.inf)))
        egam.append(jnp.exp(gam_col[h]))
        kbeta.append(kn[h] * beta[h])

    a_mat = [jnp.where(strict, _dot_nt(kbeta[h].astype(BF16), knb[h]) * decay[h], 0.0) for h in hs]
    qk = [(_dot_nt(qn[h].astype(BF16), knb[h]) * decay[h]).astype(BF16) for h in hs]
    x = [eye_f - a_mat[h] for h in hs]
    p = [_dot3(a_mat[h], a_mat[h]) for h in hs]
    n_sq = (c - 1).bit_length() - 1
    for lvl in range(n_sq):
        if lvl < n_sq - 1:
            y = [_dot3(jnp.concatenate([x[h], p[h]], axis=0), p[h]) for h in hs]
            x = [x[h] + y[h][:c] for h in hs]
            p = [y[h][c:] for h in hs]
        else:
            y = [_dot3(x[h], p[h]) for h in hs]
            x = [x[h] + y[h] for h in hs]
    rhs = [jnp.concatenate([vc[:, sls[h]] * beta[h], kbeta[h] * egam[h]], axis=1).astype(BF16) for h in hs]
    uw = [_dot(x[h].astype(BF16), rhs[h]) for h in hs]

    st = [st_ref[h] for h in hs]
    stb = [s.astype(BF16) for s in st]
    v_new = [uw[h][:, :HEAD_DIM] - _dot_nt(uw[h][:, HEAD_DIM:].astype(BF16), stb[h]) for h in hs]
    vnb = [v.astype(BF16) for v in v_new]
    o = [_dot_nt((qn[h] * egam[h]).astype(BF16), stb[h]) + _dot(qk[h], vnb[h]) for h in hs]
    for h in hs:
        glast = gam_col[h][c - 1:c, :]
        kd = (kn[h] * jnp.exp(glast - gam_col[h])).astype(BF16)
        st_ref[h] = st[h] * jnp.exp(glast) + _dot_tn(vnb[h], kd)
    for h in hs:
        on = o[h] * lax.rsqrt(jnp.mean(o[h] * o[h], axis=1, keepdims=True) + EPS) * gn
        o_ref[:, sls[h]] = (on * _silu(z_ref[:, sls[h]])).astype(BF16)


def gdn(proj, small, conv_w, a_log, dt_bias, gnorm, *, bsz, seq, heads, col0):
    t = proj.shape[0]
    hw = heads * HEAD_DIM
    nc = seq // CHUNK
    cb = col0 // hw

    def spec(k):
        return pl.BlockSpec((CHUNK, hw), lambda b, s: (b * nc + s, cb + k))

    def const(shape):
        return pl.BlockSpec(shape, lambda b, s: (0, 0))

    return pl.pallas_call(
        functools.partial(_gdn_kernel, heads=heads),
        grid=(bsz, nc),
        in_specs=[spec(0), spec(1), spec(2), spec(3),
                  pl.BlockSpec((CHUNK, small.shape[1]), lambda b, s: (b * nc + s, 0)),
                  const(conv_w.shape), const((1, heads)), const((1, heads)), const((1, HEAD_DIM))],
        out_specs=pl.BlockSpec((CHUNK, hw), lambda b, s: (b * nc + s, 0)),
        out_shape=jax.ShapeDtypeStruct((t, hw), BF16),
        scratch_shapes=[pltpu.VMEM((heads, HEAD_DIM, HEAD_DIM), F32)]
        + [pltpu.VMEM((CHUNK + CONV_TAIL, hw), F32)] * 3,
        compiler_params=_cparams(("parallel", "arbitrary")),
        name="gdn",
    )(proj, proj, proj, proj, small, conv_w, a_log.reshape(1, heads), dt_bias.reshape(1, heads),
      gnorm.reshape(1, HEAD_DIM))


def _out_proj2_kernel(a_ref, b_ref, wa_ref, wb_ref, h_ref, o_ref):
    o_ref[...] = h_ref[...] + _dot(a_ref[...], wa_ref[...]) + _dot(b_ref[...], wb_ref[...])


def out_proj2(a, b, w, h, tm=1024, tn=512):
    t, ka = a.shape
    tm = min(tm, t)
    kb = b.shape[1]
    n = w.shape[1]
    return pl.pallas_call(
        _out_proj2_kernel,
        grid=(t // tm, n // tn),
        in_specs=[pl.BlockSpec((tm, ka), lambda i, j: (i, 0)),
                  pl.BlockSpec((tm, kb), lambda i, j: (i, 0)),
                  pl.BlockSpec((ka, tn), lambda i, j: (0, j)),
                  pl.BlockSpec((kb, tn), lambda i, j: (ka // kb, j)),
                  pl.BlockSpec((tm, tn), lambda i, j: (i, j))],
        out_specs=pl.BlockSpec((tm, tn), lambda i, j: (i, j)),
        out_shape=jax.ShapeDtypeStruct((t, n), F32),
        compiler_params=_cparams(("parallel", "arbitrary")),
        name="out_proj2",
    )(a, b, w, w, h)


def _out_proj_kernel(a_ref, w_ref, h_ref, o_ref):
    o_ref[...] = h_ref[...] + _dot(a_ref[...], w_ref[...])


def out_proj(a, w, h, tm=1024, tn=512):
    t, k = a.shape
    tm = min(tm, t)
    n = w.shape[1]
    return pl.pallas_call(
        _out_proj_kernel,
        grid=(t // tm, n // tn),
        in_specs=[pl.BlockSpec((tm, k), lambda i, j: (i, 0)),
                  pl.BlockSpec((k, tn), lambda i, j: (0, j)),
                  pl.BlockSpec((tm, tn), lambda i, j: (i, j))],
        out_specs=pl.BlockSpec((tm, tn), lambda i, j: (i, j)),
        out_shape=jax.ShapeDtypeStruct((t, n), F32),
        compiler_params=_cparams(("parallel", "arbitrary")),
        name="out_proj",
    )(a, w, h)


def _swiglu_kernel(h_ref, g_ref, wg_ref, wu_ref, wd_ref, o_ref, un_ref):
    f = pl.program_id(1)

    @pl.when(f == 0)
    def _():
        un_ref[...] = _rms(h_ref[...], g_ref[...]).astype(BF16)
        o_ref[...] = h_ref[...]

    un = un_ref[...]
    hb = (_silu(_dot(un, wg_ref[...])) * _dot(un, wu_ref[...])).astype(BF16)
    o_ref[...] += _dot(hb, wd_ref[...])


def swiglu(h, g, wg, wu, wd, tm=1024, tf=512):
    t, d = h.shape
    tm = min(tm, t)
    ff = wg.shape[1]
    return pl.pallas_call(
        _swiglu_kernel,
        grid=(t // tm, ff // tf),
        in_specs=[pl.BlockSpec((tm, d), lambda i, f: (i, 0), pipeline_mode=pl.Buffered(1)),
                  pl.BlockSpec((1, d), lambda i, f: (0, 0)),
                  pl.BlockSpec((d, tf), lambda i, f: (0, f)),
                  pl.BlockSpec((d, tf), lambda i, f: (0, f)),
                  pl.BlockSpec((tf, d), lambda i, f: (f, 0))],
        out_specs=pl.BlockSpec((tm, d), lambda i, f: (i, 0)),
        out_shape=jax.ShapeDtypeStruct((t, d), F32),
        scratch_shapes=[pltpu.VMEM((tm, d), BF16)],
        compiler_params=_cparams(("parallel", "arbitrary")),
        name="swiglu",
    )(h, g.reshape(1, d), wg, wu, wd)


def _ple_kernel(*refs, has_add, has_final):
    h_ref, p_ref, g_ref, wg_ref, wp_ref = refs[:5]
    k = 5
    add_ref = gf_ref = None
    if has_add:
        add_ref = refs[k]
        k += 1
    if has_final:
        gf_ref = refs[k]
        k += 1
    o_ref = refs[k]
    h = h_ref[...]
    if has_add:
        h = h + add_ref[...]
    un = _rms(h, g_ref[...]).astype(BF16)
    gate = _sigmoid(_dot(un, wg_ref[...]))
    out = h + gate * _dot(p_ref[...].astype(BF16), wp_ref[...])
    if has_final:
        out = _rms(out, gf_ref[...])
    o_ref[...] = out


def ple(h, p, g, wg, wp, add=None, g_final=None, tm=512):
    t, d = h.shape
    tm = min(tm, t)
    pd = p.shape[1]
    row = lambda i: (i, 0)
    const = lambda i: (0, 0)
    in_specs = [pl.BlockSpec((tm, d), row), pl.BlockSpec((tm, pd), row), pl.BlockSpec((1, d), const),
                pl.BlockSpec((d, d), const), pl.BlockSpec((pd, d), const)]
    args = [h, p, g.reshape(1, d), wg, wp]
    if add is not None:
        in_specs.append(pl.BlockSpec((tm, d), row))
        args.append(add)
    if g_final is not None:
        in_specs.append(pl.BlockSpec((1, d), const))
        args.append(g_final.reshape(1, d))
    return pl.pallas_call(
        functools.partial(_ple_kernel, has_add=add is not None, has_final=g_final is not None),
        grid=(t // tm,),
        in_specs=in_specs,
        out_specs=pl.BlockSpec((tm, d), row),
        out_shape=jax.ShapeDtypeStruct((t, d), F32),
        compiler_params=_cparams(("parallel",)),
        name="ple",
    )(*args)


def _rglru_kernel(x_ref, y_ref, cw_ref, cb_ref, wr_ref, br_ref, wi_ref, bi_ref, lam_ref, o_ref,
                  ext_ref, hc_ref, *, blocks):
    n = x_ref.shape[0]
    first = pl.program_id(1) == 0

    @pl.when(first)
    def _():
        hc_ref[...] = jnp.zeros_like(hc_ref)

    xc = _causal_conv(ext_ref, x_ref[...], cw_ref[...], first) + cb_ref[...]
    bw = xc.shape[1] // blocks
    rowi = lax.broadcasted_iota(jnp.int32, (n, bw), 0)
    at_start = jnp.logical_and(first, rowi == 0)

    for nb in range(blocks):
        sl = slice(nb * bw, (nb + 1) * bw)
        xb = xc[:, sl]
        xbb = xb.astype(BF16)
        r = _sigmoid(_dot(xbb, wr_ref[nb]) + br_ref[:, sl])
        gi = _sigmoid(_dot(xbb, wi_ref[nb]) + bi_ref[:, sl])
        log_a = -RGLRU_C * r * jax.nn.softplus(-lam_ref[:, sl])
        a = jnp.exp(log_a)
        mult = jnp.sqrt(jnp.maximum(1.0 - a * a, 0.0))
        mult = jnp.where(at_start, 1.0, mult)
        b = mult * gi * xb
        sh = 1
        while sh < n:
            ok = rowi >= sh
            a_prev = jnp.where(ok, pltpu.roll(a, sh, 0), 1.0)
            b_prev = jnp.where(ok, pltpu.roll(b, sh, 0), 0.0)
            b = b + a * b_prev
            a = a * a_prev
            sh *= 2
        hseq = b + a * hc_ref[:, sl]
        hc_ref[:, sl] = hseq[n - 1:n, :]
        o_ref[:, sl] = (hseq * y_ref[:, sl].astype(F32)).astype(BF16)


def rglru(xr, y, conv_w, conv_b, w_r, b_r, w_i, b_i, lam, *, bsz, seq, rows=256):
    t, cwid = xr.shape
    blocks = w_r.shape[0]
    ns = seq // rows
    row = lambda b, s: (b * ns + s, 0)
    c2 = lambda b, s: (0, 0)
    c3 = lambda b, s: (0, 0, 0)
    vec = lambda a: a.reshape(1, cwid)
    return pl.pallas_call(
        functools.partial(_rglru_kernel, blocks=blocks),
        grid=(bsz, ns),
        in_specs=[pl.BlockSpec((rows, cwid), row), pl.BlockSpec((rows, cwid), row),
                  pl.BlockSpec(conv_w.shape, c2), pl.BlockSpec((1, cwid), c2),
                  pl.BlockSpec(w_r.shape, c3), pl.BlockSpec((1, cwid), c2),
                  pl.BlockSpec(w_i.shape, c3), pl.BlockSpec((1, cwid), c2),
                  pl.BlockSpec((1, cwid), c2)],
        out_specs=pl.BlockSpec((rows, cwid), row),
        out_shape=jax.ShapeDtypeStruct((t, cwid), BF16),
        scratch_shapes=[pltpu.VMEM((rows + CONV_TAIL, cwid), F32), pltpu.VMEM((1, cwid), F32)],
        compiler_params=_cparams(("parallel", "arbitrary")),
        name="rglru",
    )(xr, y, conv_w, vec(conv_b), w_r, vec(b_r), w_i, vec(b_i), vec(lam))


def _router_kernel(h_ref, g_ref, wr_ref, un_ref, pos_ref, gate_ref, cnt_ref):
    tm = h_ref.shape[0]
    ne = wr_ref.shape[0]
    un = _rms(h_ref[...], g_ref[...])
    uh, ul = _split(un)
    un_ref[...] = uh
    wh, wl = _split(wr_ref[...])
    logits = _dot_nt(wh, uh) + _dot_nt(wh, ul) + _dot_nt(wl, uh)
    eidx = lax.broadcasted_iota(jnp.int32, (ne, tm), 0).astype(F32)
    m1 = jnp.max(logits, axis=0, keepdims=True)
    i1 = jnp.min(jnp.where(logits == m1, eidx, float(ne)), axis=0, keepdims=True)
    mask1 = eidx == i1
    rest = jnp.where(mask1, -jnp.inf, logits)
    m2 = jnp.max(rest, axis=0, keepdims=True)
    i2 = jnp.min(jnp.where(rest == m2, eidx, float(ne)), axis=0, keepdims=True)
    mask2 = eidx == i2
    e2 = jnp.exp(m2 - m1)
    g1 = 1.0 / (1.0 + e2)
    g2 = e2 / (1.0 + e2)
    gate_ref[...] = jnp.where(mask1, g1, jnp.where(mask2, g2, 0.0))
    sel = jnp.logical_or(mask1, mask2)
    self32 = jnp.where(sel, 1.0, 0.0)
    before = lax.broadcasted_iota(jnp.int32, (tm, tm), 0) < lax.broadcasted_iota(jnp.int32, (tm, tm), 1)
    rank = _dot(self32.astype(BF16), jnp.where(before, 1.0, 0.0).astype(BF16))
    pos_ref[...] = jnp.where(sel, rank, -1.0)
    cnt = jnp.sum(self32, axis=1, keepdims=True).astype(jnp.int32)
    cnt_ref[0] = jnp.broadcast_to(cnt, cnt_ref.shape[1:])


def router(h, g, wr_t, tm):
    t, d = h.shape
    tm = min(tm, t)
    ne = wr_t.shape[0]
    nt = t // tm
    return pl.pallas_call(
        _router_kernel,
        grid=(nt,),
        in_specs=[pl.BlockSpec((tm, d), lambda i: (i, 0)),
                  pl.BlockSpec((1, d), lambda i: (0, 0)),
                  pl.BlockSpec((ne, d), lambda i: (0, 0))],
        out_specs=[pl.BlockSpec((tm, d), lambda i: (i, 0)),
                   pl.BlockSpec((ne, tm), lambda i: (0, i)),
                   pl.BlockSpec((ne, tm), lambda i: (0, i)),
                   pl.BlockSpec((1, ne, 128), lambda i: (i, 0, 0))],
        out_shape=[jax.ShapeDtypeStruct((t, d), BF16), jax.ShapeDtypeStruct((ne, t), F32),
                   jax.ShapeDtypeStruct((ne, t), F32), jax.ShapeDtypeStruct((nt, ne, 128), jnp.int32)],
        compiler_params=_cparams(("parallel",)),
        name="moe_router",
    )(h, g.reshape(1, d), wr_t)


def _experts_kernel(cnt_ref, un_ref, pos_ref, gate_ref, wg_ref, wu_ref, wd_ref, o_ref, xg_ref, acc_ref, *, rs):
    i, e, f = pl.program_id(0), pl.program_id(1), pl.program_id(2)
    ne, nf = pl.num_programs(1), pl.num_programs(2)
    tm = un_ref.shape[0]
    nsub = (cnt_ref[i * ne + e] + rs - 1) // rs
    pos = pos_ref[0]
    slot = lax.broadcasted_iota(jnp.int32, (rs, tm), 0).astype(F32)

    def pick(j):
        return pos == slot + (j * rs).astype(F32)

    @pl.when(jnp.logical_and(e == 0, f == 0))
    def _():
        o_ref[...] = jnp.zeros_like(o_ref)

    @pl.when(f == 0)
    def _():
        def gather(j, carry):
            rows = pl.ds(pl.multiple_of(j * rs, rs), rs)
            sel = jnp.where(pick(j), 1.0, 0.0).astype(BF16)
            xg_ref[rows, :] = _dot(sel, un_ref[...]).astype(BF16)
            acc_ref[rows, :] = jnp.zeros((rs, acc_ref.shape[1]), F32)
            return carry
        lax.fori_loop(0, nsub, gather, 0)

    def ffn(j, carry):
        rows = pl.ds(pl.multiple_of(j * rs, rs), rs)
        x = xg_ref[rows, :]
        hb = (_silu(_dot(x, wg_ref[0])) * _dot(x, wu_ref[0])).astype(BF16)
        acc_ref[rows, :] += _dot(hb, wd_ref[0])
        return carry
    lax.fori_loop(0, nsub, ffn, 0)

    @pl.when(f == nf - 1)
    def _():
        gate = gate_ref[0]

        def scatter(j, carry):
            rows = pl.ds(pl.multiple_of(j * rs, rs), rs)
            hit = pick(j)
            gsub = jnp.sum(jnp.where(hit, gate, 0.0), axis=1, keepdims=True)
            yb = (acc_ref[rows, :] * gsub).astype(BF16)
            o_ref[...] += _dot_tn(jnp.where(hit, 1.0, 0.0).astype(BF16), yb)
            return carry
        lax.fori_loop(0, nsub, scatter, 0)


def experts(un, pos, gate, counts, wg, wu, wd, tm, tf=512, rs=128):
    t, d = un.shape
    ne, _, ff = wg.shape
    nt = t // tm
    grid_spec = pltpu.PrefetchScalarGridSpec(
        num_scalar_prefetch=1,
        grid=(nt, ne, ff // tf),
        in_specs=[pl.BlockSpec((tm, d), lambda i, e, f, c: (i, 0)),
                  pl.BlockSpec((1, 1, tm), lambda i, e, f, c: (e, 0, i)),
                  pl.BlockSpec((1, 1, tm), lambda i, e, f, c: (e, 0, i)),
                  pl.BlockSpec((1, d, tf), lambda i, e, f, c: (e, 0, f)),
                  pl.BlockSpec((1, d, tf), lambda i, e, f, c: (e, 0, f)),
                  pl.BlockSpec((1, tf, d), lambda i, e, f, c: (e, f, 0))],
        out_specs=pl.BlockSpec((tm, d), lambda i, e, f, c: (i, 0)),
        scratch_shapes=[pltpu.VMEM((tm, d), BF16), pltpu.VMEM((tm, d), F32)],
    )
    return pl.pallas_call(
        functools.partial(_experts_kernel, rs=rs),
        grid_spec=grid_spec,
        out_shape=jax.ShapeDtypeStruct((t, d), F32),
        compiler_params=_cparams(("parallel", "arbitrary", "arbitrary")),
        name="moe_experts",
    )(counts, un, pos.reshape(ne, 1, t), gate.reshape(ne, 1, t), wg, wu, wd)


def moe(h, g, w_router, wg, wu, wd, tm=1024):
    t = h.shape[0]
    tm = min(tm, t)
    un, pos, gate, cnt = router(h, g, w_router.T, tm)
    counts = cnt[:, :, 0].reshape(-1)
    return experts(un, pos, gate, counts, wg, wu, wd, tm)


def kernel(x, p, ln_mix, ln_ffn, ln_ple, ln_final, lb_table, ab_w_in, ab_conv, b_a_log, b_dt_bias, a_gnorm, b_gnorm, ab_w_out, c_w_in, c_conv_w, c_conv_b, c_w_r, c_b_r, c_w_i, c_b_i, c_lambda, c_w_out, ffn_w_gate, ffn_w_up, ffn_w_down, moe_router, moe_w_gate, moe_w_up, moe_w_down, ple_w_proj, ple_w_gate):
    bsz, seq, d = x.shape
    t = bsz * seq
    depth = ln_mix.shape[0]
    a_heads = lb_table.shape[1] // HEAD_DIM
    b_heads = b_a_log.shape[1]
    a_w = a_heads * HEAD_DIM
    b_w = b_heads * HEAD_DIM
    main_w = 4 * a_w + 4 * b_w
    bf = lambda a: a.astype(BF16)

    h = x.reshape(t, d)
    for layer in range(depth):
        j = layer // 2
        if layer % 2 == 0:
            w_in = ab_w_in[j]
            w_small = jnp.pad(w_in[:, main_w:], ((0, 0), (0, 128 - 2 * b_heads)))
            proj, small = norm_proj(h, ln_mix[layer], bf(w_in[:, :main_w]), bf(w_small))
            o_a = hgrn2(proj, lb_table, a_gnorm[j], bsz=bsz, seq=seq, heads=a_heads, layer=layer, col0=0)
            o_b = gdn(proj, small, ab_conv[j], b_a_log[j], b_dt_bias[j], b_gnorm[j],
                      bsz=bsz, seq=seq, heads=b_heads, col0=4 * a_w)
            h = out_proj2(o_a, o_b, bf(ab_w_out[j]), h)
            h = swiglu(h, ln_ffn[layer], bf(ffn_w_gate[j]), bf(ffn_w_up[j]), bf(ffn_w_down[j]))
            add = None
        else:
            xr, yb = norm_proj_gelu(h, ln_mix[layer], bf(c_w_in[j]))
            hy = rglru(xr, yb, c_conv_w[j], c_conv_b[j], bf(c_w_r[j]), c_b_r[j], bf(c_w_i[j]), c_b_i[j],
                       c_lambda[j], bsz=bsz, seq=seq)
            h = out_proj(hy, bf(c_w_out[j]), h)
            add = moe(h, ln_ffn[layer], moe_router[j], bf(moe_w_gate[j]), bf(moe_w_up[j]), bf(moe_w_down[j]))
        g_final = ln_final if layer == depth - 1 else None
        h = ple(h, p[layer].reshape(t, -1), ln_ple[layer], bf(ple_w_gate[layer]), bf(ple_w_proj[layer]),
                add=add, g_final=g_final)
    if depth == 0:
        raise ValueError("depth must be positive")
    return h.reshape(bsz, seq, d)
```

```python
import functools

import jax
import jax.numpy as jnp
from jax import lax
from jax.experimental import pallas as pl
from jax.experimental.pallas import tpu as pltpu

F32 = jnp.float32
BF16 = jnp.bfloat16
EPS = 1e-6
CHUNK = 64
SUB = 16
HEAD_DIM = 128
CONV_WIDTH = 4
CONV_TAIL = 8
RGLRU_C = 8.0
TOP_K = 2
MOE_TILE = 1024
MOE_BLOCK = 1024
MOE_GRANULE = 16
MOE_GATHER_ROWS = 128
MOE_COMBINE_ROWS = 256
MOE_SLACK = 256
VMEM_LIMIT = 56 * 1024 * 1024


def _cparams(sem, vmem=VMEM_LIMIT):
    return pltpu.CompilerParams(dimension_semantics=sem, vmem_limit_bytes=vmem)


def _dot(a, b):
    return jnp.dot(a, b, preferred_element_type=F32)


def _dot_nt(a, b):
    return lax.dot_general(a, b, (((1,), (1,)), ((), ())), preferred_element_type=F32)


def _dot_tn(a, b):
    return lax.dot_general(a, b, (((0,), (0,)), ((), ())), preferred_element_type=F32)


def _split(a):
    hi = a.astype(BF16)
    lo = (a - hi.astype(F32)).astype(BF16)
    return hi, lo


def _dot3(a, b):
    ah, al = _split(a)
    bh, bl = _split(b)
    return _dot(ah, bh) + _dot(ah, bl) + _dot(al, bh)


def _rms(x, g):
    return x * lax.rsqrt(jnp.mean(x * x, axis=-1, keepdims=True) + EPS) * g


def _sigmoid(x):
    return 1.0 / (1.0 + jnp.exp(-x))


def _silu(x):
    return x * _sigmoid(x)


def _norm_proj_kernel(h_ref, g_ref, w_ref, ws_ref, o_ref, os_ref, un_ref):
    j = pl.program_id(1)

    @pl.when(j == 0)
    def _():
        un = _rms(h_ref[...], g_ref[...]).astype(BF16)
        un_ref[...] = un
        os_ref[...] = _dot(un, ws_ref[...])

    o_ref[...] = _dot(un_ref[...], w_ref[...])


def norm_proj(h, g, w, ws, tm=1024, tn=512):
    t, d = h.shape
    tm = min(tm, t)
    n = w.shape[1]
    return pl.pallas_call(
        _norm_proj_kernel,
        grid=(t // tm, n // tn),
        in_specs=[pl.BlockSpec((tm, d), lambda i, j: (i, 0)),
                  pl.BlockSpec((1, d), lambda i, j: (0, 0)),
                  pl.BlockSpec((d, tn), lambda i, j: (0, j)),
                  pl.BlockSpec((d, ws.shape[1]), lambda i, j: (0, 0))],
        out_specs=[pl.BlockSpec((tm, tn), lambda i, j: (i, j)),
                   pl.BlockSpec((tm, ws.shape[1]), lambda i, j: (i, 0))],
        out_shape=[jax.ShapeDtypeStruct((t, n), F32), jax.ShapeDtypeStruct((t, ws.shape[1]), F32)],
        scratch_shapes=[pltpu.VMEM((tm, d), BF16)],
        compiler_params=_cparams(("parallel", "arbitrary")),
        name="norm_proj",
    )(h, g.reshape(1, d), w, ws)


def _norm_proj_gelu_kernel(h_ref, g_ref, w_ref, o_ref, y_ref, un_ref, *, n_gelu_tiles):
    j = pl.program_id(1)

    @pl.when(j == 0)
    def _():
        un_ref[...] = _rms(h_ref[...], g_ref[...]).astype(BF16)

    acc = _dot(un_ref[...], w_ref[...])

    @pl.when(j < n_gelu_tiles)
    def _():
        y_ref[...] = jax.nn.gelu(acc).astype(BF16)

    @pl.when(j >= n_gelu_tiles)
    def _():
        o_ref[...] = acc


def norm_proj_gelu(h, g, w, tm=1024, tn=512):
    t, d = h.shape
    tm = min(tm, t)
    n = w.shape[1]
    half = n // 2
    nh = half // tn
    return pl.pallas_call(
        functools.partial(_norm_proj_gelu_kernel, n_gelu_tiles=nh),
        grid=(t // tm, n // tn),
        in_specs=[pl.BlockSpec((tm, d), lambda i, j: (i, 0)),
                  pl.BlockSpec((1, d), lambda i, j: (0, 0)),
                  pl.BlockSpec((d, tn), lambda i, j: (0, j))],
        out_specs=[pl.BlockSpec((tm, tn), lambda i, j: (i, jnp.maximum(j - nh, 0))),
                   pl.BlockSpec((tm, tn), lambda i, j: (i, jnp.minimum(j, nh - 1)))],
        out_shape=[jax.ShapeDtypeStruct((t, half), F32), jax.ShapeDtypeStruct((t, half), BF16)],
        scratch_shapes=[pltpu.VMEM((tm, d), BF16)],
        compiler_params=_cparams(("parallel", "arbitrary")),
        name="norm_proj_gelu",
    )(h, g.reshape(1, d), w)


def _hgrn2_kernel(q_ref, f_ref, i_ref, g_ref, lbt_ref, gn_ref, o_ref, st_ref, *, layer, heads):
    c = CHUNK

    @pl.when(pl.program_id(1) == 0)
    def _():
        st_ref[...] = jnp.zeros_like(st_ref)

    lbt = lbt_ref[...]
    e = jnp.exp(lbt - jnp.max(lbt, axis=0, keepdims=True))
    lb_all = jnp.sum(e[:layer + 1], axis=0, keepdims=True) / jnp.sum(e, axis=0, keepdims=True)

    row = lax.broadcasted_iota(jnp.int32, (c, c), 0)
    col = lax.broadcasted_iota(jnp.int32, (c, c), 1)
    tril = jnp.where(row >= col, 1.0, 0.0).astype(BF16)
    rsub = lax.broadcasted_iota(jnp.int32, (c, HEAD_DIM), 0) & (SUB - 1)
    gn = gn_ref[...]

    for h in range(heads):
        sl = slice(h * HEAD_DIM, (h + 1) * HEAD_DIM)
        lb = lb_all[:, sl]
        q = q_ref[:, sl] * (HEAD_DIM ** -0.5)
        forget = lb + (1.0 - lb) * _sigmoid(f_ref[:, sl])
        k = 1.0 - forget
        v = i_ref[:, sl]
        logf = jnp.log(forget)
        lh, ll = _split(logf)
        b = _dot(tril, lh) + _dot(tril, ll)
        st = st_ref[h]

        o = _dot_nt((q * jnp.exp(b)).astype(BF16), st.astype(BF16))
        parts = [jnp.zeros((SUB, HEAD_DIM), F32)]
        for blk in range(1, c // SUB):
            lo = blk * SUB
            bref = b[lo - 1:lo, :]
            qi = (q[lo:lo + SUB] * jnp.exp(b[lo:lo + SUB] - bref)).astype(BF16)
            kj = (k[:lo] * jnp.exp(bref - b[:lo])).astype(BF16)
            s = _dot_nt(qi, kj)
            parts.append(_dot(s.astype(BF16), v[:lo].astype(BF16)))
        o = o + jnp.concatenate(parts, axis=0)
        for d in range(SUB):
            if d == 0:
                w = jnp.sum(q * k, axis=1, keepdims=True)
                o = o + w * v
            else:
                bs = pltpu.roll(b, d, 0)
                ks = pltpu.roll(k, d, 0)
                vs = pltpu.roll(v, d, 0)
                valid = rsub >= d
                dec = jnp.exp(jnp.where(valid, b - bs, -jnp.inf))
                w = jnp.sum(q * ks * dec, axis=1, keepdims=True)
                o = o + w * vs

        blast = b[c - 1:c, :]
        kd = (k * jnp.exp(blast - b)).astype(BF16)
        st_ref[h] = st * jnp.exp(blast) + _dot_tn(v.astype(BF16), kd)

        on = o * lax.rsqrt(jnp.mean(o * o, axis=1, keepdims=True) + EPS) * gn
        o_ref[:, sl] = (on * _silu(g_ref[:, sl])).astype(BF16)


def hgrn2(proj, lb_table, gnorm, *, bsz, seq, heads, layer, col0):
    t = proj.shape[0]
    hw = heads * HEAD_DIM
    nc = seq // CHUNK
    cb = col0 // hw

    def spec(k):
        return pl.BlockSpec((CHUNK, hw), lambda b, s: (b * nc + s, cb + k))

    return pl.pallas_call(
        functools.partial(_hgrn2_kernel, layer=layer, heads=heads),
        grid=(bsz, nc),
        in_specs=[spec(0), spec(1), spec(2), spec(3),
                  pl.BlockSpec(lb_table.shape, lambda b, s: (0, 0)),
                  pl.BlockSpec((1, HEAD_DIM), lambda b, s: (0, 0))],
        out_specs=pl.BlockSpec((CHUNK, hw), lambda b, s: (b * nc + s, 0)),
        out_shape=jax.ShapeDtypeStruct((t, hw), BF16),
        scratch_shapes=[pltpu.VMEM((heads, HEAD_DIM, HEAD_DIM), F32)],
        compiler_params=_cparams(("parallel", "arbitrary")),
        name="hgrn2",
    )(proj, proj, proj, proj, lb_table, gnorm.reshape(1, HEAD_DIM))


def _causal_conv(ext_ref, x, w, first):
    n = x.shape[0]

    @pl.when(first)
    def _():
        ext_ref[0:CONV_TAIL, :] = jnp.zeros((CONV_TAIL, x.shape[1]), F32)

    ext_ref[CONV_TAIL:CONV_TAIL + n, :] = x
    y = x * w[CONV_WIDTH - 1:CONV_WIDTH, :]
    for k in range(CONV_WIDTH - 1):
        off = CONV_TAIL - (CONV_WIDTH - 1) + k
        y = y + ext_ref[off:off + n, :] * w[k:k + 1, :]
    ext_ref[0:CONV_TAIL, :] = ext_ref[n:n + CONV_TAIL, :]
    return y


def _gdn_kernel(q_ref, k_ref, v_ref, z_ref, sm_ref, cw_ref, alog_ref, dtb_ref, gn_ref, o_ref,
                st_ref, eq_ref, ek_ref, ev_ref, *, heads):
    c = CHUNK
    first = pl.program_id(1) == 0

    @pl.when(first)
    def _():
        st_ref[...] = jnp.zeros_like(st_ref)

    hw = heads * HEAD_DIM
    cw = cw_ref[...]
    qc = _silu(_causal_conv(eq_ref, q_ref[...], cw[:, 0:hw], first))
    kc = _silu(_causal_conv(ek_ref, k_ref[...], cw[:, hw:2 * hw], first))
    vc = _silu(_causal_conv(ev_ref, v_ref[...], cw[:, 2 * hw:3 * hw], first))

    row = lax.broadcasted_iota(jnp.int32, (c, c), 0)
    col = lax.broadcasted_iota(jnp.int32, (c, c), 1)
    causal = row >= col
    strict = row > col
    eye = row == col
    eye_f = jnp.where(eye, 1.0, 0.0)
    sm = sm_ref[...]
    gn = gn_ref[...]

    def to_row(colv):
        return jnp.sum(jnp.where(eye, colv, 0.0), axis=0, keepdims=True)

    hs = range(heads)
    sls = [slice(h * HEAD_DIM, (h + 1) * HEAD_DIM) for h in hs]
    qn, kn, knb, beta, gam_col, egam, decay, kbeta = [], [], [], [], [], [], [], []
    for h in hs:
        qh, kh = qc[:, sls[h]], kc[:, sls[h]]
        qn.append(qh * lax.rsqrt(jnp.sum(qh * qh, axis=1, keepdims=True) + EPS) * (HEAD_DIM ** -0.5))
        kn.append(kh * lax.rsqrt(jnp.sum(kh * kh, axis=1, keepdims=True) + EPS))
        knb.append(kn[h].astype(BF16))
        beta.append(_sigmoid(sm[:, heads + h:heads + h + 1]))
        g_col = -jnp.exp(alog_ref[:, h:h + 1]) * jax.nn.softplus(sm[:, h:h + 1] + dtb_ref[:, h:h + 1])
        g_row = to_row(g_col)
        gam_col.append(jnp.sum(jnp.where(causal, g_row, 0.0), axis=1, keepdims=True))
        gam_row = jnp.sum(jnp.where(strict, 0.0, g_col), axis=0, keepdims=True)
        decay.append(jnp.exp(jnp.where(causal, gam_col[h] - gam_row, -jnp.inf)))
        egam.append(jnp.exp(gam_col[h]))
        kbeta.append(kn[h] * beta[h])

    a_mat = [jnp.where(strict, _dot_nt(kbeta[h].astype(BF16), knb[h]) * decay[h], 0.0) for h in hs]
    qk = [(_dot_nt(qn[h].astype(BF16), knb[h]) * decay[h]).astype(BF16) for h in hs]
    x = [eye_f - a_mat[h] for h in hs]
    p = [_dot3(a_mat[h], a_mat[h]) for h in hs]
    n_sq = (c - 1).bit_length() - 1
    for lvl in range(n_sq):
        if lvl < n_sq - 1:
            y = [_dot3(jnp.concatenate([x[h], p[h]], axis=0), p[h]) for h in hs]
            x = [x[h] + y[h][:c] for h in hs]
            p = [y[h][c:] for h in hs]
        else:
            y = [_dot3(x[h], p[h]) for h in hs]
            x = [x[h] + y[h] for h in hs]
    rhs = [jnp.concatenate([vc[:, sls[h]] * beta[h], kbeta[h] * egam[h]], axis=1).astype(BF16) for h in hs]
    uw = [_dot(x[h].astype(BF16), rhs[h]) for h in hs]

    st = [st_ref[h] for h in hs]
    stb = [s.astype(BF16) for s in st]
    v_new = [uw[h][:, :HEAD_DIM] - _dot_nt(uw[h][:, HEAD_DIM:].astype(BF16), stb[h]) for h in hs]
    vnb = [v.astype(BF16) for v in v_new]
    o = [_dot_nt((qn[h] * egam[h]).astype(BF16), stb[h]) + _dot(qk[h], vnb[h]) for h in hs]
    for h in hs:
        glast = gam_col[h][c - 1:c, :]
        kd = (kn[h] * jnp.exp(glast - gam_col[h])).astype(BF16)
        st_ref[h] = st[h] * jnp.exp(glast) + _dot_tn(vnb[h], kd)
    for h in hs:
        on = o[h] * lax.rsqrt(jnp.mean(o[h] * o[h], axis=1, keepdims=True) + EPS) * gn
        o_ref[:, sls[h]] = (on * _silu(z_ref[:, sls[h]])).astype(BF16)


def gdn(proj, small, conv_w, a_log, dt_bias, gnorm, *, bsz, seq, heads, col0):
    t = proj.shape[0]
    hw = heads * HEAD_DIM
    nc = seq // CHUNK
    cb = col0 // hw

    def spec(k):
        return pl.BlockSpec((CHUNK, hw), lambda b, s: (b * nc + s, cb + k))

    def const(shape):
        return pl.BlockSpec(shape, lambda b, s: (0, 0))

    return pl.pallas_call(
        functools.partial(_gdn_kernel, heads=heads),
        grid=(bsz, nc),
        in_specs=[spec(0), spec(1), spec(2), spec(3),
                  pl.BlockSpec((CHUNK, small.shape[1]), lambda b, s: (b * nc + s, 0)),
                  const(conv_w.shape), const((1, heads)), const((1, heads)), const((1, HEAD_DIM))],
        out_specs=pl.BlockSpec((CHUNK, hw), lambda b, s: (b * nc + s, 0)),
        out_shape=jax.ShapeDtypeStruct((t, hw), BF16),
        scratch_shapes=[pltpu.VMEM((heads, HEAD_DIM, HEAD_DIM), F32)]
        + [pltpu.VMEM((CHUNK + CONV_TAIL, hw), F32)] * 3,
        compiler_params=_cparams(("parallel", "arbitrary")),
        name="gdn",
    )(proj, proj, proj, proj, small, conv_w, a_log.reshape(1, heads), dt_bias.reshape(1, heads),
      gnorm.reshape(1, HEAD_DIM))


def _out_proj2_kernel(a_ref, b_ref, wa_ref, wb_ref, h_ref, o_ref):
    o_ref[...] = h_ref[...] + _dot(a_ref[...], wa_ref[...]) + _dot(b_ref[...], wb_ref[...])


def out_proj2(a, b, w, h, tm=1024, tn=512):
    t, ka = a.shape
    tm = min(tm, t)
    kb = b.shape[1]
    n = w.shape[1]
    return pl.pallas_call(
        _out_proj2_kernel,
        grid=(t // tm, n // tn),
        in_specs=[pl.BlockSpec((tm, ka), lambda i, j: (i, 0)),
                  pl.BlockSpec((tm, kb), lambda i, j: (i, 0)),
                  pl.BlockSpec((ka, tn), lambda i, j: (0, j)),
                  pl.BlockSpec((kb, tn), lambda i, j: (ka // kb, j)),
                  pl.BlockSpec((tm, tn), lambda i, j: (i, j))],
        out_specs=pl.BlockSpec((tm, tn), lambda i, j: (i, j)),
        out_shape=jax.ShapeDtypeStruct((t, n), F32),
        compiler_params=_cparams(("parallel", "arbitrary")),
        name="out_proj2",
    )(a, b, w, w, h)


def _out_proj_kernel(a_ref, w_ref, h_ref, o_ref):
    o_ref[...] = h_ref[...] + _dot(a_ref[...], w_ref[...])


def out_proj(a, w, h, tm=1024, tn=512):
    t, k = a.shape
    tm = min(tm, t)
    n = w.shape[1]
    return pl.pallas_call(
        _out_proj_kernel,
        grid=(t // tm, n // tn),
        in_specs=[pl.BlockSpec((tm, k), lambda i, j: (i, 0)),
                  pl.BlockSpec((k, tn), lambda i, j: (0, j)),
                  pl.BlockSpec((tm, tn), lambda i, j: (i, j))],
        out_specs=pl.BlockSpec((tm, tn), lambda i, j: (i, j)),
        out_shape=jax.ShapeDtypeStruct((t, n), F32),
        compiler_params=_cparams(("parallel", "arbitrary")),
        name="out_proj",
    )(a, w, h)


def _swiglu_kernel(h_ref, g_ref, wg_ref, wu_ref, wd_ref, o_ref, un_ref):
    f = pl.program_id(1)

    @pl.when(f == 0)
    def _():
        un_ref[...] = _rms(h_ref[...], g_ref[...]).astype(BF16)
        o_ref[...] = h_ref[...]

    un = un_ref[...]
    hb = (_silu(_dot(un, wg_ref[...])) * _dot(un, wu_ref[...])).astype(BF16)
    o_ref[...] += _dot(hb, wd_ref[...])


def swiglu(h, g, wg, wu, wd, tm=1024, tf=512):
    t, d = h.shape
    tm = min(tm, t)
    ff = wg.shape[1]
    return pl.pallas_call(
        _swiglu_kernel,
        grid=(t // tm, ff // tf),
        in_specs=[pl.BlockSpec((tm, d), lambda i, f: (i, 0), pipeline_mode=pl.Buffered(1)),
                  pl.BlockSpec((1, d), lambda i, f: (0, 0)),
                  pl.BlockSpec((d, tf), lambda i, f: (0, f)),
                  pl.BlockSpec((d, tf), lambda i, f: (0, f)),
                  pl.BlockSpec((tf, d), lambda i, f: (f, 0))],
        out_specs=pl.BlockSpec((tm, d), lambda i, f: (i, 0)),
        out_shape=jax.ShapeDtypeStruct((t, d), F32),
        scratch_shapes=[pltpu.VMEM((tm, d), BF16)],
        compiler_params=_cparams(("parallel", "arbitrary")),
        name="swiglu",
    )(h, g.reshape(1, d), wg, wu, wd)


def _ple_kernel(*refs, has_add, has_final):
    h_ref, p_ref, g_ref, wg_ref, wp_ref = refs[:5]
    k = 5
    add_ref = gf_ref = None
    if has_add:
        add_ref = refs[k]
        k += 1
    if has_final:
        gf_ref = refs[k]
        k += 1
    o_ref = refs[k]
    h = h_ref[...]
    if has_add:
        h = h + add_ref[...]
    un = _rms(h, g_ref[...]).astype(BF16)
    gate = _sigmoid(_dot(un, wg_ref[...]))
    out = h + gate * _dot(p_ref[...].astype(BF16), wp_ref[...])
    if has_final:
        out = _rms(out, gf_ref[...])
    o_ref[...] = out


def ple(h, p, g, wg, wp, add=None, g_final=None, tm=512):
    t, d = h.shape
    tm = min(tm, t)
    pd = p.shape[1]
    row = lambda i: (i, 0)
    const = lambda i: (0, 0)
    in_specs = [pl.BlockSpec((tm, d), row), pl.BlockSpec((tm, pd), row), pl.BlockSpec((1, d), const),
                pl.BlockSpec((d, d), const), pl.BlockSpec((pd, d), const)]
    args = [h, p, g.reshape(1, d), wg, wp]
    if add is not None:
        in_specs.append(pl.BlockSpec((tm, d), row))
        args.append(add)
    if g_final is not None:
        in_specs.append(pl.BlockSpec((1, d), const))
        args.append(g_final.reshape(1, d))
    return pl.pallas_call(
        functools.partial(_ple_kernel, has_add=add is not None, has_final=g_final is not None),
        grid=(t // tm,),
        in_specs=in_specs,
        out_specs=pl.BlockSpec((tm, d), row),
        out_shape=jax.ShapeDtypeStruct((t, d), F32),
        compiler_params=_cparams(("parallel",)),
        name="ple",
    )(*args)


def _rglru_kernel(x_ref, y_ref, cw_ref, cb_ref, wr_ref, br_ref, wi_ref, bi_ref, lam_ref, o_ref,
                  ext_ref, hc_ref, *, blocks):
    n = x_ref.shape[0]
    first = pl.program_id(1) == 0

    @pl.when(first)
    def _():
        hc_ref[...] = jnp.zeros_like(hc_ref)

    xc = _causal_conv(ext_ref, x_ref[...], cw_ref[...], first) + cb_ref[...]
    bw = xc.shape[1] // blocks
    rowi = lax.broadcasted_iota(jnp.int32, (n, bw), 0)
    at_start = jnp.logical_and(first, rowi == 0)

    for nb in range(blocks):
        sl = slice(nb * bw, (nb + 1) * bw)
        xb = xc[:, sl]
        xbb = xb.astype(BF16)
        r = _sigmoid(_dot(xbb, wr_ref[nb]) + br_ref[:, sl])
        gi = _sigmoid(_dot(xbb, wi_ref[nb]) + bi_ref[:, sl])
        log_a = -RGLRU_C * r * jax.nn.softplus(-lam_ref[:, sl])
        a = jnp.exp(log_a)
        mult = jnp.sqrt(jnp.maximum(1.0 - a * a, 0.0))
        mult = jnp.where(at_start, 1.0, mult)
        b = mult * gi * xb
        sh = 1
        while sh < n:
            ok = rowi >= sh
            a_prev = jnp.where(ok, pltpu.roll(a, sh, 0), 1.0)
            b_prev = jnp.where(ok, pltpu.roll(b, sh, 0), 0.0)
            b = b + a * b_prev
            a = a * a_prev
            sh *= 2
        hseq = b + a * hc_ref[:, sl]
        hc_ref[:, sl] = hseq[n - 1:n, :]
        o_ref[:, sl] = (hseq * y_ref[:, sl].astype(F32)).astype(BF16)


def rglru(xr, y, conv_w, conv_b, w_r, b_r, w_i, b_i, lam, *, bsz, seq, rows=256):
    t, cwid = xr.shape
    blocks = w_r.shape[0]
    ns = seq // rows
    row = lambda b, s: (b * ns + s, 0)
    c2 = lambda b, s: (0, 0)
    c3 = lambda b, s: (0, 0, 0)
    vec = lambda a: a.reshape(1, cwid)
    return pl.pallas_call(
        functools.partial(_rglru_kernel, blocks=blocks),
        grid=(bsz, ns),
        in_specs=[pl.BlockSpec((rows, cwid), row), pl.BlockSpec((rows, cwid), row),
                  pl.BlockSpec(conv_w.shape, c2), pl.BlockSpec((1, cwid), c2),
                  pl.BlockSpec(w_r.shape, c3), pl.BlockSpec((1, cwid), c2),
                  pl.BlockSpec(w_i.shape, c3), pl.BlockSpec((1, cwid), c2),
                  pl.BlockSpec((1, cwid), c2)],
        out_specs=pl.BlockSpec((rows, cwid), row),
        out_shape=jax.ShapeDtypeStruct((t, cwid), BF16),
        scratch_shapes=[pltpu.VMEM((rows + CONV_TAIL, cwid), F32), pltpu.VMEM((1, cwid), F32)],
        compiler_params=_cparams(("parallel", "arbitrary")),
        name="rglru",
    )(xr, y, conv_w, vec(conv_b), w_r, vec(b_r), w_i, vec(b_i), vec(lam))


def _router_kernel(h_ref, g_ref, wr_ref, un_ref, pos_ref, gate_ref, cnt_ref):
    tm = h_ref.shape[0]
    ne = wr_ref.shape[0]
    un = _rms(h_ref[...], g_ref[...])
    uh, ul = _split(un)
    un_ref[...] = uh
    wh, wl = _split(wr_ref[...])
    logits = _dot_nt(wh, uh) + _dot_nt(wh, ul) + _dot_nt(wl, uh)
    eidx = lax.broadcasted_iota(jnp.int32, (ne, tm), 0).astype(F32)
    m1 = jnp.max(logits, axis=0, keepdims=True)
    i1 = jnp.min(jnp.where(logits == m1, eidx, float(ne)), axis=0, keepdims=True)
    mask1 = eidx == i1
    rest = jnp.where(mask1, -jnp.inf, logits)
    m2 = jnp.max(rest, axis=0, keepdims=True)
    i2 = jnp.min(jnp.where(rest == m2, eidx, float(ne)), axis=0, keepdims=True)
    mask2 = eidx == i2
    e2 = jnp.exp(m2 - m1)
    g1 = 1.0 / (1.0 + e2)
    g2 = e2 / (1.0 + e2)
    gate_ref[...] = jnp.where(mask1, g1, jnp.where(mask2, g2, 0.0))
    sel = jnp.logical_or(mask1, mask2)
    self32 = jnp.where(sel, 1.0, 0.0)
    before = lax.broadcasted_iota(jnp.int32, (tm, tm), 0) < lax.broadcasted_iota(jnp.int32, (tm, tm), 1)
    rank = _dot(self32.astype(BF16), jnp.where(before, 1.0, 0.0).astype(BF16))
    pos_ref[...] = jnp.where(sel, rank, -1.0)
    cnt = jnp.sum(self32, axis=1, keepdims=True).astype(jnp.int32)
    cnt_ref[0] = jnp.broadcast_to(cnt, cnt_ref.shape[1:])


def router(h, g, wr_t, tm):
    t, d = h.shape
    tm = min(tm, t)
    ne = wr_t.shape[0]
    nt = t // tm
    return pl.pallas_call(
        _router_kernel,
        grid=(nt,),
        in_specs=[pl.BlockSpec((tm, d), lambda i: (i, 0)),
                  pl.BlockSpec((1, d), lambda i: (0, 0)),
                  pl.BlockSpec((ne, d), lambda i: (0, 0))],
        out_specs=[pl.BlockSpec((tm, d), lambda i: (i, 0)),
                   pl.BlockSpec((ne, tm), lambda i: (0, i)),
                   pl.BlockSpec((ne, tm), lambda i: (0, i)),
                   pl.BlockSpec((1, ne, 128), lambda i: (i, 0, 0))],
        out_shape=[jax.ShapeDtypeStruct((t, d), BF16), jax.ShapeDtypeStruct((ne, t), F32),
                   jax.ShapeDtypeStruct((ne, t), F32), jax.ShapeDtypeStruct((nt, ne, 128), jnp.int32)],
        compiler_params=_cparams(("parallel",)),
        name="moe_router",
    )(h, g.reshape(1, d), wr_t)


def _pick(pos, base, rows):
    slot = lax.broadcasted_iota(jnp.int32, (rows, pos.shape[1]), 0).astype(F32)
    return pos == slot + base.astype(F32)


def _moe_gather_kernel(ce_ref, cb_ref, co_ref, nq_ref, un_ref, pos_ref, xs_in_ref, xs_ref, buf_ref, sem, *, rows, qmax):
    del xs_in_ref
    i = pl.program_id(0)
    n = nq_ref[i]

    def copy(slot, off):
        return pltpu.make_async_copy(buf_ref.at[slot], xs_ref.at[pl.ds(off, rows)], sem.at[slot])

    def body(q, carry):
        k = i * qmax + q
        slot = lax.rem(q, 2)

        @pl.when(q >= 2)
        def _():
            copy(slot, 0).wait()

        pos = pos_ref[pl.ds(ce_ref[k], 1), :]
        sel = jnp.where(_pick(pos, cb_ref[k], rows), 1.0, 0.0).astype(BF16)
        buf_ref[slot] = _dot(sel, un_ref[...]).astype(BF16)
        copy(slot, pl.multiple_of(co_ref[k], MOE_GRANULE)).start()
        return carry

    lax.fori_loop(0, n, body, 0)

    @pl.when(n >= 2)
    def _():
        copy(lax.rem(n, 2), 0).wait()

    @pl.when(n >= 1)
    def _():
        copy(lax.rem(n + 1, 2), 0).wait()


def moe_gather(un, pos, tables, n_rows, tm, rows):
    t, d = un.shape
    ne = pos.shape[0]
    ce, cb, co, nq, qmax = tables
    grid_spec = pltpu.PrefetchScalarGridSpec(
        num_scalar_prefetch=4,
        grid=(t // tm,),
        in_specs=[pl.BlockSpec((tm, d), lambda i, *_: (i, 0)),
                  pl.BlockSpec((ne, tm), lambda i, *_: (0, i)),
                  pl.BlockSpec(memory_space=pl.ANY)],
        out_specs=pl.BlockSpec(memory_space=pl.ANY),
        scratch_shapes=[pltpu.VMEM((2, rows, d), BF16), pltpu.SemaphoreType.DMA((2,))],
    )
    return pl.pallas_call(
        functools.partial(_moe_gather_kernel, rows=rows, qmax=qmax),
        grid_spec=grid_spec,
        out_shape=jax.ShapeDtypeStruct((n_rows, d), BF16),
        input_output_aliases={6: 0},
        compiler_params=_cparams(("arbitrary",)),
        name="moe_gather",
    )(ce, cb, co, nq, un, pos, jnp.zeros((n_rows, d), BF16))


def _moe_ffn_kernel(be_ref, nu_ref, x_ref, wg_ref, wu_ref, wd_ref, o_ref, acc_ref):
    del be_ref
    b, f = pl.program_id(0), pl.program_id(1)
    nf = pl.num_programs(1)
    used = b < nu_ref[0]

    @pl.when(used)
    def _():
        x = x_ref[...]
        hb = (_silu(_dot(x, wg_ref[0])) * _dot(x, wu_ref[0])).astype(BF16)
        c = _dot(hb, wd_ref[0])

        @pl.when(f == 0)
        def _():
            acc_ref[...] = c

        @pl.when(f > 0)
        def _():
            acc_ref[...] += c

        @pl.when(f == nf - 1)
        def _():
            o_ref[...] = acc_ref[...].astype(BF16)

    @pl.when(jnp.logical_and(jnp.logical_not(used), f == nf - 1))
    def _():
        o_ref[...] = jnp.zeros_like(o_ref)


def moe_ffn(xs, blk_e, n_used, wg, wu, wd, bm, tf=512):
    n_rows, d = xs.shape
    ff = wg.shape[2]
    nf = ff // tf

    def fidx(b, f, nu):
        return jnp.where(b < nu[0], f, nf - 1)

    grid_spec = pltpu.PrefetchScalarGridSpec(
        num_scalar_prefetch=2,
        grid=(n_rows // bm, nf),
        in_specs=[pl.BlockSpec((bm, d), lambda b, f, be, nu: (b, 0)),
                  pl.BlockSpec((1, d, tf), lambda b, f, be, nu: (be[b], 0, fidx(b, f, nu))),
                  pl.BlockSpec((1, d, tf), lambda b, f, be, nu: (be[b], 0, fidx(b, f, nu))),
                  pl.BlockSpec((1, tf, d), lambda b, f, be, nu: (be[b], fidx(b, f, nu), 0))],
        out_specs=pl.BlockSpec((bm, d), lambda b, f, be, nu: (b, 0)),
        scratch_shapes=[pltpu.VMEM((bm, d), F32)],
    )
    return pl.pallas_call(
        _moe_ffn_kernel,
        grid_spec=grid_spec,
        out_shape=jax.ShapeDtypeStruct((n_rows, d), BF16),
        compiler_params=_cparams(("parallel", "arbitrary")),
        name="moe_ffn",
    )(blk_e, n_used, xs, wg, wu, wd)


def _moe_combine_kernel(ce_ref, cb_ref, co_ref, nq_ref, pos_ref, gate_ref, y_ref, o_ref, buf_ref, sem, *, rows, qmax):
    i = pl.program_id(0)
    n = nq_ref[i]
    o_ref[...] = jnp.zeros_like(o_ref)

    def copy(slot, off):
        return pltpu.make_async_copy(y_ref.at[pl.ds(off, rows)], buf_ref.at[slot], sem.at[slot])

    def start(q):
        copy(lax.rem(q, 2), pl.multiple_of(co_ref[i * qmax + q], MOE_GRANULE)).start()

    @pl.when(n > 0)
    def _():
        start(0)

    def body(q, carry):
        k = i * qmax + q
        slot = lax.rem(q, 2)

        @pl.when(q + 1 < n)
        def _():
            start(q + 1)

        copy(slot, 0).wait()
        e = ce_ref[k]
        hit = _pick(pos_ref[pl.ds(e, 1), :], cb_ref[k], rows)
        gsub = jnp.sum(jnp.where(hit, gate_ref[pl.ds(e, 1), :], 0.0), axis=1, keepdims=True)
        yb = (buf_ref[slot].astype(F32) * gsub).astype(BF16)
        o_ref[...] += _dot_tn(jnp.where(hit, 1.0, 0.0).astype(BF16), yb)
        return carry

    lax.fori_loop(0, n, body, 0)


def moe_combine(ys, pos, gate, tables, tm, rows):
    ne, t = pos.shape
    d = ys.shape[1]
    ce, cb, co, nq, qmax = tables
    grid_spec = pltpu.PrefetchScalarGridSpec(
        num_scalar_prefetch=4,
        grid=(t // tm,),
        in_specs=[pl.BlockSpec((ne, tm), lambda i, *_: (0, i)),
                  pl.BlockSpec((ne, tm), lambda i, *_: (0, i)),
                  pl.BlockSpec(memory_space=pl.ANY)],
        out_specs=pl.BlockSpec((tm, d), lambda i, *_: (i, 0)),
        scratch_shapes=[pltpu.VMEM((2, rows, d), BF16), pltpu.SemaphoreType.DMA((2,))],
    )
    return pl.pallas_call(
        functools.partial(_moe_combine_kernel, rows=rows, qmax=qmax),
        grid_spec=grid_spec,
        out_shape=jax.ShapeDtypeStruct((t, d), F32),
        compiler_params=_cparams(("arbitrary",)),
        name="moe_combine",
    )(ce, cb, co, nq, pos, gate, ys)


def _chunk_tables(counts, seg, rows, qmax):
    ne = counts.shape[1]
    ns = (counts + rows - 1) // rows
    cs = jnp.cumsum(ns, axis=1)
    q = jnp.arange(qmax, dtype=jnp.int32)
    ce = jnp.minimum(jnp.sum(q[None, :, None] >= cs[:, None, :], axis=-1), ne - 1).astype(jnp.int32)
    cj = q[None, :] - jnp.take_along_axis(cs - ns, ce, axis=1)
    co = jnp.take_along_axis(seg, ce, axis=1) + cj * rows
    flat = lambda a: a.reshape(-1).astype(jnp.int32)
    return flat(ce), flat(cj * rows), flat(co), cs[:, -1].astype(jnp.int32), qmax


def moe(h, g, w_router, wg, wu, wd, tm=MOE_TILE, bm=MOE_BLOCK):
    t = h.shape[0]
    tm = min(tm, t)
    ne = wg.shape[0]
    nt = t // tm
    un, pos, gate, cnt = router(h, g, w_router.T, tm)
    counts = cnt[:, :, 0]
    padded = (counts + MOE_GRANULE - 1) // MOE_GRANULE * MOE_GRANULE
    tot = jnp.sum(padded, axis=0)
    ptot = (tot + MOE_SLACK + bm - 1) // bm * bm
    eend = jnp.cumsum(ptot)
    seg = (eend - ptot)[None, :] + jnp.cumsum(padded, axis=0) - padded
    n_blocks = (TOP_K * t + nt * ne * (MOE_GRANULE - 1) + ne * MOE_SLACK) // bm + ne
    n_used = (eend[-1:] // bm).astype(jnp.int32)
    blk_e = jnp.minimum(jnp.searchsorted(eend, jnp.arange(n_blocks, dtype=jnp.int32) * bm, side="right"),
                        ne - 1).astype(jnp.int32)
    g_tab = _chunk_tables(counts, seg, MOE_GATHER_ROWS, TOP_K * tm // MOE_GATHER_ROWS + ne)
    c_tab = _chunk_tables(counts, seg, MOE_COMBINE_ROWS, TOP_K * tm // MOE_COMBINE_ROWS + ne)
    xs = moe_gather(un, pos, g_tab, n_blocks * bm, tm, MOE_GATHER_ROWS)
    ys = moe_ffn(xs, blk_e, n_used, wg, wu, wd, bm)
    return moe_combine(ys, pos, gate, c_tab, tm, MOE_COMBINE_ROWS)


def kernel(x, p, ln_mix, ln_ffn, ln_ple, ln_final, lb_table, ab_w_in, ab_conv, b_a_log, b_dt_bias, a_gnorm, b_gnorm, ab_w_out, c_w_in, c_conv_w, c_conv_b, c_w_r, c_b_r, c_w_i, c_b_i, c_lambda, c_w_out, ffn_w_gate, ffn_w_up, ffn_w_down, moe_router, moe_w_gate, moe_w_up, moe_w_down, ple_w_proj, ple_w_gate):
    bsz, seq, d = x.shape
    t = bsz * seq
    depth = ln_mix.shape[0]
    a_heads = lb_table.shape[1] // HEAD_DIM
    b_heads = b_a_log.shape[1]
    a_w = a_heads * HEAD_DIM
    b_w = b_heads * HEAD_DIM
    main_w = 4 * a_w + 4 * b_w
    bf = lambda a: a.astype(BF16)

    h = x.reshape(t, d)
    for layer in range(depth):
        j = layer // 2
        if layer % 2 == 0:
            w_in = ab_w_in[j]
            w_small = jnp.pad(w_in[:, main_w:], ((0, 0), (0, 128 - 2 * b_heads)))
            proj, small = norm_proj(h, ln_mix[layer], bf(w_in[:, :main_w]), bf(w_small))
            o_a = hgrn2(proj, lb_table, a_gnorm[j], bsz=bsz, seq=seq, heads=a_heads, layer=layer, col0=0)
            o_b = gdn(proj, small, ab_conv[j], b_a_log[j], b_dt_bias[j], b_gnorm[j],
                      bsz=bsz, seq=seq, heads=b_heads, col0=4 * a_w)
            h = out_proj2(o_a, o_b, bf(ab_w_out[j]), h)
            h = swiglu(h, ln_ffn[layer], bf(ffn_w_gate[j]), bf(ffn_w_up[j]), bf(ffn_w_down[j]))
            add = None
        else:
            xr, yb = norm_proj_gelu(h, ln_mix[layer], bf(c_w_in[j]))
            hy = rglru(xr, yb, c_conv_w[j], c_conv_b[j], bf(c_w_r[j]), c_b_r[j], bf(c_w_i[j]), c_b_i[j],
                       c_lambda[j], bsz=bsz, seq=seq)
            h = out_proj(hy, bf(c_w_out[j]), h)
            add = moe(h, ln_ffn[layer], moe_router[j], bf(moe_w_gate[j]), bf(moe_w_up[j]), bf(moe_w_down[j]))
        g_final = ln_final if layer == depth - 1 else None
        h = ple(h, p[layer].reshape(t, -1), ln_ple[layer], bf(ple_w_gate[layer]), bf(ple_w_proj[layer]),
                add=add, g_final=g_final)
    if depth == 0:
        raise ValueError("depth must be positive")
    return h.reshape(bsz, seq, d)
```

```python
import functools

import jax
import jax.numpy as jnp
from jax import lax
from jax.experimental import pallas as pl
from jax.experimental.pallas import tpu as pltpu

F32 = jnp.float32
BF16 = jnp.bfloat16
EPS = 1e-6
CHUNK = 64
SUB = 8
HEAD_DIM = 128
CONV_WIDTH = 4
GDN_HEAD_GROUP = 8
CONV_TAIL = 8
RGLRU_C = 8.0
SCAN_GROUP = 8
TOP_K = 2
MOE_TILE = 1024
MOE_BLOCK = 1024
MOE_GRANULE = 16
MOE_GATHER_ROWS = 128
MOE_COMBINE_ROWS = 256
MOE_SLACK = 256
VMEM_LIMIT = 56 * 1024 * 1024


def _cparams(sem, vmem=VMEM_LIMIT):
    return pltpu.CompilerParams(dimension_semantics=sem, vmem_limit_bytes=vmem)


def _dot(a, b):
    return jnp.dot(a, b, preferred_element_type=F32)


def _dot_nt(a, b):
    return lax.dot_general(a, b, (((1,), (1,)), ((), ())), preferred_element_type=F32)


def _dot_tn(a, b):
    return lax.dot_general(a, b, (((0,), (0,)), ((), ())), preferred_element_type=F32)


def _split(a):
    hi = a.astype(BF16)
    lo = (a - hi.astype(F32)).astype(BF16)
    return hi, lo


def _dot3(a, b):
    ah, al = _split(a)
    bh, bl = _split(b)
    return _dot(jnp.concatenate([ah, al, ah], axis=1), jnp.concatenate([bh, bh, bl], axis=0))


def _rms(x, g):
    return x * lax.rsqrt(jnp.mean(x * x, axis=-1, keepdims=True) + EPS) * g


def _sigmoid(x):
    return 1.0 / (1.0 + jnp.exp(-x))


def _silu(x):
    return x * _sigmoid(x)


def _norm_proj_kernel(h_ref, g_ref, w_ref, ws_ref, o_ref, os_ref, un_ref):
    j = pl.program_id(1)

    @pl.when(j == 0)
    def _():
        un = _rms(h_ref[...], g_ref[...]).astype(BF16)
        un_ref[...] = un
        os_ref[...] = _dot(un, ws_ref[...])

    o_ref[...] = _dot(un_ref[...], w_ref[...])


def norm_proj(h, g, w, ws, tm=1024, tn=512):
    t, d = h.shape
    tm = min(tm, t)
    n = w.shape[1]
    return pl.pallas_call(
        _norm_proj_kernel,
        grid=(t // tm, n // tn),
        in_specs=[pl.BlockSpec((tm, d), lambda i, j: (i, 0)),
                  pl.BlockSpec((1, d), lambda i, j: (0, 0)),
                  pl.BlockSpec((d, tn), lambda i, j: (0, j)),
                  pl.BlockSpec((d, ws.shape[1]), lambda i, j: (0, 0))],
        out_specs=[pl.BlockSpec((tm, tn), lambda i, j: (i, j)),
                   pl.BlockSpec((tm, ws.shape[1]), lambda i, j: (i, 0))],
        out_shape=[jax.ShapeDtypeStruct((t, n), F32), jax.ShapeDtypeStruct((t, ws.shape[1]), F32)],
        scratch_shapes=[pltpu.VMEM((tm, d), BF16)],
        compiler_params=_cparams(("parallel", "arbitrary")),
        name="norm_proj",
    )(h, g.reshape(1, d), w, ws)


def _norm_proj_gelu_kernel(h_ref, g_ref, wy_ref, wx_ref, o_ref, y_ref, un_ref):
    @pl.when(pl.program_id(1) == 0)
    def _():
        un_ref[...] = _rms(h_ref[...], g_ref[...]).astype(BF16)

    un = un_ref[...]
    y_ref[...] = jax.nn.gelu(_dot(un, wy_ref[...])).astype(BF16)
    o_ref[...] = _dot(un, wx_ref[...])


def norm_proj_gelu(h, g, w, tm=1024, tn=512):
    t, d = h.shape
    tm = min(tm, t)
    half = w.shape[1] // 2
    nh = half // tn
    return pl.pallas_call(
        _norm_proj_gelu_kernel,
        grid=(t // tm, nh),
        in_specs=[pl.BlockSpec((tm, d), lambda i, j: (i, 0)),
                  pl.BlockSpec((1, d), lambda i, j: (0, 0)),
                  pl.BlockSpec((d, tn), lambda i, j: (0, j)),
                  pl.BlockSpec((d, tn), lambda i, j: (0, nh + j))],
        out_specs=[pl.BlockSpec((tm, tn), lambda i, j: (i, j)),
                   pl.BlockSpec((tm, tn), lambda i, j: (i, j))],
        out_shape=[jax.ShapeDtypeStruct((t, half), F32), jax.ShapeDtypeStruct((t, half), BF16)],
        scratch_shapes=[pltpu.VMEM((tm, d), BF16)],
        compiler_params=_cparams(("parallel", "arbitrary")),
        name="norm_proj_gelu",
    )(h, g.reshape(1, d), w, w)


def _hgrn2_kernel(q_ref, f_ref, i_ref, g_ref, lbt_ref, gn_ref, o_ref, st_ref, *, layer, heads):
    c = CHUNK

    @pl.when(pl.program_id(1) == 0)
    def _():
        st_ref[...] = jnp.zeros_like(st_ref)

    lbt = lbt_ref[...]
    e = jnp.exp(lbt - jnp.max(lbt, axis=0, keepdims=True))
    lb_all = jnp.sum(e[:layer + 1], axis=0, keepdims=True) / jnp.sum(e, axis=0, keepdims=True)

    row = lax.broadcasted_iota(jnp.int32, (c, c), 0)
    col = lax.broadcasted_iota(jnp.int32, (c, c), 1)
    tril = jnp.where(row >= col, 1.0, 0.0).astype(BF16)
    gn = gn_ref[...]

    levels = []
    ln = c // 2
    while ln >= SUB:
        levels += [(m * 2 * ln, m * 2 * ln + ln, ln) for m in range(c // (2 * ln))]
        ln //= 2
    n_pairs = sum(l[2] for l in levels)

    def seg_id(idx):
        sid = jnp.zeros_like(idx)
        start = 0
        for l in levels[:-1]:
            start += l[2]
            sid = sid + jnp.where(idx >= start, 1, 0)
        return sid

    same_seg = (seg_id(lax.broadcasted_iota(jnp.int32, (n_pairs, n_pairs), 0))
                == seg_id(lax.broadcasted_iota(jnp.int32, (n_pairs, n_pairs), 1)))
    sub_i = lax.broadcasted_iota(jnp.int32, (c // SUB, SUB, HEAD_DIM), 1)

    sls = [slice(h * HEAD_DIM, (h + 1) * HEAD_DIM) for h in range(heads)]

    def gates(h):
        lb = lb_all[:, sls[h]]
        q = q_ref[:, sls[h]] * (HEAD_DIM ** -0.5)
        forget = lb + (1.0 - lb) * _sigmoid(f_ref[:, sls[h]])
        lh, ll = _split(jnp.log(forget))
        b2 = _dot(tril, jnp.concatenate([lh, ll], axis=1))
        return q, 1.0 - forget, i_ref[:, sls[h]], b2[:, :HEAD_DIM] + b2[:, HEAD_DIM:]

    def block_pairs(h, q, k, v, b):
        o = _dot_nt((q * jnp.exp(b)).astype(BF16), st_ref[h].astype(BF16))
        qs, ks, vs = [], [], []
        for k0, q0, ln in levels:
            bref = b[q0 - 1:q0, :]
            qs.append(q[q0:q0 + ln] * jnp.exp(b[q0:q0 + ln] - bref))
            ks.append(k[k0:k0 + ln] * jnp.exp(bref - b[k0:k0 + ln]))
            vs.append(v[k0:k0 + ln])
        s = _dot_nt(jnp.concatenate(qs, axis=0).astype(BF16), jnp.concatenate(ks, axis=0).astype(BF16))
        r = _dot(jnp.where(same_seg, s, 0.0).astype(BF16), jnp.concatenate(vs, axis=0).astype(BF16))
        groups = [None] * (c // SUB)
        start = 0
        for k0, q0, ln in levels:
            for j in range(ln // SUB):
                piece = r[start + j * SUB:start + (j + 1) * SUB]
                gi = q0 // SUB + j
                groups[gi] = piece if groups[gi] is None else groups[gi] + piece
            start += ln
        groups[0] = jnp.zeros((SUB, HEAD_DIM), F32)
        return o + jnp.concatenate(groups, axis=0)

    def near_pairs(q, k, v, b):
        q3, k3, v3, b3 = (a.reshape(c // SUB, SUB, HEAD_DIM) for a in (q, k, v, b))
        o3 = jnp.sum(q3 * k3, axis=2, keepdims=True) * v3
        for d in range(1, SUB):
            dec = jnp.exp(jnp.where(sub_i >= d, b3 - pltpu.roll(b3, d, 1), -jnp.inf))
            w = jnp.sum(q3 * pltpu.roll(k3, d, 1) * dec, axis=2, keepdims=True)
            o3 = o3 + w * pltpu.roll(v3, d, 1)
        return o3.reshape(c, HEAD_DIM)

    qkvb = [gates(h) for h in range(heads)]
    far = [block_pairs(h, *qkvb[h]) for h in range(heads)]
    for h in range(heads):
        q, k, v, b = qkvb[h]
        blast = b[c - 1:c, :]
        kd = (k * jnp.exp(blast - b)).astype(BF16)
        st_ref[h] = st_ref[h] * jnp.exp(blast) + _dot_tn(v.astype(BF16), kd)
    for h in range(heads):
        o = far[h] + near_pairs(*qkvb[h])
        on = o * lax.rsqrt(jnp.mean(o * o, axis=1, keepdims=True) + EPS) * gn
        o_ref[:, sls[h]] = (on * _silu(g_ref[:, sls[h]])).astype(BF16)


def hgrn2(proj, lb_table, gnorm, *, bsz, seq, heads, layer, col0):
    t = proj.shape[0]
    hw = heads * HEAD_DIM
    nc = seq // CHUNK
    cb = col0 // hw

    def spec(k):
        return pl.BlockSpec((CHUNK, hw), lambda b, s: (b * nc + s, cb + k))

    return pl.pallas_call(
        functools.partial(_hgrn2_kernel, layer=layer, heads=heads),
        grid=(bsz, nc),
        in_specs=[spec(0), spec(1), spec(2), spec(3),
                  pl.BlockSpec(lb_table.shape, lambda b, s: (0, 0)),
                  pl.BlockSpec((1, HEAD_DIM), lambda b, s: (0, 0))],
        out_specs=pl.BlockSpec((CHUNK, hw), lambda b, s: (b * nc + s, 0)),
        out_shape=jax.ShapeDtypeStruct((t, hw), BF16),
        scratch_shapes=[pltpu.VMEM((heads, HEAD_DIM, HEAD_DIM), F32)],
        compiler_params=_cparams(("parallel", "arbitrary")),
        name="hgrn2",
    )(proj, proj, proj, proj, lb_table, gnorm.reshape(1, HEAD_DIM))


def _causal_conv(ext_ref, x, w, first):
    n = x.shape[0]

    @pl.when(first)
    def _():
        ext_ref[0:CONV_TAIL, :] = jnp.zeros((CONV_TAIL, x.shape[1]), F32)

    ext_ref[CONV_TAIL:CONV_TAIL + n, :] = x
    y = x * w[CONV_WIDTH - 1:CONV_WIDTH, :]
    for k in range(CONV_WIDTH - 1):
        off = CONV_TAIL - (CONV_WIDTH - 1) + k
        y = y + ext_ref[off:off + n, :] * w[k:k + 1, :]
    ext_ref[0:CONV_TAIL, :] = ext_ref[n:n + CONV_TAIL, :]
    return y


def _gdn_kernel(q_ref, k_ref, v_ref, z_ref, sm_ref, cw_ref, alog_ref, dtb_ref, gn_ref, o_ref,
                st_ref, eq_ref, ek_ref, ev_ref, *, heads):
    c = CHUNK
    first = pl.program_id(1) == 0

    @pl.when(first)
    def _():
        st_ref[...] = jnp.zeros_like(st_ref)

    hw = heads * HEAD_DIM
    cw = cw_ref[...]
    qc = _silu(_causal_conv(eq_ref, q_ref[...], cw[:, 0:hw], first))
    kc = _silu(_causal_conv(ek_ref, k_ref[...], cw[:, hw:2 * hw], first))
    vc = _silu(_causal_conv(ev_ref, v_ref[...], cw[:, 2 * hw:3 * hw], first))

    row = lax.broadcasted_iota(jnp.int32, (c, c), 0)
    col = lax.broadcasted_iota(jnp.int32, (c, c), 1)
    causal = row >= col
    strict = row > col
    eye = row == col
    eye_f = jnp.where(eye, 1.0, 0.0)
    sm = sm_ref[...]
    gn = gn_ref[...]

    def to_row(colv):
        return jnp.sum(jnp.where(eye, colv, 0.0), axis=0, keepdims=True)

    def head_group(hg):
        hs = range(len(hg))
        sls = [slice(h * HEAD_DIM, (h + 1) * HEAD_DIM) for h in hg]
        qn, kn, knb, beta, gam_col, egam, decay, kbeta = [], [], [], [], [], [], [], []
        for i, h in enumerate(hg):
            qh, kh = qc[:, sls[i]], kc[:, sls[i]]
            qn.append(qh * lax.rsqrt(jnp.sum(qh * qh, axis=1, keepdims=True) + EPS) * (HEAD_DIM ** -0.5))
            kn.append(kh * lax.rsqrt(jnp.sum(kh * kh, axis=1, keepdims=True) + EPS))
            knb.append(kn[i].astype(BF16))
            beta.append(_sigmoid(sm[:, heads + h:heads + h + 1]))
            g_col = -jnp.exp(alog_ref[:, h:h + 1]) * jax.nn.softplus(sm[:, h:h + 1] + dtb_ref[:, h:h + 1])
            g_row = to_row(g_col)
            gam_col.append(jnp.sum(jnp.where(causal, g_row, 0.0), axis=1, keepdims=True))
            gam_row = jnp.sum(jnp.where(strict, 0.0, g_col), axis=0, keepdims=True)
            decay.append(jnp.exp(jnp.where(causal, gam_col[i] - gam_row, -jnp.inf)))
            egam.append(jnp.exp(gam_col[i]))
            kbeta.append(kn[i] * beta[i])

        a_mat = [jnp.where(strict, _dot_nt(kbeta[i].astype(BF16), knb[i]) * decay[i], 0.0) for i in hs]
        qk = [(_dot_nt(qn[i].astype(BF16), knb[i]) * decay[i]).astype(BF16) for i in hs]
        x = [eye_f - a_mat[i] for i in hs]
        p = [_dot3(a_mat[i], a_mat[i]) for i in hs]
        n_sq = (c - 1).bit_length() - 1
        for lvl in range(n_sq):
            if lvl < n_sq - 1:
                y = [_dot3(jnp.concatenate([x[i], p[i]], axis=0), p[i]) for i in hs]
                x = [x[i] + y[i][:c] for i in hs]
                p = [y[i][c:] for i in hs]
            else:
                y = [_dot3(x[i], p[i]) for i in hs]
                x = [x[i] + y[i] for i in hs]
        rhs = [jnp.concatenate([vc[:, sls[i]] * beta[i], kbeta[i] * egam[i]], axis=1).astype(BF16) for i in hs]
        uw = [_dot(x[i].astype(BF16), rhs[i]) for i in hs]

        st = [st_ref[h] for h in hg]
        stb = [s.astype(BF16) for s in st]
        v_new = [uw[i][:, :HEAD_DIM] - _dot_nt(uw[i][:, HEAD_DIM:].astype(BF16), stb[i]) for i in hs]
        vnb = [v.astype(BF16) for v in v_new]
        o = [_dot_nt((qn[i] * egam[i]).astype(BF16), stb[i]) + _dot(qk[i], vnb[i]) for i in hs]
        for i, h in enumerate(hg):
            glast = gam_col[i][c - 1:c, :]
            kd = (kn[i] * jnp.exp(glast - gam_col[i])).astype(BF16)
            st_ref[h] = st[i] * jnp.exp(glast) + _dot_tn(vnb[i], kd)
        for i in hs:
            on = o[i] * lax.rsqrt(jnp.mean(o[i] * o[i], axis=1, keepdims=True) + EPS) * gn
            o_ref[:, sls[i]] = (on * _silu(z_ref[:, sls[i]])).astype(BF16)

    for h0 in range(0, heads, GDN_HEAD_GROUP):
        head_group(list(range(h0, min(h0 + GDN_HEAD_GROUP, heads))))


def gdn(proj, small, conv_w, a_log, dt_bias, gnorm, *, bsz, seq, heads, col0):
    t = proj.shape[0]
    hw = heads * HEAD_DIM
    nc = seq // CHUNK
    cb = col0 // hw

    def spec(k):
        return pl.BlockSpec((CHUNK, hw), lambda b, s: (b * nc + s, cb + k))

    def const(shape):
        return pl.BlockSpec(shape, lambda b, s: (0, 0))

    return pl.pallas_call(
        functools.partial(_gdn_kernel, heads=heads),
        grid=(bsz, nc),
        in_specs=[spec(0), spec(1), spec(2), spec(3),
                  pl.BlockSpec((CHUNK, small.shape[1]), lambda b, s: (b * nc + s, 0)),
                  const(conv_w.shape), const((1, heads)), const((1, heads)), const((1, HEAD_DIM))],
        out_specs=pl.BlockSpec((CHUNK, hw), lambda b, s: (b * nc + s, 0)),
        out_shape=jax.ShapeDtypeStruct((t, hw), BF16),
        scratch_shapes=[pltpu.VMEM((heads, HEAD_DIM, HEAD_DIM), F32)]
        + [pltpu.VMEM((CHUNK + CONV_TAIL, hw), F32)] * 3,
        compiler_params=_cparams(("parallel", "arbitrary")),
        name="gdn",
    )(proj, proj, proj, proj, small, conv_w, a_log.reshape(1, heads), dt_bias.reshape(1, heads),
      gnorm.reshape(1, HEAD_DIM))


def _out_proj2_kernel(a_ref, b_ref, wa_ref, wb_ref, h_ref, o_ref):
    o_ref[...] = h_ref[...] + _dot(a_ref[...], wa_ref[...]) + _dot(b_ref[...], wb_ref[...])


def out_proj2(a, b, w, h, tm=1024, tn=512):
    t, ka = a.shape
    tm = min(tm, t)
    kb = b.shape[1]
    n = w.shape[1]
    return pl.pallas_call(
        _out_proj2_kernel,
        grid=(t // tm, n // tn),
        in_specs=[pl.BlockSpec((tm, ka), lambda i, j: (i, 0)),
                  pl.BlockSpec((tm, kb), lambda i, j: (i, 0)),
                  pl.BlockSpec((ka, tn), lambda i, j: (0, j)),
                  pl.BlockSpec((kb, tn), lambda i, j: (ka // kb, j)),
                  pl.BlockSpec((tm, tn), lambda i, j: (i, j))],
        out_specs=pl.BlockSpec((tm, tn), lambda i, j: (i, j)),
        out_shape=jax.ShapeDtypeStruct((t, n), F32),
        compiler_params=_cparams(("parallel", "arbitrary")),
        name="out_proj2",
    )(a, b, w, w, h)


def _out_proj_kernel(a_ref, w_ref, h_ref, o_ref):
    o_ref[...] = h_ref[...] + _dot(a_ref[...], w_ref[...])


def out_proj(a, w, h, tm=1024, tn=512):
    t, k = a.shape
    tm = min(tm, t)
    n = w.shape[1]
    return pl.pallas_call(
        _out_proj_kernel,
        grid=(t // tm, n // tn),
        in_specs=[pl.BlockSpec((tm, k), lambda i, j: (i, 0)),
                  pl.BlockSpec((k, tn), lambda i, j: (0, j)),
                  pl.BlockSpec((tm, tn), lambda i, j: (i, j))],
        out_specs=pl.BlockSpec((tm, tn), lambda i, j: (i, j)),
        out_shape=jax.ShapeDtypeStruct((t, n), F32),
        compiler_params=_cparams(("parallel", "arbitrary")),
        name="out_proj",
    )(a, w, h)


def _swiglu_kernel(h_ref, g_ref, wg_ref, wu_ref, wd_ref, o_ref, un_ref):
    f = pl.program_id(1)

    @pl.when(f == 0)
    def _():
        un_ref[...] = _rms(h_ref[...], g_ref[...]).astype(BF16)
        o_ref[...] = h_ref[...]

    un = un_ref[...]
    hb = (_silu(_dot(un, wg_ref[...])) * _dot(un, wu_ref[...])).astype(BF16)
    o_ref[...] += _dot(hb, wd_ref[...])


def swiglu(h, g, wg, wu, wd, tm=1024, tf=512):
    t, d = h.shape
    tm = min(tm, t)
    ff = wg.shape[1]
    return pl.pallas_call(
        _swiglu_kernel,
        grid=(t // tm, ff // tf),
        in_specs=[pl.BlockSpec((tm, d), lambda i, f: (i, 0), pipeline_mode=pl.Buffered(1)),
                  pl.BlockSpec((1, d), lambda i, f: (0, 0)),
                  pl.BlockSpec((d, tf), lambda i, f: (0, f)),
                  pl.BlockSpec((d, tf), lambda i, f: (0, f)),
                  pl.BlockSpec((tf, d), lambda i, f: (f, 0))],
        out_specs=pl.BlockSpec((tm, d), lambda i, f: (i, 0)),
        out_shape=jax.ShapeDtypeStruct((t, d), F32),
        scratch_shapes=[pltpu.VMEM((tm, d), BF16)],
        compiler_params=_cparams(("parallel", "arbitrary")),
        name="swiglu",
    )(h, g.reshape(1, d), wg, wu, wd)


def _ple_kernel(*refs, has_add, has_final):
    h_ref, p_ref, g_ref, wg_ref, wp_ref = refs[:5]
    k = 5
    add_ref = gf_ref = None
    if has_add:
        add_ref = refs[k]
        k += 1
    if has_final:
        gf_ref = refs[k]
        k += 1
    o_ref = refs[k]
    h = h_ref[...]
    if has_add:
        h = h + add_ref[...]
    un = _rms(h, g_ref[...]).astype(BF16)
    gate = _sigmoid(_dot(un, wg_ref[...]))
    out = h + gate * _dot(p_ref[...].astype(BF16), wp_ref[...])
    if has_final:
        out = _rms(out, gf_ref[...])
    o_ref[...] = out


def ple(h, p, g, wg, wp, add=None, g_final=None, tm=512):
    t, d = h.shape
    tm = min(tm, t)
    pd = p.shape[1]
    row = lambda i: (i, 0)
    const = lambda i: (0, 0)
    in_specs = [pl.BlockSpec((tm, d), row), pl.BlockSpec((tm, pd), row), pl.BlockSpec((1, d), const),
                pl.BlockSpec((d, d), const), pl.BlockSpec((pd, d), const)]
    args = [h, p, g.reshape(1, d), wg, wp]
    if add is not None:
        in_specs.append(pl.BlockSpec((tm, d), row))
        args.append(add)
    if g_final is not None:
        in_specs.append(pl.BlockSpec((1, d), const))
        args.append(g_final.reshape(1, d))
    return pl.pallas_call(
        functools.partial(_ple_kernel, has_add=add is not None, has_final=g_final is not None),
        grid=(t // tm,),
        in_specs=in_specs,
        out_specs=pl.BlockSpec((tm, d), row),
        out_shape=jax.ShapeDtypeStruct((t, d), F32),
        compiler_params=_cparams(("parallel",)),
        name="ple",
    )(*args)


def _rglru_kernel(x_ref, y_ref, cw_ref, cb_ref, wr_ref, br_ref, wi_ref, bi_ref, lam_ref, o_ref,
                  ext_ref, hc_ref, *, blocks):
    n = x_ref.shape[0]
    first = pl.program_id(1) == 0

    @pl.when(first)
    def _():
        hc_ref[...] = jnp.zeros_like(hc_ref)

    xc = _causal_conv(ext_ref, x_ref[...], cw_ref[...], first) + cb_ref[...]
    bw = xc.shape[1] // blocks
    rowi = lax.broadcasted_iota(jnp.int32, (n, bw), 0)
    at_start = jnp.logical_and(first, rowi == 0)
    gidx = lax.broadcasted_iota(jnp.int32, (n // SCAN_GROUP, SCAN_GROUP, bw), 1)

    for nb in range(blocks):
        sl = slice(nb * bw, (nb + 1) * bw)
        xb = xc[:, sl]
        xbb = xb.astype(BF16)
        r = _sigmoid(_dot(xbb, wr_ref[nb]) + br_ref[:, sl])
        gi = _sigmoid(_dot(xbb, wi_ref[nb]) + bi_ref[:, sl])
        log_a = -RGLRU_C * r * jax.nn.softplus(-lam_ref[:, sl])
        a = jnp.exp(log_a)
        m2 = 1.0 - a * a
        mult = jnp.where(m2 > 0.0, m2 * lax.rsqrt(m2), 0.0)
        mult = jnp.where(at_start, 1.0, mult)
        b = mult * gi * xb
        a = a.reshape(n // SCAN_GROUP, SCAN_GROUP, bw)
        b = b.reshape(n // SCAN_GROUP, SCAN_GROUP, bw)
        sh = 1
        while sh < SCAN_GROUP:
            ok = gidx >= sh
            a_prev = jnp.where(ok, pltpu.roll(a, sh, 1), 1.0)
            b_prev = jnp.where(ok, pltpu.roll(b, sh, 1), 0.0)
            b = b + a * b_prev
            a = a * a_prev
            sh *= 2
        carry = hc_ref[:, sl]
        groups = []
        for gi_ in range(n // SCAN_GROUP):
            hg = b[gi_] + a[gi_] * carry
            groups.append(hg)
            carry = hg[SCAN_GROUP - 1:SCAN_GROUP, :]
        hseq = jnp.concatenate(groups, axis=0)
        hc_ref[:, sl] = carry
        o_ref[:, sl] = (hseq * y_ref[:, sl].astype(F32)).astype(BF16)


def rglru(xr, y, conv_w, conv_b, w_r, b_r, w_i, b_i, lam, *, bsz, seq, rows=256):
    t, cwid = xr.shape
    blocks = w_r.shape[0]
    ns = seq // rows
    row = lambda b, s: (b * ns + s, 0)
    c2 = lambda b, s: (0, 0)
    c3 = lambda b, s: (0, 0, 0)
    vec = lambda a: a.reshape(1, cwid)
    return pl.pallas_call(
        functools.partial(_rglru_kernel, blocks=blocks),
        grid=(bsz, ns),
        in_specs=[pl.BlockSpec((rows, cwid), row), pl.BlockSpec((rows, cwid), row),
                  pl.BlockSpec(conv_w.shape, c2), pl.BlockSpec((1, cwid), c2),
                  pl.BlockSpec(w_r.shape, c3), pl.BlockSpec((1, cwid), c2),
                  pl.BlockSpec(w_i.shape, c3), pl.BlockSpec((1, cwid), c2),
                  pl.BlockSpec((1, cwid), c2)],
        out_specs=pl.BlockSpec((rows, cwid), row),
        out_shape=jax.ShapeDtypeStruct((t, cwid), BF16),
        scratch_shapes=[pltpu.VMEM((rows + CONV_TAIL, cwid), F32), pltpu.VMEM((1, cwid), F32)],
        compiler_params=_cparams(("parallel", "arbitrary")),
        name="rglru",
    )(xr, y, conv_w, vec(conv_b), w_r, vec(b_r), w_i, vec(b_i), vec(lam))


def _router_kernel(h_ref, g_ref, wr_ref, un_ref, pos_ref, gate_ref, cnt_ref):
    tm = h_ref.shape[0]
    ne = wr_ref.shape[0]
    un = _rms(h_ref[...], g_ref[...])
    uh, ul = _split(un)
    un_ref[...] = uh
    wh, wl = _split(wr_ref[...])
    logits = _dot_nt(wh, uh) + _dot_nt(wh, ul) + _dot_nt(wl, uh)
    eidx = lax.broadcasted_iota(jnp.int32, (ne, tm), 0).astype(F32)
    m1 = jnp.max(logits, axis=0, keepdims=True)
    i1 = jnp.min(jnp.where(logits == m1, eidx, float(ne)), axis=0, keepdims=True)
    mask1 = eidx == i1
    rest = jnp.where(mask1, -jnp.inf, logits)
    m2 = jnp.max(rest, axis=0, keepdims=True)
    i2 = jnp.min(jnp.where(rest == m2, eidx, float(ne)), axis=0, keepdims=True)
    mask2 = eidx == i2
    e2 = jnp.exp(m2 - m1)
    g1 = 1.0 / (1.0 + e2)
    g2 = e2 / (1.0 + e2)
    gate_ref[...] = jnp.where(mask1, g1, jnp.where(mask2, g2, 0.0))
    sel = jnp.logical_or(mask1, mask2)
    self32 = jnp.where(sel, 1.0, 0.0)
    before = lax.broadcasted_iota(jnp.int32, (tm, tm), 0) < lax.broadcasted_iota(jnp.int32, (tm, tm), 1)
    rank = _dot(self32.astype(BF16), jnp.where(before, 1.0, 0.0).astype(BF16))
    pos_ref[...] = jnp.where(sel, rank, -1.0)
    cnt = jnp.sum(self32, axis=1, keepdims=True).astype(jnp.int32)
    cnt_ref[0] = jnp.broadcast_to(cnt, cnt_ref.shape[1:])


def router(h, g, wr_t, tm):
    t, d = h.shape
    tm = min(tm, t)
    ne = wr_t.shape[0]
    nt = t // tm
    return pl.pallas_call(
        _router_kernel,
        grid=(nt,),
        in_specs=[pl.BlockSpec((tm, d), lambda i: (i, 0)),
                  pl.BlockSpec((1, d), lambda i: (0, 0)),
                  pl.BlockSpec((ne, d), lambda i: (0, 0))],
        out_specs=[pl.BlockSpec((tm, d), lambda i: (i, 0)),
                   pl.BlockSpec((ne, tm), lambda i: (0, i)),
                   pl.BlockSpec((ne, tm), lambda i: (0, i)),
                   pl.BlockSpec((1, ne, 128), lambda i: (i, 0, 0))],
        out_shape=[jax.ShapeDtypeStruct((t, d), BF16), jax.ShapeDtypeStruct((ne, t), F32),
                   jax.ShapeDtypeStruct((ne, t), F32), jax.ShapeDtypeStruct((nt, ne, 128), jnp.int32)],
        compiler_params=_cparams(("parallel",)),
        name="moe_router",
    )(h, g.reshape(1, d), wr_t)


def _pick(pos, base, rows):
    slot = lax.broadcasted_iota(jnp.int32, (rows, pos.shape[1]), 0).astype(F32)
    return pos == slot + base.astype(F32)


def _moe_gather_kernel(ce_ref, cb_ref, co_ref, nq_ref, un_ref, pos_ref, xs_in_ref, xs_ref, buf_ref, sem, *, rows, qmax):
    del xs_in_ref
    i = pl.program_id(0)
    n = nq_ref[i]

    def copy(slot, off):
        return pltpu.make_async_copy(buf_ref.at[slot], xs_ref.at[pl.ds(off, rows)], sem.at[slot])

    def body(q, carry):
        k = i * qmax + q
        slot = lax.rem(q, 2)

        @pl.when(q >= 2)
        def _():
            copy(slot, 0).wait()

        pos = pos_ref[pl.ds(ce_ref[k], 1), :]
        sel = jnp.where(_pick(pos, cb_ref[k], rows), 1.0, 0.0).astype(BF16)
        buf_ref[slot] = _dot(sel, un_ref[...]).astype(BF16)
        copy(slot, pl.multiple_of(co_ref[k], MOE_GRANULE)).start()
        return carry

    lax.fori_loop(0, n, body, 0)

    @pl.when(n >= 2)
    def _():
        copy(lax.rem(n, 2), 0).wait()

    @pl.when(n >= 1)
    def _():
        copy(lax.rem(n + 1, 2), 0).wait()


def moe_gather(un, pos, tables, n_rows, tm, rows):
    t, d = un.shape
    ne = pos.shape[0]
    ce, cb, co, nq, qmax = tables
    grid_spec = pltpu.PrefetchScalarGridSpec(
        num_scalar_prefetch=4,
        grid=(t // tm,),
        in_specs=[pl.BlockSpec((tm, d), lambda i, *_: (i, 0)),
                  pl.BlockSpec((ne, tm), lambda i, *_: (0, i)),
                  pl.BlockSpec(memory_space=pl.ANY)],
        out_specs=pl.BlockSpec(memory_space=pl.ANY),
        scratch_shapes=[pltpu.VMEM((2, rows, d), BF16), pltpu.SemaphoreType.DMA((2,))],
    )
    return pl.pallas_call(
        functools.partial(_moe_gather_kernel, rows=rows, qmax=qmax),
        grid_spec=grid_spec,
        out_shape=jax.ShapeDtypeStruct((n_rows, d), BF16),
        input_output_aliases={6: 0},
        compiler_params=_cparams(("arbitrary",)),
        name="moe_gather",
    )(ce, cb, co, nq, un, pos, jnp.zeros((n_rows, d), BF16))


def _moe_ffn_kernel(be_ref, nu_ref, x_ref, wg_ref, wu_ref, wd_ref, o_ref, acc_ref):
    del be_ref
    b, f = pl.program_id(0), pl.program_id(1)
    nf = pl.num_programs(1)
    used = b < nu_ref[0]

    @pl.when(f == 0)
    def _():
        acc_ref[...] = jnp.zeros_like(acc_ref)

    @pl.when(used)
    def _():
        x = x_ref[...]
        hb = (_silu(_dot(x, wg_ref[0])) * _dot(x, wu_ref[0])).astype(BF16)
        acc_ref[...] += _dot(hb, wd_ref[0])

    @pl.when(f == nf - 1)
    def _():
        o_ref[...] = acc_ref[...].astype(BF16)


def moe_ffn(xs, blk_e, n_used, wg, wu, wd, bm, tf=512):
    n_rows, d = xs.shape
    ff = wg.shape[2]
    nf = ff // tf

    def fidx(b, f, nu):
        return jnp.where(b < nu[0], f, nf - 1)

    grid_spec = pltpu.PrefetchScalarGridSpec(
        num_scalar_prefetch=2,
        grid=(n_rows // bm, nf),
        in_specs=[pl.BlockSpec((bm, d), lambda b, f, be, nu: (b, 0)),
                  pl.BlockSpec((1, d, tf), lambda b, f, be, nu: (be[b], 0, fidx(b, f, nu))),
                  pl.BlockSpec((1, d, tf), lambda b, f, be, nu: (be[b], 0, fidx(b, f, nu))),
                  pl.BlockSpec((1, tf, d), lambda b, f, be, nu: (be[b], fidx(b, f, nu), 0))],
        out_specs=pl.BlockSpec((bm, d), lambda b, f, be, nu: (b, 0)),
        scratch_shapes=[pltpu.VMEM((bm, d), F32)],
    )
    return pl.pallas_call(
        _moe_ffn_kernel,
        grid_spec=grid_spec,
        out_shape=jax.ShapeDtypeStruct((n_rows, d), BF16),
        compiler_params=_cparams(("parallel", "arbitrary")),
        name="moe_ffn",
    )(blk_e, n_used, xs, wg, wu, wd)


def _moe_combine_kernel(ce_ref, cb_ref, co_ref, nq_ref, pos_ref, gate_ref, y_ref, o_ref, buf_ref, sem, *, rows, qmax):
    i = pl.program_id(0)
    n = nq_ref[i]
    o_ref[...] = jnp.zeros_like(o_ref)

    def copy(slot, off):
        return pltpu.make_async_copy(y_ref.at[pl.ds(off, rows)], buf_ref.at[slot], sem.at[slot])

    def start(q):
        copy(lax.rem(q, 2), pl.multiple_of(co_ref[i * qmax + q], MOE_GRANULE)).start()

    @pl.when(n > 0)
    def _():
        start(0)

    def body(q, carry):
        k = i * qmax + q
        slot = lax.rem(q, 2)

        @pl.when(q + 1 < n)
        def _():
            start(q + 1)

        copy(slot, 0).wait()
        e = ce_ref[k]
        hit = _pick(pos_ref[pl.ds(e, 1), :], cb_ref[k], rows)
        gsub = jnp.sum(jnp.where(hit, gate_ref[pl.ds(e, 1), :], 0.0), axis=1, keepdims=True)
        yb = (buf_ref[slot].astype(F32) * gsub).astype(BF16)
        o_ref[...] += _dot_tn(jnp.where(hit, 1.0, 0.0).astype(BF16), yb)
        return carry

    lax.fori_loop(0, n, body, 0)


def moe_combine(ys, pos, gate, tables, tm, rows):
    ne, t = pos.shape
    d = ys.shape[1]
    ce, cb, co, nq, qmax = tables
    grid_spec = pltpu.PrefetchScalarGridSpec(
        num_scalar_prefetch=4,
        grid=(t // tm,),
        in_specs=[pl.BlockSpec((ne, tm), lambda i, *_: (0, i)),
                  pl.BlockSpec((ne, tm), lambda i, *_: (0, i)),
                  pl.BlockSpec(memory_space=pl.ANY)],
        out_specs=pl.BlockSpec((tm, d), lambda i, *_: (i, 0)),
        scratch_shapes=[pltpu.VMEM((2, rows, d), BF16), pltpu.SemaphoreType.DMA((2,))],
    )
    return pl.pallas_call(
        functools.partial(_moe_combine_kernel, rows=rows, qmax=qmax),
        grid_spec=grid_spec,
        out_shape=jax.ShapeDtypeStruct((t, d), F32),
        compiler_params=_cparams(("arbitrary",)),
        name="moe_combine",
    )(ce, cb, co, nq, pos, gate, ys)


def _chunk_tables(counts, seg, rows, qmax):
    ne = counts.shape[1]
    ns = (counts + rows - 1) // rows
    cs = jnp.cumsum(ns, axis=1)
    q = jnp.arange(qmax, dtype=jnp.int32)
    ce = jnp.minimum(jnp.sum(q[None, :, None] >= cs[:, None, :], axis=-1), ne - 1).astype(jnp.int32)
    cj = q[None, :] - jnp.take_along_axis(cs - ns, ce, axis=1)
    co = jnp.take_along_axis(seg, ce, axis=1) + cj * rows
    flat = lambda a: a.reshape(-1).astype(jnp.int32)
    return flat(ce), flat(cj * rows), flat(co), cs[:, -1].astype(jnp.int32), qmax


def moe(h, g, w_router, wg, wu, wd, tm=MOE_TILE, bm=MOE_BLOCK):
    t = h.shape[0]
    tm = min(tm, t)
    ne = wg.shape[0]
    nt = t // tm
    un, pos, gate, cnt = router(h, g, w_router.T, tm)
    counts = cnt[:, :, 0]
    padded = (counts + MOE_GRANULE - 1) // MOE_GRANULE * MOE_GRANULE
    tot = jnp.sum(padded, axis=0)
    ptot = (tot + MOE_SLACK + bm - 1) // bm * bm
    eend = jnp.cumsum(ptot)
    seg = (eend - ptot)[None, :] + jnp.cumsum(padded, axis=0) - padded
    n_blocks = (TOP_K * t + nt * ne * (MOE_GRANULE - 1) + ne * MOE_SLACK) // bm + ne
    n_used = (eend[-1:] // bm).astype(jnp.int32)
    blk_e = jnp.minimum(jnp.searchsorted(eend, jnp.arange(n_blocks, dtype=jnp.int32) * bm, side="right"),
                        ne - 1).astype(jnp.int32)
    g_tab = _chunk_tables(counts, seg, MOE_GATHER_ROWS, TOP_K * tm // MOE_GATHER_ROWS + ne)
    c_tab = _chunk_tables(counts, seg, MOE_COMBINE_ROWS, TOP_K * tm // MOE_COMBINE_ROWS + ne)
    xs = moe_gather(un, pos, g_tab, n_blocks * bm, tm, MOE_GATHER_ROWS)
    ys = moe_ffn(xs, blk_e, n_used, wg, wu, wd, bm)
    return moe_combine(ys, pos, gate, c_tab, tm, MOE_COMBINE_ROWS)


def kernel(x, p, ln_mix, ln_ffn, ln_ple, ln_final, lb_table, ab_w_in, ab_conv, b_a_log, b_dt_bias, a_gnorm, b_gnorm, ab_w_out, c_w_in, c_conv_w, c_conv_b, c_w_r, c_b_r, c_w_i, c_b_i, c_lambda, c_w_out, ffn_w_gate, ffn_w_up, ffn_w_down, moe_router, moe_w_gate, moe_w_up, moe_w_down, ple_w_proj, ple_w_gate):
    bsz, seq, d = x.shape
    t = bsz * seq
    depth = ln_mix.shape[0]
    a_heads = lb_table.shape[1] // HEAD_DIM
    b_heads = b_a_log.shape[1]
    a_w = a_heads * HEAD_DIM
    b_w = b_heads * HEAD_DIM
    main_w = 4 * a_w + 4 * b_w
    bf = lambda a: a.astype(BF16)

    h = x.reshape(t, d)
    for layer in range(depth):
        j = layer // 2
        if layer % 2 == 0:
            w_in = ab_w_in[j]
            w_small = jnp.pad(w_in[:, main_w:], ((0, 0), (0, 128 - 2 * b_heads)))
            proj, small = norm_proj(h, ln_mix[layer], bf(w_in[:, :main_w]), bf(w_small))
            o_a = hgrn2(proj, lb_table, a_gnorm[j], bsz=bsz, seq=seq, heads=a_heads, layer=layer, col0=0)
            o_b = gdn(proj, small, ab_conv[j], b_a_log[j], b_dt_bias[j], b_gnorm[j],
                      bsz=bsz, seq=seq, heads=b_heads, col0=4 * a_w)
            h = out_proj2(o_a, o_b, bf(ab_w_out[j]), h)
            h = swiglu(h, ln_ffn[layer], bf(ffn_w_gate[j]), bf(ffn_w_up[j]), bf(ffn_w_down[j]))
            add = None
        else:
            xr, yb = norm_proj_gelu(h, ln_mix[layer], bf(c_w_in[j]))
            hy = rglru(xr, yb, c_conv_w[j], c_conv_b[j], bf(c_w_r[j]), c_b_r[j], bf(c_w_i[j]), c_b_i[j],
                       c_lambda[j], bsz=bsz, seq=seq)
            h = out_proj(hy, bf(c_w_out[j]), h)
            add = moe(h, ln_ffn[layer], moe_router[j], bf(moe_w_gate[j]), bf(moe_w_up[j]), bf(moe_w_down[j]))
        g_final = ln_final if layer == depth - 1 else None
        h = ple(h, p[layer].reshape(t, -1), ln_ple[layer], bf(ple_w_gate[layer]), bf(ple_w_proj[layer]),
                add=add, g_final=g_final)
    if depth == 0:
        raise ValueError("depth must be positive")
    return h.reshape(bsz, seq, d)
```

```python
import functools

import jax
import jax.numpy as jnp
from jax import lax
from jax.experimental import pallas as pl
from jax.experimental.pallas import tpu as pltpu

F32 = jnp.float32
BF16 = jnp.bfloat16
EPS = 1e-6
CHUNK = 64
SUB = 8
HEAD_DIM = 128
CONV_WIDTH = 4
GDN_HEAD_GROUP = 8
FILL_AFTER_PREP = 4
FILL_AFTER_SCORES = 4
FILL_PER_LEVEL = 1
CONV_TAIL = 8
RGLRU_C = 8.0
SCAN_GROUP = 8
TOP_K = 2
MOE_TILE = 1024
MOE_BLOCK = 1024
MOE_GRANULE = 16
MOE_GATHER_ROWS = 128
MOE_COMBINE_ROWS = 256
MOE_SLACK = 256
VMEM_LIMIT = 56 * 1024 * 1024


def _cparams(sem, vmem=VMEM_LIMIT):
    return pltpu.CompilerParams(dimension_semantics=sem, vmem_limit_bytes=vmem)


def _dot(a, b):
    return jnp.dot(a, b, preferred_element_type=F32)


def _dot_nt(a, b):
    return lax.dot_general(a, b, (((1,), (1,)), ((), ())), preferred_element_type=F32)


def _dot_tn(a, b):
    return lax.dot_general(a, b, (((0,), (0,)), ((), ())), preferred_element_type=F32)


def _split(a):
    hi = a.astype(BF16)
    lo = (a - hi.astype(F32)).astype(BF16)
    return hi, lo


def _dot3(a, b):
    ah, al = _split(a)
    bh, bl = _split(b)
    return _dot(jnp.concatenate([ah, al, ah], axis=1), jnp.concatenate([bh, bh, bl], axis=0))


def _rms(x, g):
    return x * lax.rsqrt(jnp.mean(x * x, axis=-1, keepdims=True) + EPS) * g


def _sigmoid(x):
    return 1.0 / (1.0 + jnp.exp(-x))


def _silu(x):
    return x * _sigmoid(x)


def _norm_proj_kernel(h_ref, g_ref, w_ref, ws_ref, o_ref, os_ref, un_ref):
    j = pl.program_id(1)

    @pl.when(j == 0)
    def _():
        un = _rms(h_ref[...], g_ref[...]).astype(BF16)
        un_ref[...] = un
        os_ref[...] = _dot(un, ws_ref[...])

    o_ref[...] = _dot(un_ref[...], w_ref[...])


def norm_proj(h, g, w, ws, tm=1024, tn=512):
    t, d = h.shape
    tm = min(tm, t)
    n = w.shape[1]
    return pl.pallas_call(
        _norm_proj_kernel,
        grid=(t // tm, n // tn),
        in_specs=[pl.BlockSpec((tm, d), lambda i, j: (i, 0)),
                  pl.BlockSpec((1, d), lambda i, j: (0, 0)),
                  pl.BlockSpec((d, tn), lambda i, j: (0, j)),
                  pl.BlockSpec((d, ws.shape[1]), lambda i, j: (0, 0))],
        out_specs=[pl.BlockSpec((tm, tn), lambda i, j: (i, j)),
                   pl.BlockSpec((tm, ws.shape[1]), lambda i, j: (i, 0))],
        out_shape=[jax.ShapeDtypeStruct((t, n), F32), jax.ShapeDtypeStruct((t, ws.shape[1]), F32)],
        scratch_shapes=[pltpu.VMEM((tm, d), BF16)],
        compiler_params=_cparams(("parallel", "arbitrary")),
        name="norm_proj",
    )(h, g.reshape(1, d), w, ws)


def _norm_proj_gelu_kernel(h_ref, g_ref, wy_ref, wx_ref, o_ref, y_ref, un_ref):
    @pl.when(pl.program_id(1) == 0)
    def _():
        un_ref[...] = _rms(h_ref[...], g_ref[...]).astype(BF16)

    un = un_ref[...]
    y_ref[...] = jax.nn.gelu(_dot(un, wy_ref[...])).astype(BF16)
    o_ref[...] = _dot(un, wx_ref[...])


def norm_proj_gelu(h, g, w, tm=1024, tn=512):
    t, d = h.shape
    tm = min(tm, t)
    half = w.shape[1] // 2
    nh = half // tn
    return pl.pallas_call(
        _norm_proj_gelu_kernel,
        grid=(t // tm, nh),
        in_specs=[pl.BlockSpec((tm, d), lambda i, j: (i, 0)),
                  pl.BlockSpec((1, d), lambda i, j: (0, 0)),
                  pl.BlockSpec((d, tn), lambda i, j: (0, j)),
                  pl.BlockSpec((d, tn), lambda i, j: (0, nh + j))],
        out_specs=[pl.BlockSpec((tm, tn), lambda i, j: (i, j)),
                   pl.BlockSpec((tm, tn), lambda i, j: (i, j))],
        out_shape=[jax.ShapeDtypeStruct((t, half), F32), jax.ShapeDtypeStruct((t, half), BF16)],
        scratch_shapes=[pltpu.VMEM((tm, d), BF16)],
        compiler_params=_cparams(("parallel", "arbitrary")),
        name="norm_proj_gelu",
    )(h, g.reshape(1, d), w, w)


def _hgrn2_work(q_ref, f_ref, i_ref, g_ref, lbt_ref, gn_ref, o_ref, st_ref, *, layer, heads):
    c = CHUNK

    lbt = lbt_ref[...]
    e = jnp.exp(lbt - jnp.max(lbt, axis=0, keepdims=True))
    lb_all = jnp.sum(e[:layer + 1], axis=0, keepdims=True) / jnp.sum(e, axis=0, keepdims=True)

    row = lax.broadcasted_iota(jnp.int32, (c, c), 0)
    col = lax.broadcasted_iota(jnp.int32, (c, c), 1)
    tril = jnp.where(row >= col, 1.0, 0.0).astype(BF16)
    gn = gn_ref[...]

    levels = []
    ln = c // 2
    while ln >= SUB:
        levels += [(m * 2 * ln, m * 2 * ln + ln, ln) for m in range(c // (2 * ln))]
        ln //= 2
    n_pairs = sum(l[2] for l in levels)

    def seg_id(idx):
        sid = jnp.zeros_like(idx)
        start = 0
        for l in levels[:-1]:
            start += l[2]
            sid = sid + jnp.where(idx >= start, 1, 0)
        return sid

    same_seg = (seg_id(lax.broadcasted_iota(jnp.int32, (n_pairs, n_pairs), 0))
                == seg_id(lax.broadcasted_iota(jnp.int32, (n_pairs, n_pairs), 1)))
    sub_i = lax.broadcasted_iota(jnp.int32, (c // SUB, SUB, HEAD_DIM), 1)

    sls = [slice(h * HEAD_DIM, (h + 1) * HEAD_DIM) for h in range(heads)]

    def gates(h):
        lb = lb_all[:, sls[h]]
        q = q_ref[:, sls[h]] * (HEAD_DIM ** -0.5)
        forget = lb + (1.0 - lb) * _sigmoid(f_ref[:, sls[h]])
        lh, ll = _split(jnp.log(forget))
        b2 = _dot(tril, jnp.concatenate([lh, ll], axis=1))
        return q, 1.0 - forget, i_ref[:, sls[h]], b2[:, :HEAD_DIM] + b2[:, HEAD_DIM:]

    def block_pairs(h, q, k, v, b):
        o = _dot_nt((q * jnp.exp(b)).astype(BF16), st_ref[h].astype(BF16))
        qs, ks, vs = [], [], []
        for k0, q0, ln in levels:
            bref = b[q0 - 1:q0, :]
            qs.append(q[q0:q0 + ln] * jnp.exp(b[q0:q0 + ln] - bref))
            ks.append(k[k0:k0 + ln] * jnp.exp(bref - b[k0:k0 + ln]))
            vs.append(v[k0:k0 + ln])
        s = _dot_nt(jnp.concatenate(qs, axis=0).astype(BF16), jnp.concatenate(ks, axis=0).astype(BF16))
        r = _dot(jnp.where(same_seg, s, 0.0).astype(BF16), jnp.concatenate(vs, axis=0).astype(BF16))
        groups = [None] * (c // SUB)
        start = 0
        for k0, q0, ln in levels:
            for j in range(ln // SUB):
                piece = r[start + j * SUB:start + (j + 1) * SUB]
                gi = q0 // SUB + j
                groups[gi] = piece if groups[gi] is None else groups[gi] + piece
            start += ln
        groups[0] = jnp.zeros((SUB, HEAD_DIM), F32)
        return o + jnp.concatenate(groups, axis=0)

    def near_pairs(q, k, v, b):
        q3, k3, v3, b3 = (a.reshape(c // SUB, SUB, HEAD_DIM) for a in (q, k, v, b))
        o3 = jnp.sum(q3 * k3, axis=2, keepdims=True) * v3
        for d in range(1, SUB):
            dec = jnp.exp(jnp.where(sub_i >= d, b3 - pltpu.roll(b3, d, 1), -jnp.inf))
            w = jnp.sum(q3 * pltpu.roll(k3, d, 1) * dec, axis=2, keepdims=True)
            o3 = o3 + w * pltpu.roll(v3, d, 1)
        return o3.reshape(c, HEAD_DIM)

    qkvb, far = {}, {}

    def state_part(h):
        qkvb[h] = gates(h)
        q, k, v, b = qkvb[h]
        far[h] = block_pairs(h, q, k, v, b)
        blast = b[c - 1:c, :]
        kd = (k * jnp.exp(blast - b)).astype(BF16)
        st_ref[h] = st_ref[h] * jnp.exp(blast) + _dot_tn(v.astype(BF16), kd)

    def block_part(h):
        o = far[h] + near_pairs(*qkvb[h])
        on = o * lax.rsqrt(jnp.mean(o * o, axis=1, keepdims=True) + EPS) * gn
        o_ref[:, sls[h]] = (on * _silu(g_ref[:, sls[h]])).astype(BF16)

    return ([functools.partial(state_part, h) for h in range(heads)]
            + [functools.partial(block_part, h) for h in range(heads)])


def _causal_conv(ext_ref, x, w, first):
    n = x.shape[0]

    @pl.when(first)
    def _():
        ext_ref[0:CONV_TAIL, :] = jnp.zeros((CONV_TAIL, x.shape[1]), F32)

    ext_ref[CONV_TAIL:CONV_TAIL + n, :] = x
    y = x * w[CONV_WIDTH - 1:CONV_WIDTH, :]
    for k in range(CONV_WIDTH - 1):
        off = CONV_TAIL - (CONV_WIDTH - 1) + k
        y = y + ext_ref[off:off + n, :] * w[k:k + 1, :]
    ext_ref[0:CONV_TAIL, :] = ext_ref[n:n + CONV_TAIL, :]
    return y


def _mixer_ab_kernel(qa_ref, fa_ref, ia_ref, ga_ref, q_ref, k_ref, v_ref, z_ref, sm_ref, lbt_ref, gna_ref, cw_ref,
                     alog_ref, dtb_ref, gn_ref, o_ref, sta_ref, st_ref, eq_ref, ek_ref, ev_ref,
                     *, heads_a, heads, layer):
    c = CHUNK
    first = pl.program_id(1) == 0

    @pl.when(first)
    def _():
        sta_ref[...] = jnp.zeros_like(sta_ref)
        st_ref[...] = jnp.zeros_like(st_ref)

    pending = _hgrn2_work(qa_ref, fa_ref, ia_ref, ga_ref, lbt_ref, gna_ref, o_ref, sta_ref, layer=layer, heads=heads_a)

    def fill(n):
        for _ in range(min(n, len(pending))):
            pending.pop(0)()

    col0 = heads_a * HEAD_DIM

    hw = heads * HEAD_DIM
    cw = cw_ref[...]
    qc = _silu(_causal_conv(eq_ref, q_ref[...], cw[:, 0:hw], first))
    kc = _silu(_causal_conv(ek_ref, k_ref[...], cw[:, hw:2 * hw], first))
    vc = _silu(_causal_conv(ev_ref, v_ref[...], cw[:, 2 * hw:3 * hw], first))

    row = lax.broadcasted_iota(jnp.int32, (c, c), 0)
    col = lax.broadcasted_iota(jnp.int32, (c, c), 1)
    causal = row >= col
    strict = row > col
    eye = row == col
    eye_f = jnp.where(eye, 1.0, 0.0)
    diag_blk = (row // SUB) == (col // SUB)
    merge_masks = []
    s = SUB
    while s < c:
        merge_masks.append(jnp.logical_and((row // (2 * s)) == (col // (2 * s)), (row // s) == (col // s) + 1))
        s *= 2
    sm = sm_ref[...]
    gn = gn_ref[...]

    def to_row(colv):
        return jnp.sum(jnp.where(eye, colv, 0.0), axis=0, keepdims=True)

    def head_group(hg):
        hs = range(len(hg))
        sls = [slice(h * HEAD_DIM, (h + 1) * HEAD_DIM) for h in hg]
        qn, kn, knb, beta, gam_col, egam, decay, kbeta = [], [], [], [], [], [], [], []
        for i, h in enumerate(hg):
            qh, kh = qc[:, sls[i]], kc[:, sls[i]]
            qn.append(qh * lax.rsqrt(jnp.sum(qh * qh, axis=1, keepdims=True) + EPS) * (HEAD_DIM ** -0.5))
            kn.append(kh * lax.rsqrt(jnp.sum(kh * kh, axis=1, keepdims=True) + EPS))
            knb.append(kn[i].astype(BF16))
            beta.append(_sigmoid(sm[:, heads + h:heads + h + 1]))
            g_col = -jnp.exp(alog_ref[:, h:h + 1]) * jax.nn.softplus(sm[:, h:h + 1] + dtb_ref[:, h:h + 1])
            g_row = to_row(g_col)
            gam_col.append(jnp.sum(jnp.where(causal, g_row, 0.0), axis=1, keepdims=True))
            gam_row = jnp.sum(jnp.where(strict, 0.0, g_col), axis=0, keepdims=True)
            decay.append(jnp.exp(jnp.where(causal, gam_col[i] - gam_row, -jnp.inf)))
            egam.append(jnp.exp(gam_col[i]))
            kbeta.append(kn[i] * beta[i])

        fill(FILL_AFTER_PREP)
        a_mat = [jnp.where(strict, _dot_nt(kbeta[i].astype(BF16), knb[i]) * decay[i], 0.0) for i in hs]
        qk = [(_dot_nt(qn[i].astype(BF16), knb[i]) * decay[i]).astype(BF16) for i in hs]
        fill(FILL_AFTER_SCORES)
        d_mat = [jnp.where(diag_blk, a_mat[i], 0.0) for i in hs]
        x = [eye_f - d_mat[i] for i in hs]
        p = [_dot3(d_mat[i], d_mat[i]) for i in hs]
        fill(FILL_PER_LEVEL)
        n_sq = (SUB - 1).bit_length() - 1
        for lvl in range(n_sq):
            if lvl < n_sq - 1:
                y = [_dot3(jnp.concatenate([x[i], p[i]], axis=0), p[i]) for i in hs]
                fill(FILL_PER_LEVEL)
                x = [x[i] + y[i][:c] for i in hs]
                p = [y[i][c:] for i in hs]
            else:
                y = [_dot3(x[i], p[i]) for i in hs]
                fill(FILL_PER_LEVEL)
                x = [x[i] + y[i] for i in hs]
        for below in merge_masks:
            y = [_dot3(x[i], jnp.where(below, a_mat[i], 0.0)) for i in hs]
            fill(FILL_PER_LEVEL)
            y = [_dot3(y[i], x[i]) for i in hs]
            fill(FILL_PER_LEVEL)
            x = [x[i] - y[i] for i in hs]
        rhs = [jnp.concatenate([vc[:, sls[i]] * beta[i], kbeta[i] * egam[i]], axis=1).astype(BF16) for i in hs]
        uw = [_dot(x[i].astype(BF16), rhs[i]) for i in hs]

        st = [st_ref[h] for h in hg]
        stb = [s.astype(BF16) for s in st]
        v_new = [uw[i][:, :HEAD_DIM] - _dot_nt(uw[i][:, HEAD_DIM:].astype(BF16), stb[i]) for i in hs]
        vnb = [v.astype(BF16) for v in v_new]
        o = [_dot_nt((qn[i] * egam[i]).astype(BF16), stb[i]) + _dot(qk[i], vnb[i]) for i in hs]
        for i, h in enumerate(hg):
            glast = gam_col[i][c - 1:c, :]
            kd = (kn[i] * jnp.exp(glast - gam_col[i])).astype(BF16)
            st_ref[h] = st[i] * jnp.exp(glast) + _dot_tn(vnb[i], kd)
        for i in hs:
            on = o[i] * lax.rsqrt(jnp.mean(o[i] * o[i], axis=1, keepdims=True) + EPS) * gn
            o_ref[:, col0 + hg[i] * HEAD_DIM:col0 + (hg[i] + 1) * HEAD_DIM] = (on * _silu(z_ref[:, sls[i]])).astype(BF16)

    for h0 in range(0, heads, GDN_HEAD_GROUP):
        head_group(list(range(h0, min(h0 + GDN_HEAD_GROUP, heads))))
    fill(len(pending))


def mixer_ab(proj, small, lb_table, gnorm_a, conv_w, a_log, dt_bias, gnorm_b, *, bsz, seq, heads_a, heads_b, layer):
    t = proj.shape[0]
    wa, wb = heads_a * HEAD_DIM, heads_b * HEAD_DIM
    nc = seq // CHUNK
    row = lambda b, s: b * nc + s

    def spec_a(k):
        return pl.BlockSpec((CHUNK, wa), lambda b, s: (row(b, s), k))

    def spec_b(k):
        return pl.BlockSpec((CHUNK, wb), lambda b, s: (row(b, s), 4 * wa // wb + k))

    def const(shape):
        return pl.BlockSpec(shape, lambda b, s: (0, 0))

    return pl.pallas_call(
        functools.partial(_mixer_ab_kernel, heads_a=heads_a, heads=heads_b, layer=layer),
        grid=(bsz, nc),
        in_specs=[spec_a(0), spec_a(1), spec_a(2), spec_a(3), spec_b(0), spec_b(1), spec_b(2), spec_b(3),
                  pl.BlockSpec((CHUNK, small.shape[1]), lambda b, s: (row(b, s), 0)),
                  const(lb_table.shape), const((1, HEAD_DIM)),
                  const(conv_w.shape), const((1, heads_b)), const((1, heads_b)), const((1, HEAD_DIM))],
        out_specs=pl.BlockSpec((CHUNK, wa + wb), lambda b, s: (row(b, s), 0)),
        out_shape=jax.ShapeDtypeStruct((t, wa + wb), BF16),
        scratch_shapes=[pltpu.VMEM((heads_a, HEAD_DIM, HEAD_DIM), F32), pltpu.VMEM((heads_b, HEAD_DIM, HEAD_DIM), F32)]
        + [pltpu.VMEM((CHUNK + CONV_TAIL, wb), F32)] * 3,
        compiler_params=_cparams(("parallel", "arbitrary")),
        name="mixer_ab",
    )(proj, proj, proj, proj, proj, proj, proj, proj, small, lb_table, gnorm_a.reshape(1, HEAD_DIM), conv_w,
      a_log.reshape(1, heads_b), dt_bias.reshape(1, heads_b), gnorm_b.reshape(1, HEAD_DIM))


def _out_proj_kernel(a_ref, w_ref, h_ref, o_ref):
    o_ref[...] = h_ref[...] + _dot(a_ref[...], w_ref[...])


def out_proj(a, w, h, tm=1024, tn=512):
    t, k = a.shape
    tm = min(tm, t)
    n = w.shape[1]
    return pl.pallas_call(
        _out_proj_kernel,
        grid=(t // tm, n // tn),
        in_specs=[pl.BlockSpec((tm, k), lambda i, j: (i, 0)),
                  pl.BlockSpec((k, tn), lambda i, j: (0, j)),
                  pl.BlockSpec((tm, tn), lambda i, j: (i, j))],
        out_specs=pl.BlockSpec((tm, tn), lambda i, j: (i, j)),
        out_shape=jax.ShapeDtypeStruct((t, n), F32),
        compiler_params=_cparams(("parallel", "arbitrary")),
        name="out_proj",
    )(a, w, h)


def _swiglu_kernel(h_ref, g_ref, wg_ref, wu_ref, wd_ref, o_ref, un_ref):
    f = pl.program_id(1)

    @pl.when(f == 0)
    def _():
        un_ref[...] = _rms(h_ref[...], g_ref[...]).astype(BF16)
        o_ref[...] = h_ref[...]

    un = un_ref[...]
    hb = (_silu(_dot(un, wg_ref[...])) * _dot(un, wu_ref[...])).astype(BF16)
    o_ref[...] += _dot(hb, wd_ref[...])


def swiglu(h, g, wg, wu, wd, tm=1024, tf=512):
    t, d = h.shape
    tm = min(tm, t)
    ff = wg.shape[1]
    return pl.pallas_call(
        _swiglu_kernel,
        grid=(t // tm, ff // tf),
        in_specs=[pl.BlockSpec((tm, d), lambda i, f: (i, 0), pipeline_mode=pl.Buffered(1)),
                  pl.BlockSpec((1, d), lambda i, f: (0, 0)),
                  pl.BlockSpec((d, tf), lambda i, f: (0, f)),
                  pl.BlockSpec((d, tf), lambda i, f: (0, f)),
                  pl.BlockSpec((tf, d), lambda i, f: (f, 0))],
        out_specs=pl.BlockSpec((tm, d), lambda i, f: (i, 0)),
        out_shape=jax.ShapeDtypeStruct((t, d), F32),
        scratch_shapes=[pltpu.VMEM((tm, d), BF16)],
        compiler_params=_cparams(("parallel", "arbitrary")),
        name="swiglu",
    )(h, g.reshape(1, d), wg, wu, wd)


def _ple_kernel(*refs, has_add, has_final):
    h_ref, p_ref, g_ref, wg_ref, wp_ref = refs[:5]
    k = 5
    add_ref = gf_ref = None
    if has_add:
        add_ref = refs[k]
        k += 1
    if has_final:
        gf_ref = refs[k]
        k += 1
    o_ref = refs[k]
    h = h_ref[...]
    if has_add:
        h = h + add_ref[...]
    un = _rms(h, g_ref[...]).astype(BF16)
    gate = _sigmoid(_dot(un, wg_ref[...]))
    out = h + gate * _dot(p_ref[...].astype(BF16), wp_ref[...])
    if has_final:
        out = _rms(out, gf_ref[...])
    o_ref[...] = out


def ple(h, p, g, wg, wp, add=None, g_final=None, tm=512):
    t, d = h.shape
    tm = min(tm, t)
    pd = p.shape[1]
    row = lambda i: (i, 0)
    const = lambda i: (0, 0)
    in_specs = [pl.BlockSpec((tm, d), row), pl.BlockSpec((tm, pd), row), pl.BlockSpec((1, d), const),
                pl.BlockSpec((d, d), const), pl.BlockSpec((pd, d), const)]
    args = [h, p, g.reshape(1, d), wg, wp]
    if add is not None:
        in_specs.append(pl.BlockSpec((tm, d), row))
        args.append(add)
    if g_final is not None:
        in_specs.append(pl.BlockSpec((1, d), const))
        args.append(g_final.reshape(1, d))
    return pl.pallas_call(
        functools.partial(_ple_kernel, has_add=add is not None, has_final=g_final is not None),
        grid=(t // tm,),
        in_specs=in_specs,
        out_specs=pl.BlockSpec((tm, d), row),
        out_shape=jax.ShapeDtypeStruct((t, d), F32),
        compiler_params=_cparams(("parallel",)),
        name="ple",
    )(*args)


def _rglru_kernel(x_ref, y_ref, cw_ref, cb_ref, wr_ref, br_ref, wi_ref, bi_ref, lam_ref, o_ref,
                  ext_ref, hc_ref, *, blocks):
    n = x_ref.shape[0]
    first = pl.program_id(1) == 0

    @pl.when(first)
    def _():
        hc_ref[...] = jnp.zeros_like(hc_ref)

    xc = _causal_conv(ext_ref, x_ref[...], cw_ref[...], first) + cb_ref[...]
    bw = xc.shape[1] // blocks
    rowi = lax.broadcasted_iota(jnp.int32, (n, bw), 0)
    at_start = jnp.logical_and(first, rowi == 0)
    gidx = lax.broadcasted_iota(jnp.int32, (n // SCAN_GROUP, SCAN_GROUP, bw), 1)

    for nb in range(blocks):
        sl = slice(nb * bw, (nb + 1) * bw)
        xb = xc[:, sl]
        xbb = xb.astype(BF16)
        r = _sigmoid(_dot(xbb, wr_ref[nb]) + br_ref[:, sl])
        gi = _sigmoid(_dot(xbb, wi_ref[nb]) + bi_ref[:, sl])
        log_a = -RGLRU_C * r * jax.nn.softplus(-lam_ref[:, sl])
        a = jnp.exp(log_a)
        m2 = 1.0 - a * a
        mult = jnp.where(m2 > 0.0, m2 * lax.rsqrt(m2), 0.0)
        mult = jnp.where(at_start, 1.0, mult)
        b = mult * gi * xb
        a = a.reshape(n // SCAN_GROUP, SCAN_GROUP, bw)
        b = b.reshape(n // SCAN_GROUP, SCAN_GROUP, bw)
        sh = 1
        while sh < SCAN_GROUP:
            ok = gidx >= sh
            a_prev = jnp.where(ok, pltpu.roll(a, sh, 1), 1.0)
            b_prev = jnp.where(ok, pltpu.roll(b, sh, 1), 0.0)
            b = b + a * b_prev
            a = a * a_prev
            sh *= 2
        carry = hc_ref[:, sl]
        groups = []
        for gi_ in range(n // SCAN_GROUP):
            hg = b[gi_] + a[gi_] * carry
            groups.append(hg)
            carry = hg[SCAN_GROUP - 1:SCAN_GROUP, :]
        hseq = jnp.concatenate(groups, axis=0)
        hc_ref[:, sl] = carry
        o_ref[:, sl] = (hseq * y_ref[:, sl].astype(F32)).astype(BF16)


def rglru(xr, y, conv_w, conv_b, w_r, b_r, w_i, b_i, lam, *, bsz, seq, rows=256):
    t, cwid = xr.shape
    blocks = w_r.shape[0]
    ns = seq // rows
    row = lambda b, s: (b * ns + s, 0)
    c2 = lambda b, s: (0, 0)
    c3 = lambda b, s: (0, 0, 0)
    vec = lambda a: a.reshape(1, cwid)
    return pl.pallas_call(
        functools.partial(_rglru_kernel, blocks=blocks),
        grid=(bsz, ns),
        in_specs=[pl.BlockSpec((rows, cwid), row), pl.BlockSpec((rows, cwid), row),
                  pl.BlockSpec(conv_w.shape, c2), pl.BlockSpec((1, cwid), c2),
                  pl.BlockSpec(w_r.shape, c3), pl.BlockSpec((1, cwid), c2),
                  pl.BlockSpec(w_i.shape, c3), pl.BlockSpec((1, cwid), c2),
                  pl.BlockSpec((1, cwid), c2)],
        out_specs=pl.BlockSpec((rows, cwid), row),
        out_shape=jax.ShapeDtypeStruct((t, cwid), BF16),
        scratch_shapes=[pltpu.VMEM((rows + CONV_TAIL, cwid), F32), pltpu.VMEM((1, cwid), F32)],
        compiler_params=_cparams(("parallel", "arbitrary")),
        name="rglru",
    )(xr, y, conv_w, vec(conv_b), w_r, vec(b_r), w_i, vec(b_i), vec(lam))


def _router_kernel(h_ref, g_ref, wr_ref, un_ref, pos_ref, gate_ref, cnt_ref):
    tm = h_ref.shape[0]
    ne = wr_ref.shape[0]
    un = _rms(h_ref[...], g_ref[...])
    uh, ul = _split(un)
    un_ref[...] = uh
    wh, wl = _split(wr_ref[...])
    logits = _dot_nt(wh, uh) + _dot_nt(wh, ul) + _dot_nt(wl, uh)
    eidx = lax.broadcasted_iota(jnp.int32, (ne, tm), 0).astype(F32)
    m1 = jnp.max(logits, axis=0, keepdims=True)
    i1 = jnp.min(jnp.where(logits == m1, eidx, float(ne)), axis=0, keepdims=True)
    mask1 = eidx == i1
    rest = jnp.where(mask1, -jnp.inf, logits)
    m2 = jnp.max(rest, axis=0, keepdims=True)
    i2 = jnp.min(jnp.where(rest == m2, eidx, float(ne)), axis=0, keepdims=True)
    mask2 = eidx == i2
    e2 = jnp.exp(m2 - m1)
    g1 = 1.0 / (1.0 + e2)
    g2 = e2 / (1.0 + e2)
    gate_ref[...] = jnp.where(mask1, g1, jnp.where(mask2, g2, 0.0))
    sel = jnp.logical_or(mask1, mask2)
    self32 = jnp.where(sel, 1.0, 0.0)
    before = lax.broadcasted_iota(jnp.int32, (tm, tm), 0) < lax.broadcasted_iota(jnp.int32, (tm, tm), 1)
    rank = _dot(self32.astype(BF16), jnp.where(before, 1.0, 0.0).astype(BF16))
    pos_ref[...] = jnp.where(sel, rank, -1.0)
    cnt = jnp.sum(self32, axis=1, keepdims=True).astype(jnp.int32)
    cnt_ref[0] = jnp.broadcast_to(cnt, cnt_ref.shape[1:])


def router(h, g, wr_t, tm):
    t, d = h.shape
    tm = min(tm, t)
    ne = wr_t.shape[0]
    nt = t // tm
    return pl.pallas_call(
        _router_kernel,
        grid=(nt,),
        in_specs=[pl.BlockSpec((tm, d), lambda i: (i, 0)),
                  pl.BlockSpec((1, d), lambda i: (0, 0)),
                  pl.BlockSpec((ne, d), lambda i: (0, 0))],
        out_specs=[pl.BlockSpec((tm, d), lambda i: (i, 0)),
                   pl.BlockSpec((ne, tm), lambda i: (0, i)),
                   pl.BlockSpec((ne, tm), lambda i: (0, i)),
                   pl.BlockSpec((1, ne, 128), lambda i: (i, 0, 0))],
        out_shape=[jax.ShapeDtypeStruct((t, d), BF16), jax.ShapeDtypeStruct((ne, t), F32),
                   jax.ShapeDtypeStruct((ne, t), F32), jax.ShapeDtypeStruct((nt, ne, 128), jnp.int32)],
        compiler_params=_cparams(("parallel",)),
        name="moe_router",
    )(h, g.reshape(1, d), wr_t)


def _pick(pos, base, rows):
    slot = lax.broadcasted_iota(jnp.int32, (rows, pos.shape[1]), 0).astype(F32)
    return pos == slot + base.astype(F32)


def _moe_gather_kernel(ce_ref, cb_ref, co_ref, nq_ref, un_ref, pos_ref, xs_in_ref, xs_ref, buf_ref, sem, *, rows, qmax):
    del xs_in_ref
    i = pl.program_id(0)
    n = nq_ref[i]

    def copy(slot, off):
        return pltpu.make_async_copy(buf_ref.at[slot], xs_ref.at[pl.ds(off, rows)], sem.at[slot])

    def body(q, carry):
        k = i * qmax + q
        slot = lax.rem(q, 2)

        @pl.when(q >= 2)
        def _():
            copy(slot, 0).wait()

        pos = pos_ref[pl.ds(ce_ref[k], 1), :]
        sel = jnp.where(_pick(pos, cb_ref[k], rows), 1.0, 0.0).astype(BF16)
        buf_ref[slot] = _dot(sel, un_ref[...]).astype(BF16)
        copy(slot, pl.multiple_of(co_ref[k], MOE_GRANULE)).start()
        return carry

    lax.fori_loop(0, n, body, 0)

    @pl.when(n >= 2)
    def _():
        copy(lax.rem(n, 2), 0).wait()

    @pl.when(n >= 1)
    def _():
        copy(lax.rem(n + 1, 2), 0).wait()


def moe_gather(un, pos, tables, n_rows, tm, rows):
    t, d = un.shape
    ne = pos.shape[0]
    ce, cb, co, nq, qmax = tables
    grid_spec = pltpu.PrefetchScalarGridSpec(
        num_scalar_prefetch=4,
        grid=(t // tm,),
        in_specs=[pl.BlockSpec((tm, d), lambda i, *_: (i, 0)),
                  pl.BlockSpec((ne, tm), lambda i, *_: (0, i)),
                  pl.BlockSpec(memory_space=pl.ANY)],
        out_specs=pl.BlockSpec(memory_space=pl.ANY),
        scratch_shapes=[pltpu.VMEM((2, rows, d), BF16), pltpu.SemaphoreType.DMA((2,))],
    )
    return pl.pallas_call(
        functools.partial(_moe_gather_kernel, rows=rows, qmax=qmax),
        grid_spec=grid_spec,
        out_shape=jax.ShapeDtypeStruct((n_rows, d), BF16),
        input_output_aliases={6: 0},
        compiler_params=_cparams(("arbitrary",)),
        name="moe_gather",
    )(ce, cb, co, nq, un, pos, jnp.zeros((n_rows, d), BF16))


def _moe_ffn_kernel(be_ref, nu_ref, x_ref, wg_ref, wu_ref, wd_ref, o_ref, acc_ref):
    del be_ref
    b, f = pl.program_id(0), pl.program_id(1)
    nf = pl.num_programs(1)
    used = b < nu_ref[0]

    @pl.when(f == 0)
    def _():
        acc_ref[...] = jnp.zeros_like(acc_ref)

    @pl.when(used)
    def _():
        x = x_ref[...]
        hb = (_silu(_dot(x, wg_ref[0])) * _dot(x, wu_ref[0])).astype(BF16)
        acc_ref[...] += _dot(hb, wd_ref[0])

    @pl.when(f == nf - 1)
    def _():
        o_ref[...] = acc_ref[...].astype(BF16)


def moe_ffn(xs, blk_e, n_used, wg, wu, wd, bm, tf=512):
    n_rows, d = xs.shape
    ff = wg.shape[2]
    nf = ff // tf

    def fidx(b, f, nu):
        return jnp.where(b < nu[0], f, nf - 1)

    grid_spec = pltpu.PrefetchScalarGridSpec(
        num_scalar_prefetch=2,
        grid=(n_rows // bm, nf),
        in_specs=[pl.BlockSpec((bm, d), lambda b, f, be, nu: (b, 0)),
                  pl.BlockSpec((1, d, tf), lambda b, f, be, nu: (be[b], 0, fidx(b, f, nu))),
                  pl.BlockSpec((1, d, tf), lambda b, f, be, nu: (be[b], 0, fidx(b, f, nu))),
                  pl.BlockSpec((1, tf, d), lambda b, f, be, nu: (be[b], fidx(b, f, nu), 0))],
        out_specs=pl.BlockSpec((bm, d), lambda b, f, be, nu: (b, 0)),
        scratch_shapes=[pltpu.VMEM((bm, d), F32)],
    )
    return pl.pallas_call(
        _moe_ffn_kernel,
        grid_spec=grid_spec,
        out_shape=jax.ShapeDtypeStruct((n_rows, d), BF16),
        compiler_params=_cparams(("parallel", "arbitrary")),
        name="moe_ffn",
    )(blk_e, n_used, xs, wg, wu, wd)


def _moe_combine_kernel(ce_ref, cb_ref, co_ref, nq_ref, pos_ref, gate_ref, y_ref, o_ref, buf_ref, sem, *, rows, qmax):
    i = pl.program_id(0)
    n = nq_ref[i]
    o_ref[...] = jnp.zeros_like(o_ref)

    def copy(slot, off):
        return pltpu.make_async_copy(y_ref.at[pl.ds(off, rows)], buf_ref.at[slot], sem.at[slot])

    def start(q):
        copy(lax.rem(q, 2), pl.multiple_of(co_ref[i * qmax + q], MOE_GRANULE)).start()

    @pl.when(n > 0)
    def _():
        start(0)

    def body(q, carry):
        k = i * qmax + q
        slot = lax.rem(q, 2)

        @pl.when(q + 1 < n)
        def _():
            start(q + 1)

        copy(slot, 0).wait()
        e = ce_ref[k]
        hit = _pick(pos_ref[pl.ds(e, 1), :], cb_ref[k], rows)
        gsub = jnp.sum(jnp.where(hit, gate_ref[pl.ds(e, 1), :], 0.0), axis=1, keepdims=True)
        yb = (buf_ref[slot].astype(F32) * gsub).astype(BF16)
        o_ref[...] += _dot_tn(jnp.where(hit, 1.0, 0.0).astype(BF16), yb)
        return carry

    lax.fori_loop(0, n, body, 0)


def moe_combine(ys, pos, gate, tables, tm, rows):
    ne, t = pos.shape
    d = ys.shape[1]
    ce, cb, co, nq, qmax = tables
    grid_spec = pltpu.PrefetchScalarGridSpec(
        num_scalar_prefetch=4,
        grid=(t // tm,),
        in_specs=[pl.BlockSpec((ne, tm), lambda i, *_: (0, i)),
                  pl.BlockSpec((ne, tm), lambda i, *_: (0, i)),
                  pl.BlockSpec(memory_space=pl.ANY)],
        out_specs=pl.BlockSpec((tm, d), lambda i, *_: (i, 0)),
        scratch_shapes=[pltpu.VMEM((2, rows, d), BF16), pltpu.SemaphoreType.DMA((2,))],
    )
    return pl.pallas_call(
        functools.partial(_moe_combine_kernel, rows=rows, qmax=qmax),
        grid_spec=grid_spec,
        out_shape=jax.ShapeDtypeStruct((t, d), F32),
        compiler_params=_cparams(("arbitrary",)),
        name="moe_combine",
    )(ce, cb, co, nq, pos, gate, ys)


def _chunk_tables(counts, seg, rows, qmax):
    ne = counts.shape[1]
    ns = (counts + rows - 1) // rows
    cs = jnp.cumsum(ns, axis=1)
    q = jnp.arange(qmax, dtype=jnp.int32)
    ce = jnp.minimum(jnp.sum(q[None, :, None] >= cs[:, None, :], axis=-1), ne - 1).astype(jnp.int32)
    cj = q[None, :] - jnp.take_along_axis(cs - ns, ce, axis=1)
    co = jnp.take_along_axis(seg, ce, axis=1) + cj * rows
    flat = lambda a: a.reshape(-1).astype(jnp.int32)
    return flat(ce), flat(cj * rows), flat(co), cs[:, -1].astype(jnp.int32), qmax


def moe(h, g, w_router, wg, wu, wd, tm=MOE_TILE, bm=MOE_BLOCK):
    t = h.shape[0]
    tm = min(tm, t)
    ne = wg.shape[0]
    nt = t // tm
    un, pos, gate, cnt = router(h, g, w_router.T, tm)
    counts = cnt[:, :, 0]
    padded = (counts + MOE_GRANULE - 1) // MOE_GRANULE * MOE_GRANULE
    tot = jnp.sum(padded, axis=0)
    ptot = (tot + MOE_SLACK + bm - 1) // bm * bm
    eend = jnp.cumsum(ptot)
    seg = (eend - ptot)[None, :] + jnp.cumsum(padded, axis=0) - padded
    n_blocks = (TOP_K * t + nt * ne * (MOE_GRANULE - 1) + ne * MOE_SLACK) // bm + ne
    n_used = (eend[-1:] // bm).astype(jnp.int32)
    blk_e = jnp.minimum(jnp.searchsorted(eend, jnp.arange(n_blocks, dtype=jnp.int32) * bm, side="right"),
                        ne - 1).astype(jnp.int32)
    g_tab = _chunk_tables(counts, seg, MOE_GATHER_ROWS, TOP_K * tm // MOE_GATHER_ROWS + ne)
    c_tab = _chunk_tables(counts, seg, MOE_COMBINE_ROWS, TOP_K * tm // MOE_COMBINE_ROWS + ne)
    xs = moe_gather(un, pos, g_tab, n_blocks * bm, tm, MOE_GATHER_ROWS)
    ys = moe_ffn(xs, blk_e, n_used, wg, wu, wd, bm)
    return moe_combine(ys, pos, gate, c_tab, tm, MOE_COMBINE_ROWS)


def kernel(x, p, ln_mix, ln_ffn, ln_ple, ln_final, lb_table, ab_w_in, ab_conv, b_a_log, b_dt_bias, a_gnorm, b_gnorm, ab_w_out, c_w_in, c_conv_w, c_conv_b, c_w_r, c_b_r, c_w_i, c_b_i, c_lambda, c_w_out, ffn_w_gate, ffn_w_up, ffn_w_down, moe_router, moe_w_gate, moe_w_up, moe_w_down, ple_w_proj, ple_w_gate):
    bsz, seq, d = x.shape
    t = bsz * seq
    depth = ln_mix.shape[0]
    a_heads = lb_table.shape[1] // HEAD_DIM
    b_heads = b_a_log.shape[1]
    a_w = a_heads * HEAD_DIM
    b_w = b_heads * HEAD_DIM
    main_w = 4 * a_w + 4 * b_w
    bf = lambda a: a.astype(BF16)

    h = x.reshape(t, d)
    for layer in range(depth):
        j = layer // 2
        if layer % 2 == 0:
            w_in = ab_w_in[j]
            w_small = jnp.pad(w_in[:, main_w:], ((0, 0), (0, 128 - 2 * b_heads)))
            proj, small = norm_proj(h, ln_mix[layer], bf(w_in[:, :main_w]), bf(w_small))
            mixed = mixer_ab(proj, small, lb_table, a_gnorm[j], ab_conv[j], b_a_log[j], b_dt_bias[j], b_gnorm[j],
                             bsz=bsz, seq=seq, heads_a=a_heads, heads_b=b_heads, layer=layer)
            h = out_proj(mixed, bf(ab_w_out[j]), h)
            h = swiglu(h, ln_ffn[layer], bf(ffn_w_gate[j]), bf(ffn_w_up[j]), bf(ffn_w_down[j]))
            add = None
        else:
            xr, yb = norm_proj_gelu(h, ln_mix[layer], bf(c_w_in[j]))
            hy = rglru(xr, yb, c_conv_w[j], c_conv_b[j], bf(c_w_r[j]), c_b_r[j], bf(c_w_i[j]), c_b_i[j],
                       c_lambda[j], bsz=bsz, seq=seq)
            h = out_proj(hy, bf(c_w_out[j]), h)
            add = moe(h, ln_ffn[layer], moe_router[j], bf(moe_w_gate[j]), bf(moe_w_up[j]), bf(moe_w_down[j]))
        g_final = ln_final if layer == depth - 1 else None
        h = ple(h, p[layer].reshape(t, -1), ln_ple[layer], bf(ple_w_gate[layer]), bf(ple_w_proj[layer]),
                add=add, g_final=g_final)
    if depth == 0:
        raise ValueError("depth must be positive")
    return h.reshape(bsz, seq, d)
```

```python
import functools

import jax
import jax.numpy as jnp
from jax import lax
from jax.experimental import pallas as pl
from jax.experimental.pallas import tpu as pltpu

F32 = jnp.float32
BF16 = jnp.bfloat16
EPS = 1e-6
CHUNK = 64
SUB = 8
HEAD_DIM = 128
CONV_WIDTH = 4
GDN_HEAD_GROUP = 8
FILL_AFTER_PREP = 4
FILL_AFTER_SCORES = 4
FILL_PER_LEVEL = 1
CONV_TAIL = 8
RGLRU_C = 8.0
SCAN_GROUP = 8
TOP_K = 2
MOE_TILE = 1024
MOE_BLOCK = 1024
MOE_GRANULE = 16
MOE_GATHER_ROWS = 128
MOE_COMBINE_ROWS = 256
MOE_SLACK = 256
VMEM_LIMIT = 56 * 1024 * 1024


def _cparams(sem, vmem=VMEM_LIMIT):
    return pltpu.CompilerParams(dimension_semantics=sem, vmem_limit_bytes=vmem)


def _dot(a, b):
    return jnp.dot(a, b, preferred_element_type=F32)


def _dot_nt(a, b):
    return lax.dot_general(a, b, (((1,), (1,)), ((), ())), preferred_element_type=F32)


def _dot_tn(a, b):
    return lax.dot_general(a, b, (((0,), (0,)), ((), ())), preferred_element_type=F32)


def _split(a):
    hi = a.astype(BF16)
    lo = (a - hi.astype(F32)).astype(BF16)
    return hi, lo


def _dot2(a, b):
    ah, al = _split(a)
    bh = b.astype(BF16)
    return _dot(jnp.concatenate([ah, al], axis=1), jnp.concatenate([bh, bh], axis=0))


def _rms(x, g):
    return x * lax.rsqrt(jnp.mean(x * x, axis=-1, keepdims=True) + EPS) * g


def _sigmoid(x):
    return 1.0 / (1.0 + jnp.exp(-x))


def _silu(x):
    return x * _sigmoid(x)


def _norm_proj_kernel(h_ref, g_ref, w_ref, ws_ref, o_ref, os_ref, un_ref):
    j = pl.program_id(1)

    @pl.when(j == 0)
    def _():
        un = _rms(h_ref[...], g_ref[...]).astype(BF16)
        un_ref[...] = un
        os_ref[...] = _dot(un, ws_ref[...])

    o_ref[...] = _dot(un_ref[...], w_ref[...])


def norm_proj(h, g, w, ws, tm=1024, tn=1024):
    t, d = h.shape
    tm = min(tm, t)
    n = w.shape[1]
    return pl.pallas_call(
        _norm_proj_kernel,
        grid=(t // tm, n // tn),
        in_specs=[pl.BlockSpec((tm, d), lambda i, j: (i, 0)),
                  pl.BlockSpec((1, d), lambda i, j: (0, 0)),
                  pl.BlockSpec((d, tn), lambda i, j: (0, j)),
                  pl.BlockSpec((d, ws.shape[1]), lambda i, j: (0, 0))],
        out_specs=[pl.BlockSpec((tm, tn), lambda i, j: (i, j)),
                   pl.BlockSpec((tm, ws.shape[1]), lambda i, j: (i, 0))],
        out_shape=[jax.ShapeDtypeStruct((t, n), F32), jax.ShapeDtypeStruct((t, ws.shape[1]), F32)],
        scratch_shapes=[pltpu.VMEM((tm, d), BF16)],
        compiler_params=_cparams(("parallel", "arbitrary")),
        name="norm_proj",
    )(h, g.reshape(1, d), w, ws)


def _norm_proj_gelu_kernel(h_ref, g_ref, wy_ref, wx_ref, o_ref, y_ref, un_ref):
    @pl.when(pl.program_id(1) == 0)
    def _():
        un_ref[...] = _rms(h_ref[...], g_ref[...]).astype(BF16)

    un = un_ref[...]
    y_ref[...] = jax.nn.gelu(_dot(un, wy_ref[...])).astype(BF16)
    o_ref[...] = _dot(un, wx_ref[...])


def norm_proj_gelu(h, g, w, tm=1024, tn=512):
    t, d = h.shape
    tm = min(tm, t)
    half = w.shape[1] // 2
    nh = half // tn
    return pl.pallas_call(
        _norm_proj_gelu_kernel,
        grid=(t // tm, nh),
        in_specs=[pl.BlockSpec((tm, d), lambda i, j: (i, 0)),
                  pl.BlockSpec((1, d), lambda i, j: (0, 0)),
                  pl.BlockSpec((d, tn), lambda i, j: (0, j)),
                  pl.BlockSpec((d, tn), lambda i, j: (0, nh + j))],
        out_specs=[pl.BlockSpec((tm, tn), lambda i, j: (i, j)),
                   pl.BlockSpec((tm, tn), lambda i, j: (i, j))],
        out_shape=[jax.ShapeDtypeStruct((t, half), F32), jax.ShapeDtypeStruct((t, half), BF16)],
        scratch_shapes=[pltpu.VMEM((tm, d), BF16)],
        compiler_params=_cparams(("parallel", "arbitrary")),
        name="norm_proj_gelu",
    )(h, g.reshape(1, d), w, w)


def _hgrn2_work(q_ref, f_ref, i_ref, g_ref, lbt_ref, gn_ref, o_ref, st_ref, *, layer, heads):
    c = CHUNK

    lbt = lbt_ref[...]
    e = jnp.exp(lbt - jnp.max(lbt, axis=0, keepdims=True))
    lb_all = jnp.sum(e[:layer + 1], axis=0, keepdims=True) / jnp.sum(e, axis=0, keepdims=True)

    row = lax.broadcasted_iota(jnp.int32, (c, c), 0)
    col = lax.broadcasted_iota(jnp.int32, (c, c), 1)
    tril = jnp.where(row >= col, 1.0, 0.0).astype(BF16)
    gn = gn_ref[...]

    levels = []
    ln = c // 2
    while ln >= SUB:
        levels += [(m * 2 * ln, m * 2 * ln + ln, ln) for m in range(c // (2 * ln))]
        ln //= 2
    n_pairs = sum(l[2] for l in levels)

    def seg_id(idx):
        sid = jnp.zeros_like(idx)
        start = 0
        for l in levels[:-1]:
            start += l[2]
            sid = sid + jnp.where(idx >= start, 1, 0)
        return sid

    same_seg = (seg_id(lax.broadcasted_iota(jnp.int32, (n_pairs, n_pairs), 0))
                == seg_id(lax.broadcasted_iota(jnp.int32, (n_pairs, n_pairs), 1)))
    sub_i = lax.broadcasted_iota(jnp.int32, (c // SUB, SUB, HEAD_DIM), 1)

    sls = [slice(h * HEAD_DIM, (h + 1) * HEAD_DIM) for h in range(heads)]

    def gates(h):
        lb = lb_all[:, sls[h]]
        q = q_ref[:, sls[h]] * (HEAD_DIM ** -0.5)
        forget = lb + (1.0 - lb) * _sigmoid(f_ref[:, sls[h]])
        lh, ll = _split(jnp.log(forget))
        b2 = _dot(tril, jnp.concatenate([lh, ll], axis=1))
        return q, 1.0 - forget, i_ref[:, sls[h]], b2[:, :HEAD_DIM] + b2[:, HEAD_DIM:]

    def block_pairs(h, q, k, v, b):
        o = _dot_nt((q * jnp.exp(b)).astype(BF16), st_ref[h].astype(BF16))
        qs, ks, vs = [], [], []
        for k0, q0, ln in levels:
            bref = b[q0 - 1:q0, :]
            qs.append(q[q0:q0 + ln] * jnp.exp(b[q0:q0 + ln] - bref))
            ks.append(k[k0:k0 + ln] * jnp.exp(bref - b[k0:k0 + ln]))
            vs.append(v[k0:k0 + ln])
        s = _dot_nt(jnp.concatenate(qs, axis=0).astype(BF16), jnp.concatenate(ks, axis=0).astype(BF16))
        r = _dot(jnp.where(same_seg, s, 0.0).astype(BF16), jnp.concatenate(vs, axis=0).astype(BF16))
        groups = [None] * (c // SUB)
        start = 0
        for k0, q0, ln in levels:
            for j in range(ln // SUB):
                piece = r[start + j * SUB:start + (j + 1) * SUB]
                gi = q0 // SUB + j
                groups[gi] = piece if groups[gi] is None else groups[gi] + piece
            start += ln
        groups[0] = jnp.zeros((SUB, HEAD_DIM), F32)
        return o + jnp.concatenate(groups, axis=0)

    def near_pairs(q, k, v, b):
        q3, k3, v3, b3 = (a.reshape(c // SUB, SUB, HEAD_DIM) for a in (q, k, v, b))
        o3 = jnp.sum(q3 * k3, axis=2, keepdims=True) * v3
        for d in range(1, SUB):
            dec = jnp.exp(jnp.where(sub_i >= d, b3 - pltpu.roll(b3, d, 1), -jnp.inf))
            w = jnp.sum(q3 * pltpu.roll(k3, d, 1) * dec, axis=2, keepdims=True)
            o3 = o3 + w * pltpu.roll(v3, d, 1)
        return o3.reshape(c, HEAD_DIM)

    qkvb, far = {}, {}

    def state_part(h):
        qkvb[h] = gates(h)
        q, k, v, b = qkvb[h]
        far[h] = block_pairs(h, q, k, v, b)
        blast = b[c - 1:c, :]
        kd = (k * jnp.exp(blast - b)).astype(BF16)
        st_ref[h] = st_ref[h] * jnp.exp(blast) + _dot_tn(v.astype(BF16), kd)

    def block_part(h):
        o = far[h] + near_pairs(*qkvb[h])
        on = o * lax.rsqrt(jnp.mean(o * o, axis=1, keepdims=True) + EPS) * gn
        o_ref[:, sls[h]] = (on * _silu(g_ref[:, sls[h]])).astype(BF16)

    return ([functools.partial(state_part, h) for h in range(heads)]
            + [functools.partial(block_part, h) for h in range(heads)])


def _causal_conv(ext_ref, x, w, first):
    n = x.shape[0]

    @pl.when(first)
    def _():
        ext_ref[0:CONV_TAIL, :] = jnp.zeros((CONV_TAIL, x.shape[1]), F32)

    ext_ref[CONV_TAIL:CONV_TAIL + n, :] = x
    y = x * w[CONV_WIDTH - 1:CONV_WIDTH, :]
    for k in range(CONV_WIDTH - 1):
        off = CONV_TAIL - (CONV_WIDTH - 1) + k
        y = y + ext_ref[off:off + n, :] * w[k:k + 1, :]
    ext_ref[0:CONV_TAIL, :] = ext_ref[n:n + CONV_TAIL, :]
    return y


def _mixer_ab_kernel(qa_ref, fa_ref, ia_ref, ga_ref, q_ref, k_ref, v_ref, z_ref, sm_ref, lbt_ref, gna_ref, cw_ref,
                     alog_ref, dtb_ref, gn_ref, o_ref, sta_ref, st_ref, eq_ref, ek_ref, ev_ref,
                     *, heads_a, heads, layer):
    c = CHUNK
    first = pl.program_id(1) == 0

    @pl.when(first)
    def _():
        sta_ref[...] = jnp.zeros_like(sta_ref)
        st_ref[...] = jnp.zeros_like(st_ref)

    pending = _hgrn2_work(qa_ref, fa_ref, ia_ref, ga_ref, lbt_ref, gna_ref, o_ref, sta_ref, layer=layer, heads=heads_a)

    def fill(n):
        for _ in range(min(n, len(pending))):
            pending.pop(0)()

    col0 = heads_a * HEAD_DIM

    hw = heads * HEAD_DIM
    cw = cw_ref[...]
    qc = _silu(_causal_conv(eq_ref, q_ref[...], cw[:, 0:hw], first))
    kc = _silu(_causal_conv(ek_ref, k_ref[...], cw[:, hw:2 * hw], first))
    vc = _silu(_causal_conv(ev_ref, v_ref[...], cw[:, 2 * hw:3 * hw], first))

    row = lax.broadcasted_iota(jnp.int32, (c, c), 0)
    col = lax.broadcasted_iota(jnp.int32, (c, c), 1)
    causal = row >= col
    strict = row > col
    eye = row == col
    eye_f = jnp.where(eye, 1.0, 0.0)
    diag_blk = (row // SUB) == (col // SUB)
    merge_masks = []
    s = SUB
    while s < c:
        merge_masks.append(jnp.logical_and((row // (2 * s)) == (col // (2 * s)), (row // s) == (col // s) + 1))
        s *= 2
    sm = sm_ref[...]
    gn = gn_ref[...]

    def to_row(colv):
        return jnp.sum(jnp.where(eye, colv, 0.0), axis=0, keepdims=True)

    def head_group(hg):
        hs = range(len(hg))
        sls = [slice(h * HEAD_DIM, (h + 1) * HEAD_DIM) for h in hg]
        qn, kn, knb, beta, gam_col, egam, decay, kbeta = [], [], [], [], [], [], [], []
        for i, h in enumerate(hg):
            qh, kh = qc[:, sls[i]], kc[:, sls[i]]
            qn.append(qh * lax.rsqrt(jnp.sum(qh * qh, axis=1, keepdims=True) + EPS) * (HEAD_DIM ** -0.5))
            kn.append(kh * lax.rsqrt(jnp.sum(kh * kh, axis=1, keepdims=True) + EPS))
            knb.append(kn[i].astype(BF16))
            beta.append(_sigmoid(sm[:, heads + h:heads + h + 1]))
            g_col = -jnp.exp(alog_ref[:, h:h + 1]) * jax.nn.softplus(sm[:, h:h + 1] + dtb_ref[:, h:h + 1])
            g_row = to_row(g_col)
            gam_col.append(jnp.sum(jnp.where(causal, g_row, 0.0), axis=1, keepdims=True))
            gam_row = jnp.sum(jnp.where(strict, 0.0, g_col), axis=0, keepdims=True)
            decay.append(jnp.exp(jnp.where(causal, gam_col[i] - gam_row, -jnp.inf)))
            egam.append(jnp.exp(gam_col[i]))
            kbeta.append(kn[i] * beta[i])

        fill(FILL_AFTER_PREP)
        a_mat = [jnp.where(strict, _dot_nt(kbeta[i].astype(BF16), knb[i]) * decay[i], 0.0) for i in hs]
        qk = [(_dot_nt(qn[i].astype(BF16), knb[i]) * decay[i]).astype(BF16) for i in hs]
        fill(FILL_AFTER_SCORES)
        d_mat = [jnp.where(diag_blk, a_mat[i], 0.0) for i in hs]
        x = [eye_f - d_mat[i] for i in hs]
        p = [_dot2(d_mat[i], d_mat[i]) for i in hs]
        fill(FILL_PER_LEVEL)
        n_sq = (SUB - 1).bit_length() - 1
        for lvl in range(n_sq):
            if lvl < n_sq - 1:
                y = [_dot2(jnp.concatenate([x[i], p[i]], axis=0), p[i]) for i in hs]
                fill(FILL_PER_LEVEL)
                x = [x[i] + y[i][:c] for i in hs]
                p = [y[i][c:] for i in hs]
            else:
                y = [_dot2(x[i], p[i]) for i in hs]
                fill(FILL_PER_LEVEL)
                x = [x[i] + y[i] for i in hs]
        for below in merge_masks:
            y = [_dot2(x[i], jnp.where(below, a_mat[i], 0.0)) for i in hs]
            fill(FILL_PER_LEVEL)
            y = [_dot2(y[i], x[i]) for i in hs]
            fill(FILL_PER_LEVEL)
            x = [x[i] - y[i] for i in hs]
        rhs = [jnp.concatenate([vc[:, sls[i]] * beta[i], kbeta[i] * egam[i]], axis=1).astype(BF16) for i in hs]
        uw = [_dot(x[i].astype(BF16), rhs[i]) for i in hs]

        st = [st_ref[h] for h in hg]
        stb = [s.astype(BF16) for s in st]
        v_new = [uw[i][:, :HEAD_DIM] - _dot_nt(uw[i][:, HEAD_DIM:].astype(BF16), stb[i]) for i in hs]
        vnb = [v.astype(BF16) for v in v_new]
        o = [_dot_nt((qn[i] * egam[i]).astype(BF16), stb[i]) + _dot(qk[i], vnb[i]) for i in hs]
        for i, h in enumerate(hg):
            glast = gam_col[i][c - 1:c, :]
            kd = (kn[i] * jnp.exp(glast - gam_col[i])).astype(BF16)
            st_ref[h] = st[i] * jnp.exp(glast) + _dot_tn(vnb[i], kd)
        for i in hs:
            on = o[i] * lax.rsqrt(jnp.mean(o[i] * o[i], axis=1, keepdims=True) + EPS) * gn
            o_ref[:, col0 + hg[i] * HEAD_DIM:col0 + (hg[i] + 1) * HEAD_DIM] = (on * _silu(z_ref[:, sls[i]])).astype(BF16)

    for h0 in range(0, heads, GDN_HEAD_GROUP):
        head_group(list(range(h0, min(h0 + GDN_HEAD_GROUP, heads))))
    fill(len(pending))


def mixer_ab(proj, small, lb_table, gnorm_a, conv_w, a_log, dt_bias, gnorm_b, *, bsz, seq, heads_a, heads_b, layer):
    t = proj.shape[0]
    wa, wb = heads_a * HEAD_DIM, heads_b * HEAD_DIM
    nc = seq // CHUNK
    row = lambda b, s: b * nc + s

    def spec_a(k):
        return pl.BlockSpec((CHUNK, wa), lambda b, s: (row(b, s), k))

    def spec_b(k):
        return pl.BlockSpec((CHUNK, wb), lambda b, s: (row(b, s), 4 * wa // wb + k))

    def const(shape):
        return pl.BlockSpec(shape, lambda b, s: (0, 0))

    return pl.pallas_call(
        functools.partial(_mixer_ab_kernel, heads_a=heads_a, heads=heads_b, layer=layer),
        grid=(bsz, nc),
        in_specs=[spec_a(0), spec_a(1), spec_a(2), spec_a(3), spec_b(0), spec_b(1), spec_b(2), spec_b(3),
                  pl.BlockSpec((CHUNK, small.shape[1]), lambda b, s: (row(b, s), 0)),
                  const(lb_table.shape), const((1, HEAD_DIM)),
                  const(conv_w.shape), const((1, heads_b)), const((1, heads_b)), const((1, HEAD_DIM))],
        out_specs=pl.BlockSpec((CHUNK, wa + wb), lambda b, s: (row(b, s), 0)),
        out_shape=jax.ShapeDtypeStruct((t, wa + wb), BF16),
        scratch_shapes=[pltpu.VMEM((heads_a, HEAD_DIM, HEAD_DIM), F32), pltpu.VMEM((heads_b, HEAD_DIM, HEAD_DIM), F32)]
        + [pltpu.VMEM((CHUNK + CONV_TAIL, wb), F32)] * 3,
        compiler_params=_cparams(("parallel", "arbitrary")),
        name="mixer_ab",
    )(proj, proj, proj, proj, proj, proj, proj, proj, small, lb_table, gnorm_a.reshape(1, HEAD_DIM), conv_w,
      a_log.reshape(1, heads_b), dt_bias.reshape(1, heads_b), gnorm_b.reshape(1, HEAD_DIM))


def _out_proj_kernel(a_ref, w_ref, h_ref, o_ref):
    o_ref[...] = h_ref[...] + _dot(a_ref[...], w_ref[...])


def out_proj(a, w, h, tm=1024, tn=1024):
    t, k = a.shape
    tm = min(tm, t)
    n = w.shape[1]
    return pl.pallas_call(
        _out_proj_kernel,
        grid=(t // tm, n // tn),
        in_specs=[pl.BlockSpec((tm, k), lambda i, j: (i, 0)),
                  pl.BlockSpec((k, tn), lambda i, j: (0, j)),
                  pl.BlockSpec((tm, tn), lambda i, j: (i, j))],
        out_specs=pl.BlockSpec((tm, tn), lambda i, j: (i, j)),
        out_shape=jax.ShapeDtypeStruct((t, n), F32),
        compiler_params=_cparams(("parallel", "arbitrary")),
        name="out_proj",
    )(a, w, h)


def _swiglu_kernel(h_ref, g_ref, wg_ref, wu_ref, wd_ref, o_ref, un_ref):
    f = pl.program_id(1)

    @pl.when(f == 0)
    def _():
        un_ref[...] = _rms(h_ref[...], g_ref[...]).astype(BF16)
        o_ref[...] = h_ref[...]

    un = un_ref[...]
    hb = (_silu(_dot(un, wg_ref[...])) * _dot(un, wu_ref[...])).astype(BF16)
    o_ref[...] += _dot(hb, wd_ref[...])


def swiglu(h, g, wg, wu, wd, tm=1024, tf=512):
    t, d = h.shape
    tm = min(tm, t)
    ff = wg.shape[1]
    return pl.pallas_call(
        _swiglu_kernel,
        grid=(t // tm, ff // tf),
        in_specs=[pl.BlockSpec((tm, d), lambda i, f: (i, 0), pipeline_mode=pl.Buffered(1)),
                  pl.BlockSpec((1, d), lambda i, f: (0, 0)),
                  pl.BlockSpec((d, tf), lambda i, f: (0, f)),
                  pl.BlockSpec((d, tf), lambda i, f: (0, f)),
                  pl.BlockSpec((tf, d), lambda i, f: (f, 0))],
        out_specs=pl.BlockSpec((tm, d), lambda i, f: (i, 0)),
        out_shape=jax.ShapeDtypeStruct((t, d), F32),
        scratch_shapes=[pltpu.VMEM((tm, d), BF16)],
        compiler_params=_cparams(("parallel", "arbitrary")),
        name="swiglu",
    )(h, g.reshape(1, d), wg, wu, wd)


def _ple_kernel(*refs, has_add, has_final):
    h_ref, p_ref, g_ref, wg_ref, wp_ref = refs[:5]
    k = 5
    add_ref = gf_ref = None
    if has_add:
        add_ref = refs[k]
        k += 1
    if has_final:
        gf_ref = refs[k]
        k += 1
    o_ref = refs[k]
    h = h_ref[...]
    if has_add:
        h = h + add_ref[...]
    un = _rms(h, g_ref[...]).astype(BF16)
    gate = _sigmoid(_dot(un, wg_ref[...]))
    out = h + gate * _dot(p_ref[...].astype(BF16), wp_ref[...])
    if has_final:
        out = _rms(out, gf_ref[...])
    o_ref[...] = out


def ple(h, p, g, wg, wp, add=None, g_final=None, tm=512):
    t, d = h.shape
    tm = min(tm, t)
    pd = p.shape[1]
    row = lambda i: (i, 0)
    const = lambda i: (0, 0)
    in_specs = [pl.BlockSpec((tm, d), row), pl.BlockSpec((tm, pd), row), pl.BlockSpec((1, d), const),
                pl.BlockSpec((d, d), const), pl.BlockSpec((pd, d), const)]
    args = [h, p, g.reshape(1, d), wg, wp]
    if add is not None:
        in_specs.append(pl.BlockSpec((tm, d), row))
        args.append(add)
    if g_final is not None:
        in_specs.append(pl.BlockSpec((1, d), const))
        args.append(g_final.reshape(1, d))
    return pl.pallas_call(
        functools.partial(_ple_kernel, has_add=add is not None, has_final=g_final is not None),
        grid=(t // tm,),
        in_specs=in_specs,
        out_specs=pl.BlockSpec((tm, d), row),
        out_shape=jax.ShapeDtypeStruct((t, d), F32),
        compiler_params=_cparams(("parallel",)),
        name="ple",
    )(*args)


def _rglru_kernel(x_ref, y_ref, cw_ref, cb_ref, wr_ref, br_ref, wi_ref, bi_ref, lam_ref, o_ref,
                  ext_ref, hc_ref, *, blocks):
    n = x_ref.shape[0]
    first = pl.program_id(1) == 0

    @pl.when(first)
    def _():
        hc_ref[...] = jnp.zeros_like(hc_ref)

    xc = _causal_conv(ext_ref, x_ref[...], cw_ref[...], first) + cb_ref[...]
    bw = xc.shape[1] // blocks
    rowi = lax.broadcasted_iota(jnp.int32, (n, bw), 0)
    at_start = jnp.logical_and(first, rowi == 0)
    gidx = lax.broadcasted_iota(jnp.int32, (n // SCAN_GROUP, SCAN_GROUP, bw), 1)

    for nb in range(blocks):
        sl = slice(nb * bw, (nb + 1) * bw)
        xb = xc[:, sl]
        xbb = xb.astype(BF16)
        r = _sigmoid(_dot(xbb, wr_ref[nb]) + br_ref[:, sl])
        gi = _sigmoid(_dot(xbb, wi_ref[nb]) + bi_ref[:, sl])
        log_a = -RGLRU_C * r * jax.nn.softplus(-lam_ref[:, sl])
        a = jnp.exp(log_a)
        m2 = 1.0 - a * a
        mult = jnp.where(m2 > 0.0, m2 * lax.rsqrt(m2), 0.0)
        mult = jnp.where(at_start, 1.0, mult)
        b = mult * gi * xb
        a = a.reshape(n // SCAN_GROUP, SCAN_GROUP, bw)
        b = b.reshape(n // SCAN_GROUP, SCAN_GROUP, bw)
        sh = 1
        while sh < SCAN_GROUP:
            ok = gidx >= sh
            a_prev = jnp.where(ok, pltpu.roll(a, sh, 1), 1.0)
            b_prev = jnp.where(ok, pltpu.roll(b, sh, 1), 0.0)
            b = b + a * b_prev
            a = a * a_prev
            sh *= 2
        carry = hc_ref[:, sl]
        groups = []
        for gi_ in range(n // SCAN_GROUP):
            hg = b[gi_] + a[gi_] * carry
            groups.append(hg)
            carry = hg[SCAN_GROUP - 1:SCAN_GROUP, :]
        hseq = jnp.concatenate(groups, axis=0)
        hc_ref[:, sl] = carry
        o_ref[:, sl] = (hseq * y_ref[:, sl].astype(F32)).astype(BF16)


def rglru(xr, y, conv_w, conv_b, w_r, b_r, w_i, b_i, lam, *, bsz, seq, rows=256):
    t, cwid = xr.shape
    blocks = w_r.shape[0]
    ns = seq // rows
    row = lambda b, s: (b * ns + s, 0)
    c2 = lambda b, s: (0, 0)
    c3 = lambda b, s: (0, 0, 0)
    vec = lambda a: a.reshape(1, cwid)
    return pl.pallas_call(
        functools.partial(_rglru_kernel, blocks=blocks),
        grid=(bsz, ns),
        in_specs=[pl.BlockSpec((rows, cwid), row), pl.BlockSpec((rows, cwid), row),
                  pl.BlockSpec(conv_w.shape, c2), pl.BlockSpec((1, cwid), c2),
                  pl.BlockSpec(w_r.shape, c3), pl.BlockSpec((1, cwid), c2),
                  pl.BlockSpec(w_i.shape, c3), pl.BlockSpec((1, cwid), c2),
                  pl.BlockSpec((1, cwid), c2)],
        out_specs=pl.BlockSpec((rows, cwid), row),
        out_shape=jax.ShapeDtypeStruct((t, cwid), BF16),
        scratch_shapes=[pltpu.VMEM((rows + CONV_TAIL, cwid), F32), pltpu.VMEM((1, cwid), F32)],
        compiler_params=_cparams(("parallel", "arbitrary")),
        name="rglru",
    )(xr, y, conv_w, vec(conv_b), w_r, vec(b_r), w_i, vec(b_i), vec(lam))


def _router_kernel(h_ref, g_ref, wr_ref, un_ref, pos_ref, gate_ref, cnt_ref):
    tm = h_ref.shape[0]
    ne = wr_ref.shape[0]
    un = _rms(h_ref[...], g_ref[...])
    uh, ul = _split(un)
    un_ref[...] = uh
    wh, wl = _split(wr_ref[...])
    logits = _dot_nt(wh, uh) + _dot_nt(wh, ul) + _dot_nt(wl, uh)
    eidx = lax.broadcasted_iota(jnp.int32, (ne, tm), 0).astype(F32)
    m1 = jnp.max(logits, axis=0, keepdims=True)
    i1 = jnp.min(jnp.where(logits == m1, eidx, float(ne)), axis=0, keepdims=True)
    mask1 = eidx == i1
    rest = jnp.where(mask1, -jnp.inf, logits)
    m2 = jnp.max(rest, axis=0, keepdims=True)
    i2 = jnp.min(jnp.where(rest == m2, eidx, float(ne)), axis=0, keepdims=True)
    mask2 = eidx == i2
    e2 = jnp.exp(m2 - m1)
    g1 = 1.0 / (1.0 + e2)
    g2 = e2 / (1.0 + e2)
    gate_ref[...] = jnp.where(mask1, g1, jnp.where(mask2, g2, 0.0))
    sel = jnp.logical_or(mask1, mask2)
    self32 = jnp.where(sel, 1.0, 0.0)
    before = lax.broadcasted_iota(jnp.int32, (tm, tm), 0) < lax.broadcasted_iota(jnp.int32, (tm, tm), 1)
    rank = _dot(self32.astype(BF16), jnp.where(before, 1.0, 0.0).astype(BF16))
    pos_ref[...] = jnp.where(sel, rank, -1.0)
    cnt = jnp.sum(self32, axis=1, keepdims=True).astype(jnp.int32)
    cnt_ref[0] = jnp.broadcast_to(cnt, cnt_ref.shape[1:])


def router(h, g, wr_t, tm):
    t, d = h.shape
    tm = min(tm, t)
    ne = wr_t.shape[0]
    nt = t // tm
    return pl.pallas_call(
        _router_kernel,
        grid=(nt,),
        in_specs=[pl.BlockSpec((tm, d), lambda i: (i, 0)),
                  pl.BlockSpec((1, d), lambda i: (0, 0)),
                  pl.BlockSpec((ne, d), lambda i: (0, 0))],
        out_specs=[pl.BlockSpec((tm, d), lambda i: (i, 0)),
                   pl.BlockSpec((ne, tm), lambda i: (0, i)),
                   pl.BlockSpec((ne, tm), lambda i: (0, i)),
                   pl.BlockSpec((1, ne, 128), lambda i: (i, 0, 0))],
        out_shape=[jax.ShapeDtypeStruct((t, d), BF16), jax.ShapeDtypeStruct((ne, t), F32),
                   jax.ShapeDtypeStruct((ne, t), F32), jax.ShapeDtypeStruct((nt, ne, 128), jnp.int32)],
        compiler_params=_cparams(("parallel",)),
        name="moe_router",
    )(h, g.reshape(1, d), wr_t)


def _pick(pos, base, rows):
    slot = lax.broadcasted_iota(jnp.int32, (rows, pos.shape[1]), 0).astype(F32)
    return pos == slot + base.astype(F32)


def _moe_gather_kernel(ce_ref, cb_ref, co_ref, nq_ref, un_ref, pos_ref, xs_in_ref, xs_ref, buf_ref, sem, *, rows, qmax):
    del xs_in_ref
    i = pl.program_id(0)
    n = nq_ref[i]

    def copy(slot, off):
        return pltpu.make_async_copy(buf_ref.at[slot], xs_ref.at[pl.ds(off, rows)], sem.at[slot])

    def body(q, carry):
        k = i * qmax + q
        slot = lax.rem(q, 2)

        @pl.when(q >= 2)
        def _():
            copy(slot, 0).wait()

        pos = pos_ref[pl.ds(ce_ref[k], 1), :]
        sel = jnp.where(_pick(pos, cb_ref[k], rows), 1.0, 0.0).astype(BF16)
        buf_ref[slot] = _dot(sel, un_ref[...]).astype(BF16)
        copy(slot, pl.multiple_of(co_ref[k], MOE_GRANULE)).start()
        return carry

    lax.fori_loop(0, n, body, 0)

    @pl.when(n >= 2)
    def _():
        copy(lax.rem(n, 2), 0).wait()

    @pl.when(n >= 1)
    def _():
        copy(lax.rem(n + 1, 2), 0).wait()


def moe_gather(un, pos, tables, n_rows, tm, rows):
    t, d = un.shape
    ne = pos.shape[0]
    ce, cb, co, nq, qmax = tables
    grid_spec = pltpu.PrefetchScalarGridSpec(
        num_scalar_prefetch=4,
        grid=(t // tm,),
        in_specs=[pl.BlockSpec((tm, d), lambda i, *_: (i, 0)),
                  pl.BlockSpec((ne, tm), lambda i, *_: (0, i)),
                  pl.BlockSpec(memory_space=pl.ANY)],
        out_specs=pl.BlockSpec(memory_space=pl.ANY),
        scratch_shapes=[pltpu.VMEM((2, rows, d), BF16), pltpu.SemaphoreType.DMA((2,))],
    )
    return pl.pallas_call(
        functools.partial(_moe_gather_kernel, rows=rows, qmax=qmax),
        grid_spec=grid_spec,
        out_shape=jax.ShapeDtypeStruct((n_rows, d), BF16),
        input_output_aliases={6: 0},
        compiler_params=_cparams(("arbitrary",)),
        name="moe_gather",
    )(ce, cb, co, nq, un, pos, jnp.zeros((n_rows, d), BF16))


def _moe_ffn_kernel(be_ref, nu_ref, x_ref, wg_ref, wu_ref, wd_ref, o_ref, acc_ref):
    del be_ref
    b, f = pl.program_id(0), pl.program_id(1)
    nf = pl.num_programs(1)
    used = b < nu_ref[0]

    @pl.when(f == 0)
    def _():
        acc_ref[...] = jnp.zeros_like(acc_ref)

    @pl.when(used)
    def _():
        x = x_ref[...]
        hb = (_silu(_dot(x, wg_ref[0])) * _dot(x, wu_ref[0])).astype(BF16)
        acc_ref[...] += _dot(hb, wd_ref[0])

    @pl.when(f == nf - 1)
    def _():
        o_ref[...] = acc_ref[...].astype(BF16)


def moe_ffn(xs, blk_e, n_used, wg, wu, wd, bm, tf=512):
    n_rows, d = xs.shape
    ff = wg.shape[2]
    nf = ff // tf

    def fidx(b, f, nu):
        return jnp.where(b < nu[0], f, nf - 1)

    grid_spec = pltpu.PrefetchScalarGridSpec(
        num_scalar_prefetch=2,
        grid=(n_rows // bm, nf),
        in_specs=[pl.BlockSpec((bm, d), lambda b, f, be, nu: (b, 0)),
                  pl.BlockSpec((1, d, tf), lambda b, f, be, nu: (be[b], 0, fidx(b, f, nu))),
                  pl.BlockSpec((1, d, tf), lambda b, f, be, nu: (be[b], 0, fidx(b, f, nu))),
                  pl.BlockSpec((1, tf, d), lambda b, f, be, nu: (be[b], fidx(b, f, nu), 0))],
        out_specs=pl.BlockSpec((bm, d), lambda b, f, be, nu: (b, 0)),
        scratch_shapes=[pltpu.VMEM((bm, d), F32)],
    )
    return pl.pallas_call(
        _moe_ffn_kernel,
        grid_spec=grid_spec,
        out_shape=jax.ShapeDtypeStruct((n_rows, d), BF16),
        compiler_params=_cparams(("parallel", "arbitrary")),
        name="moe_ffn",
    )(blk_e, n_used, xs, wg, wu, wd)


def _moe_combine_kernel(ce_ref, cb_ref, co_ref, nq_ref, pos_ref, gate_ref, y_ref, o_ref, buf_ref, sem, *, rows, qmax):
    i = pl.program_id(0)
    n = nq_ref[i]
    o_ref[...] = jnp.zeros_like(o_ref)

    def copy(slot, off):
        return pltpu.make_async_copy(y_ref.at[pl.ds(off, rows)], buf_ref.at[slot], sem.at[slot])

    def start(q):
        copy(lax.rem(q, 2), pl.multiple_of(co_ref[i * qmax + q], MOE_GRANULE)).start()

    @pl.when(n > 0)
    def _():
        start(0)

    def body(q, carry):
        k = i * qmax + q
        slot = lax.rem(q, 2)

        @pl.when(q + 1 < n)
        def _():
            start(q + 1)

        copy(slot, 0).wait()
        e = ce_ref[k]
        hit = _pick(pos_ref[pl.ds(e, 1), :], cb_ref[k], rows)
        gsub = jnp.sum(jnp.where(hit, gate_ref[pl.ds(e, 1), :], 0.0), axis=1, keepdims=True)
        yb = (buf_ref[slot].astype(F32) * gsub).astype(BF16)
        o_ref[...] += _dot_tn(jnp.where(hit, 1.0, 0.0).astype(BF16), yb)
        return carry

    lax.fori_loop(0, n, body, 0)


def moe_combine(ys, pos, gate, tables, tm, rows):
    ne, t = pos.shape
    d = ys.shape[1]
    ce, cb, co, nq, qmax = tables
    grid_spec = pltpu.PrefetchScalarGridSpec(
        num_scalar_prefetch=4,
        grid=(t // tm,),
        in_specs=[pl.BlockSpec((ne, tm), lambda i, *_: (0, i)),
                  pl.BlockSpec((ne, tm), lambda i, *_: (0, i)),
                  pl.BlockSpec(memory_space=pl.ANY)],
        out_specs=pl.BlockSpec((tm, d), lambda i, *_: (i, 0)),
        scratch_shapes=[pltpu.VMEM((2, rows, d), BF16), pltpu.SemaphoreType.DMA((2,))],
    )
    return pl.pallas_call(
        functools.partial(_moe_combine_kernel, rows=rows, qmax=qmax),
        grid_spec=grid_spec,
        out_shape=jax.ShapeDtypeStruct((t, d), F32),
        compiler_params=_cparams(("arbitrary",)),
        name="moe_combine",
    )(ce, cb, co, nq, pos, gate, ys)


def _chunk_tables(counts, seg, rows, qmax):
    ne = counts.shape[1]
    ns = (counts + rows - 1) // rows
    cs = jnp.cumsum(ns, axis=1)
    q = jnp.arange(qmax, dtype=jnp.int32)
    ce = jnp.minimum(jnp.sum(q[None, :, None] >= cs[:, None, :], axis=-1), ne - 1).astype(jnp.int32)
    cj = q[None, :] - jnp.take_along_axis(cs - ns, ce, axis=1)
    co = jnp.take_along_axis(seg, ce, axis=1) + cj * rows
    flat = lambda a: a.reshape(-1).astype(jnp.int32)
    return flat(ce), flat(cj * rows), flat(co), cs[:, -1].astype(jnp.int32), qmax


def moe(h, g, w_router, wg, wu, wd, tm=MOE_TILE, bm=MOE_BLOCK):
    t = h.shape[0]
    tm = min(tm, t)
    ne = wg.shape[0]
    nt = t // tm
    un, pos, gate, cnt = router(h, g, w_router.T, tm)
    counts = cnt[:, :, 0]
    padded = (counts + MOE_GRANULE - 1) // MOE_GRANULE * MOE_GRANULE
    tot = jnp.sum(padded, axis=0)
    ptot = (tot + MOE_SLACK + bm - 1) // bm * bm
    eend = jnp.cumsum(ptot)
    seg = (eend - ptot)[None, :] + jnp.cumsum(padded, axis=0) - padded
    n_blocks = (TOP_K * t + nt * ne * (MOE_GRANULE - 1) + ne * MOE_SLACK) // bm + ne
    n_used = (eend[-1:] // bm).astype(jnp.int32)
    blk_e = jnp.minimum(jnp.searchsorted(eend, jnp.arange(n_blocks, dtype=jnp.int32) * bm, side="right"),
                        ne - 1).astype(jnp.int32)
    g_tab = _chunk_tables(counts, seg, MOE_GATHER_ROWS, TOP_K * tm // MOE_GATHER_ROWS + ne)
    c_tab = _chunk_tables(counts, seg, MOE_COMBINE_ROWS, TOP_K * tm // MOE_COMBINE_ROWS + ne)
    xs = moe_gather(un, pos, g_tab, n_blocks * bm, tm, MOE_GATHER_ROWS)
    ys = moe_ffn(xs, blk_e, n_used, wg, wu, wd, bm)
    return moe_combine(ys, pos, gate, c_tab, tm, MOE_COMBINE_ROWS)


def kernel(x, p, ln_mix, ln_ffn, ln_ple, ln_final, lb_table, ab_w_in, ab_conv, b_a_log, b_dt_bias, a_gnorm, b_gnorm, ab_w_out, c_w_in, c_conv_w, c_conv_b, c_w_r, c_b_r, c_w_i, c_b_i, c_lambda, c_w_out, ffn_w_gate, ffn_w_up, ffn_w_down, moe_router, moe_w_gate, moe_w_up, moe_w_down, ple_w_proj, ple_w_gate):
    bsz, seq, d = x.shape
    t = bsz * seq
    depth = ln_mix.shape[0]
    a_heads = lb_table.shape[1] // HEAD_DIM
    b_heads = b_a_log.shape[1]
    a_w = a_heads * HEAD_DIM
    b_w = b_heads * HEAD_DIM
    main_w = 4 * a_w + 4 * b_w
    bf = lambda a: a.astype(BF16)

    h = x.reshape(t, d)
    for layer in range(depth):
        j = layer // 2
        if layer % 2 == 0:
            w_in = ab_w_in[j]
            w_small = jnp.pad(w_in[:, main_w:], ((0, 0), (0, 128 - 2 * b_heads)))
            proj, small = norm_proj(h, ln_mix[layer], bf(w_in[:, :main_w]), bf(w_small))
            mixed = mixer_ab(proj, small, lb_table, a_gnorm[j], ab_conv[j], b_a_log[j], b_dt_bias[j], b_gnorm[j],
                             bsz=bsz, seq=seq, heads_a=a_heads, heads_b=b_heads, layer=layer)
            h = out_proj(mixed, bf(ab_w_out[j]), h)
            h = swiglu(h, ln_ffn[layer], bf(ffn_w_gate[j]), bf(ffn_w_up[j]), bf(ffn_w_down[j]))
            add = None
        else:
            xr, yb = norm_proj_gelu(h, ln_mix[layer], bf(c_w_in[j]))
            hy = rglru(xr, yb, c_conv_w[j], c_conv_b[j], bf(c_w_r[j]), c_b_r[j], bf(c_w_i[j]), c_b_i[j],
                       c_lambda[j], bsz=bsz, seq=seq)
            h = out_proj(hy, bf(c_w_out[j]), h)
            add = moe(h, ln_ffn[layer], moe_router[j], bf(moe_w_gate[j]), bf(moe_w_up[j]), bf(moe_w_down[j]))
        g_final = ln_final if layer == depth - 1 else None
        h = ple(h, p[layer].reshape(t, -1), ln_ple[layer], bf(ple_w_gate[layer]), bf(ple_w_proj[layer]),
                add=add, g_final=g_final)
    if depth == 0:
        raise ValueError("depth must be positive")
    return h.reshape(bsz, seq, d)
```

```python
import functools

import jax
import jax.numpy as jnp
from jax import lax
from jax.experimental import pallas as pl
from jax.experimental.pallas import tpu as pltpu

F32 = jnp.float32
BF16 = jnp.bfloat16
EPS = 1e-6
CHUNK = 64
SUB = 8
HEAD_DIM = 128
CONV_WIDTH = 4
GDN_HEAD_GROUP = 8
MIXER_ROWS = 128
FILL_AFTER_PREP = 4
FILL_AFTER_SCORES = 4
FILL_PER_LEVEL = 1
CONV_TAIL = 8
RGLRU_C = 8.0
SCAN_GROUP = 8
TOP_K = 2
MOE_TILE = 1024
MOE_BLOCK = 1024
MOE_GRANULE = 16
MOE_GATHER_ROWS = 128
MOE_COMBINE_ROWS = 256
MOE_SLACK = 256
VMEM_LIMIT = 56 * 1024 * 1024


def _cparams(sem, vmem=VMEM_LIMIT):
    return pltpu.CompilerParams(dimension_semantics=sem, vmem_limit_bytes=vmem)


def _dot(a, b):
    return jnp.dot(a, b, preferred_element_type=F32)


def _dot_nt(a, b):
    return lax.dot_general(a, b, (((1,), (1,)), ((), ())), preferred_element_type=F32)


def _dot_tn(a, b):
    return lax.dot_general(a, b, (((0,), (0,)), ((), ())), preferred_element_type=F32)


def _split(a):
    hi = a.astype(BF16)
    lo = (a - hi.astype(F32)).astype(BF16)
    return hi, lo


def _dot2(a, b):
    ah, al = _split(a)
    bh = b.astype(BF16)
    return _dot(jnp.concatenate([ah, al], axis=1), jnp.concatenate([bh, bh], axis=0))


def _rms(x, g):
    return x * lax.rsqrt(jnp.mean(x * x, axis=-1, keepdims=True) + EPS) * g


def _sigmoid(x):
    return 1.0 / (1.0 + jnp.exp(-x))


def _silu(x):
    return x * _sigmoid(x)


def _norm_proj_kernel(h_ref, g_ref, w_ref, ws_ref, o_ref, os_ref, un_ref):
    j = pl.program_id(1)

    @pl.when(j == 0)
    def _():
        un = _rms(h_ref[...], g_ref[...]).astype(BF16)
        un_ref[...] = un
        os_ref[...] = _dot(un, ws_ref[...])

    o_ref[...] = _dot(un_ref[...], w_ref[...])


def norm_proj(h, g, w, ws, tm=1024, tn=1024):
    t, d = h.shape
    tm = min(tm, t)
    n = w.shape[1]
    return pl.pallas_call(
        _norm_proj_kernel,
        grid=(t // tm, n // tn),
        in_specs=[pl.BlockSpec((tm, d), lambda i, j: (i, 0)),
                  pl.BlockSpec((1, d), lambda i, j: (0, 0)),
                  pl.BlockSpec((d, tn), lambda i, j: (0, j)),
                  pl.BlockSpec((d, ws.shape[1]), lambda i, j: (0, 0))],
        out_specs=[pl.BlockSpec((tm, tn), lambda i, j: (i, j)),
                   pl.BlockSpec((tm, ws.shape[1]), lambda i, j: (i, 0))],
        out_shape=[jax.ShapeDtypeStruct((t, n), F32), jax.ShapeDtypeStruct((t, ws.shape[1]), F32)],
        scratch_shapes=[pltpu.VMEM((tm, d), BF16)],
        compiler_params=_cparams(("parallel", "arbitrary")),
        name="norm_proj",
    )(h, g.reshape(1, d), w, ws)


def _norm_proj_gelu_kernel(h_ref, g_ref, wy_ref, wx_ref, o_ref, y_ref, un_ref):
    @pl.when(pl.program_id(1) == 0)
    def _():
        un_ref[...] = _rms(h_ref[...], g_ref[...]).astype(BF16)

    un = un_ref[...]
    y_ref[...] = jax.nn.gelu(_dot(un, wy_ref[...])).astype(BF16)
    o_ref[...] = _dot(un, wx_ref[...])


def norm_proj_gelu(h, g, w, tm=1024, tn=512):
    t, d = h.shape
    tm = min(tm, t)
    half = w.shape[1] // 2
    nh = half // tn
    return pl.pallas_call(
        _norm_proj_gelu_kernel,
        grid=(t // tm, nh),
        in_specs=[pl.BlockSpec((tm, d), lambda i, j: (i, 0)),
                  pl.BlockSpec((1, d), lambda i, j: (0, 0)),
                  pl.BlockSpec((d, tn), lambda i, j: (0, j)),
                  pl.BlockSpec((d, tn), lambda i, j: (0, nh + j))],
        out_specs=[pl.BlockSpec((tm, tn), lambda i, j: (i, j)),
                   pl.BlockSpec((tm, tn), lambda i, j: (i, j))],
        out_shape=[jax.ShapeDtypeStruct((t, half), F32), jax.ShapeDtypeStruct((t, half), BF16)],
        scratch_shapes=[pltpu.VMEM((tm, d), BF16)],
        compiler_params=_cparams(("parallel", "arbitrary")),
        name="norm_proj_gelu",
    )(h, g.reshape(1, d), w, w)


def _hgrn2_work(q_ref, f_ref, i_ref, g_ref, lbt_ref, gn_ref, o_ref, st_ref, *, layer, heads):
    c = CHUNK

    lbt = lbt_ref[...]
    e = jnp.exp(lbt - jnp.max(lbt, axis=0, keepdims=True))
    lb_all = jnp.sum(e[:layer + 1], axis=0, keepdims=True) / jnp.sum(e, axis=0, keepdims=True)

    row = lax.broadcasted_iota(jnp.int32, (c, c), 0)
    col = lax.broadcasted_iota(jnp.int32, (c, c), 1)
    tril = jnp.where(row >= col, 1.0, 0.0).astype(BF16)
    gn = gn_ref[...]

    levels = []
    ln = c // 2
    while ln >= SUB:
        levels += [(m * 2 * ln, m * 2 * ln + ln, ln) for m in range(c // (2 * ln))]
        ln //= 2
    n_pairs = sum(l[2] for l in levels)

    def seg_id(idx):
        sid = jnp.zeros_like(idx)
        start = 0
        for l in levels[:-1]:
            start += l[2]
            sid = sid + jnp.where(idx >= start, 1, 0)
        return sid

    same_seg = (seg_id(lax.broadcasted_iota(jnp.int32, (n_pairs, n_pairs), 0))
                == seg_id(lax.broadcasted_iota(jnp.int32, (n_pairs, n_pairs), 1)))
    sub_i = lax.broadcasted_iota(jnp.int32, (c // SUB, SUB, HEAD_DIM), 1)

    sls = [slice(h * HEAD_DIM, (h + 1) * HEAD_DIM) for h in range(heads)]

    def gates(h):
        lb = lb_all[:, sls[h]]
        q = q_ref[:, sls[h]] * (HEAD_DIM ** -0.5)
        forget = lb + (1.0 - lb) * _sigmoid(f_ref[:, sls[h]])
        lh, ll = _split(jnp.log(forget))
        b2 = _dot(tril, jnp.concatenate([lh, ll], axis=1))
        return q, 1.0 - forget, i_ref[:, sls[h]], b2[:, :HEAD_DIM] + b2[:, HEAD_DIM:]

    def block_pairs(h, q, k, v, b):
        o = _dot_nt((q * jnp.exp(b)).astype(BF16), st_ref[h].astype(BF16))
        qs, ks, vs = [], [], []
        for k0, q0, ln in levels:
            bref = b[q0 - 1:q0, :]
            qs.append(q[q0:q0 + ln] * jnp.exp(b[q0:q0 + ln] - bref))
            ks.append(k[k0:k0 + ln] * jnp.exp(bref - b[k0:k0 + ln]))
            vs.append(v[k0:k0 + ln])
        s = _dot_nt(jnp.concatenate(qs, axis=0).astype(BF16), jnp.concatenate(ks, axis=0).astype(BF16))
        r = _dot(jnp.where(same_seg, s, 0.0).astype(BF16), jnp.concatenate(vs, axis=0).astype(BF16))
        groups = [None] * (c // SUB)
        start = 0
        for k0, q0, ln in levels:
            for j in range(ln // SUB):
                piece = r[start + j * SUB:start + (j + 1) * SUB]
                gi = q0 // SUB + j
                groups[gi] = piece if groups[gi] is None else groups[gi] + piece
            start += ln
        groups[0] = jnp.zeros((SUB, HEAD_DIM), F32)
        return o + jnp.concatenate(groups, axis=0)

    def near_pairs(q, k, v, b):
        q3, k3, v3, b3 = (a.reshape(c // SUB, SUB, HEAD_DIM) for a in (q, k, v, b))
        o3 = jnp.sum(q3 * k3, axis=2, keepdims=True) * v3
        for d in range(1, SUB):
            dec = jnp.exp(jnp.where(sub_i >= d, b3 - pltpu.roll(b3, d, 1), -jnp.inf))
            w = jnp.sum(q3 * pltpu.roll(k3, d, 1) * dec, axis=2, keepdims=True)
            o3 = o3 + w * pltpu.roll(v3, d, 1)
        return o3.reshape(c, HEAD_DIM)

    qkvb, far = {}, {}

    def state_part(h):
        qkvb[h] = gates(h)
        q, k, v, b = qkvb[h]
        far[h] = block_pairs(h, q, k, v, b)
        blast = b[c - 1:c, :]
        kd = (k * jnp.exp(blast - b)).astype(BF16)
        st_ref[h] = st_ref[h] * jnp.exp(blast) + _dot_tn(v.astype(BF16), kd)

    def block_part(h):
        o = far[h] + near_pairs(*qkvb[h])
        on = o * lax.rsqrt(jnp.mean(o * o, axis=1, keepdims=True) + EPS) * gn
        o_ref[:, sls[h]] = (on * _silu(g_ref[:, sls[h]])).astype(BF16)

    return ([functools.partial(state_part, h) for h in range(heads)]
            + [functools.partial(block_part, h) for h in range(heads)])


def _causal_conv(ext_ref, x, w, first):
    n = x.shape[0]

    @pl.when(first)
    def _():
        ext_ref[0:CONV_TAIL, :] = jnp.zeros((CONV_TAIL, x.shape[1]), F32)

    ext_ref[CONV_TAIL:CONV_TAIL + n, :] = x
    y = x * w[CONV_WIDTH - 1:CONV_WIDTH, :]
    for k in range(CONV_WIDTH - 1):
        off = CONV_TAIL - (CONV_WIDTH - 1) + k
        y = y + ext_ref[off:off + n, :] * w[k:k + 1, :]
    ext_ref[0:CONV_TAIL, :] = ext_ref[n:n + CONV_TAIL, :]
    return y


def _mixer_ab_kernel(qa_ref, fa_ref, ia_ref, ga_ref, q_ref, k_ref, v_ref, z_ref, sm_ref, lbt_ref, gna_ref, cw_ref,
                     alog_ref, dtb_ref, gn_ref, o_ref, sta_ref, st_ref, eq_ref, ek_ref, ev_ref,
                     *, heads_a, heads, layer):
    c = CHUNK
    first = pl.program_id(1) == 0

    @pl.when(first)
    def _():
        sta_ref[...] = jnp.zeros_like(sta_ref)
        st_ref[...] = jnp.zeros_like(st_ref)

    col0 = heads_a * HEAD_DIM
    hw = heads * HEAD_DIM
    cw = cw_ref[...]
    qc_all = _silu(_causal_conv(eq_ref, q_ref[...], cw[:, 0:hw], first))
    kc_all = _silu(_causal_conv(ek_ref, k_ref[...], cw[:, hw:2 * hw], first))
    vc_all = _silu(_causal_conv(ev_ref, v_ref[...], cw[:, 2 * hw:3 * hw], first))

    row = lax.broadcasted_iota(jnp.int32, (c, c), 0)
    col = lax.broadcasted_iota(jnp.int32, (c, c), 1)
    causal = row >= col
    strict = row > col
    eye = row == col
    eye_f = jnp.where(eye, 1.0, 0.0)
    diag_blk = (row // SUB) == (col // SUB)
    merge_masks = []
    s = SUB
    while s < c:
        merge_masks.append(jnp.logical_and((row // (2 * s)) == (col // (2 * s)), (row // s) == (col // s) + 1))
        s *= 2
    gn = gn_ref[...]

    def to_row(colv):
        return jnp.sum(jnp.where(eye, colv, 0.0), axis=0, keepdims=True)

    def head_group(hg, qc, kc, vc, sm, z_ref, o_ref, fill):
        hs = range(len(hg))
        sls = [slice(h * HEAD_DIM, (h + 1) * HEAD_DIM) for h in hg]
        qn, kn, knb, beta, gam_col, egam, decay, kbeta = [], [], [], [], [], [], [], []
        for i, h in enumerate(hg):
            qh, kh = qc[:, sls[i]], kc[:, sls[i]]
            qn.append(qh * lax.rsqrt(jnp.sum(qh * qh, axis=1, keepdims=True) + EPS) * (HEAD_DIM ** -0.5))
            kn.append(kh * lax.rsqrt(jnp.sum(kh * kh, axis=1, keepdims=True) + EPS))
            knb.append(kn[i].astype(BF16))
            beta.append(_sigmoid(sm[:, heads + h:heads + h + 1]))
            g_col = -jnp.exp(alog_ref[:, h:h + 1]) * jax.nn.softplus(sm[:, h:h + 1] + dtb_ref[:, h:h + 1])
            g_row = to_row(g_col)
            gam_col.append(jnp.sum(jnp.where(causal, g_row, 0.0), axis=1, keepdims=True))
            gam_row = jnp.sum(jnp.where(strict, 0.0, g_col), axis=0, keepdims=True)
            decay.append(jnp.exp(jnp.where(causal, gam_col[i] - gam_row, -jnp.inf)))
            egam.append(jnp.exp(gam_col[i]))
            kbeta.append(kn[i] * beta[i])

        fill(FILL_AFTER_PREP)
        a_mat = [jnp.where(strict, _dot_nt(kbeta[i].astype(BF16), knb[i]) * decay[i], 0.0) for i in hs]
        qk = [(_dot_nt(qn[i].astype(BF16), knb[i]) * decay[i]).astype(BF16) for i in hs]
        fill(FILL_AFTER_SCORES)
        d_mat = [jnp.where(diag_blk, a_mat[i], 0.0) for i in hs]
        x = [eye_f - d_mat[i] for i in hs]
        p = [_dot2(d_mat[i], d_mat[i]) for i in hs]
        fill(FILL_PER_LEVEL)
        n_sq = (SUB - 1).bit_length() - 1
        for lvl in range(n_sq):
            if lvl < n_sq - 1:
                y = [_dot2(jnp.concatenate([x[i], p[i]], axis=0), p[i]) for i in hs]
                fill(FILL_PER_LEVEL)
                x = [x[i] + y[i][:c] for i in hs]
                p = [y[i][c:] for i in hs]
            else:
                y = [_dot2(x[i], p[i]) for i in hs]
                fill(FILL_PER_LEVEL)
                x = [x[i] + y[i] for i in hs]
        for below in merge_masks:
            y = [_dot2(x[i], jnp.where(below, a_mat[i], 0.0)) for i in hs]
            fill(FILL_PER_LEVEL)
            y = [_dot2(y[i], x[i]) for i in hs]
            fill(FILL_PER_LEVEL)
            x = [x[i] - y[i] for i in hs]
        rhs = [jnp.concatenate([vc[:, sls[i]] * beta[i], kbeta[i] * egam[i]], axis=1).astype(BF16) for i in hs]
        uw = [_dot(x[i].astype(BF16), rhs[i]) for i in hs]

        st = [st_ref[h] for h in hg]
        stb = [s.astype(BF16) for s in st]
        v_new = [uw[i][:, :HEAD_DIM] - _dot_nt(uw[i][:, HEAD_DIM:].astype(BF16), stb[i]) for i in hs]
        vnb = [v.astype(BF16) for v in v_new]
        o = [_dot_nt((qn[i] * egam[i]).astype(BF16), stb[i]) + _dot(qk[i], vnb[i]) for i in hs]
        for i, h in enumerate(hg):
            glast = gam_col[i][c - 1:c, :]
            kd = (kn[i] * jnp.exp(glast - gam_col[i])).astype(BF16)
            st_ref[h] = st[i] * jnp.exp(glast) + _dot_tn(vnb[i], kd)
        for i in hs:
            on = o[i] * lax.rsqrt(jnp.mean(o[i] * o[i], axis=1, keepdims=True) + EPS) * gn
            o_ref[:, col0 + hg[i] * HEAD_DIM:col0 + (hg[i] + 1) * HEAD_DIM] = (on * _silu(z_ref[:, sls[i]])).astype(BF16)

    for ci in range(q_ref.shape[0] // c):
        rs = pl.ds(ci * c, c)
        pending = _hgrn2_work(qa_ref.at[rs], fa_ref.at[rs], ia_ref.at[rs], ga_ref.at[rs], lbt_ref, gna_ref,
                              o_ref.at[rs], sta_ref, layer=layer, heads=heads_a)

        def fill(n, pending=pending):
            for _ in range(min(n, len(pending))):
                pending.pop(0)()

        r0 = ci * c
        for h0 in range(0, heads, GDN_HEAD_GROUP):
            head_group(list(range(h0, min(h0 + GDN_HEAD_GROUP, heads))), qc_all[r0:r0 + c], kc_all[r0:r0 + c],
                       vc_all[r0:r0 + c], sm_ref[rs, :], z_ref.at[rs], o_ref.at[rs], fill)
        fill(len(pending))


def mixer_ab(proj, small, lb_table, gnorm_a, conv_w, a_log, dt_bias, gnorm_b, *, bsz, seq, heads_a, heads_b, layer):
    t = proj.shape[0]
    wa, wb = heads_a * HEAD_DIM, heads_b * HEAD_DIM
    rows = min(MIXER_ROWS, seq)
    nc = seq // rows
    row = lambda b, s: b * nc + s

    def spec_a(k):
        return pl.BlockSpec((rows, wa), lambda b, s: (row(b, s), k))

    def spec_b(k):
        return pl.BlockSpec((rows, wb), lambda b, s: (row(b, s), 4 * wa // wb + k))

    def const(shape):
        return pl.BlockSpec(shape, lambda b, s: (0, 0))

    return pl.pallas_call(
        functools.partial(_mixer_ab_kernel, heads_a=heads_a, heads=heads_b, layer=layer),
        grid=(bsz, nc),
        in_specs=[spec_a(0), spec_a(1), spec_a(2), spec_a(3), spec_b(0), spec_b(1), spec_b(2), spec_b(3),
                  pl.BlockSpec((rows, small.shape[1]), lambda b, s: (row(b, s), 0)),
                  const(lb_table.shape), const((1, HEAD_DIM)),
                  const(conv_w.shape), const((1, heads_b)), const((1, heads_b)), const((1, HEAD_DIM))],
        out_specs=pl.BlockSpec((rows, wa + wb), lambda b, s: (row(b, s), 0)),
        out_shape=jax.ShapeDtypeStruct((t, wa + wb), BF16),
        scratch_shapes=[pltpu.VMEM((heads_a, HEAD_DIM, HEAD_DIM), F32), pltpu.VMEM((heads_b, HEAD_DIM, HEAD_DIM), F32)]
        + [pltpu.VMEM((rows + CONV_TAIL, wb), F32)] * 3,
        compiler_params=_cparams(("parallel", "arbitrary")),
        name="mixer_ab",
    )(proj, proj, proj, proj, proj, proj, proj, proj, small, lb_table, gnorm_a.reshape(1, HEAD_DIM), conv_w,
      a_log.reshape(1, heads_b), dt_bias.reshape(1, heads_b), gnorm_b.reshape(1, HEAD_DIM))


def _out_proj_kernel(a_ref, w_ref, h_ref, o_ref):
    o_ref[...] = h_ref[...] + _dot(a_ref[...], w_ref[...])


def out_proj(a, w, h, tm=1024, tn=1024):
    t, k = a.shape
    tm = min(tm, t)
    n = w.shape[1]
    return pl.pallas_call(
        _out_proj_kernel,
        grid=(t // tm, n // tn),
        in_specs=[pl.BlockSpec((tm, k), lambda i, j: (i, 0)),
                  pl.BlockSpec((k, tn), lambda i, j: (0, j)),
                  pl.BlockSpec((tm, tn), lambda i, j: (i, j))],
        out_specs=pl.BlockSpec((tm, tn), lambda i, j: (i, j)),
        out_shape=jax.ShapeDtypeStruct((t, n), F32),
        compiler_params=_cparams(("parallel", "arbitrary")),
        name="out_proj",
    )(a, w, h)


def _swiglu_kernel(h_ref, g_ref, wg_ref, wu_ref, wd_ref, o_ref, un_ref):
    f = pl.program_id(1)

    @pl.when(f == 0)
    def _():
        un_ref[...] = _rms(h_ref[...], g_ref[...]).astype(BF16)
        o_ref[...] = h_ref[...]

    un = un_ref[...]
    hb = (_silu(_dot(un, wg_ref[...])) * _dot(un, wu_ref[...])).astype(BF16)
    o_ref[...] += _dot(hb, wd_ref[...])


def swiglu(h, g, wg, wu, wd, tm=1024, tf=512):
    t, d = h.shape
    tm = min(tm, t)
    ff = wg.shape[1]
    return pl.pallas_call(
        _swiglu_kernel,
        grid=(t // tm, ff // tf),
        in_specs=[pl.BlockSpec((tm, d), lambda i, f: (i, 0), pipeline_mode=pl.Buffered(1)),
                  pl.BlockSpec((1, d), lambda i, f: (0, 0)),
                  pl.BlockSpec((d, tf), lambda i, f: (0, f)),
                  pl.BlockSpec((d, tf), lambda i, f: (0, f)),
                  pl.BlockSpec((tf, d), lambda i, f: (f, 0))],
        out_specs=pl.BlockSpec((tm, d), lambda i, f: (i, 0)),
        out_shape=jax.ShapeDtypeStruct((t, d), F32),
        scratch_shapes=[pltpu.VMEM((tm, d), BF16)],
        compiler_params=_cparams(("parallel", "arbitrary")),
        name="swiglu",
    )(h, g.reshape(1, d), wg, wu, wd)


def _ple_kernel(*refs, has_add, has_final):
    h_ref, p_ref, g_ref, wg_ref, wp_ref = refs[:5]
    k = 5
    add_ref = gf_ref = None
    if has_add:
        add_ref = refs[k]
        k += 1
    if has_final:
        gf_ref = refs[k]
        k += 1
    o_ref = refs[k]
    h = h_ref[...]
    if has_add:
        h = h + add_ref[...]
    un = _rms(h, g_ref[...]).astype(BF16)
    gate = _sigmoid(_dot(un, wg_ref[...]))
    out = h + gate * _dot(p_ref[...].astype(BF16), wp_ref[...])
    if has_final:
        out = _rms(out, gf_ref[...])
    o_ref[...] = out


def ple(h, p, g, wg, wp, add=None, g_final=None, tm=512):
    t, d = h.shape
    tm = min(tm, t)
    pd = p.shape[1]
    row = lambda i: (i, 0)
    const = lambda i: (0, 0)
    in_specs = [pl.BlockSpec((tm, d), row), pl.BlockSpec((tm, pd), row), pl.BlockSpec((1, d), const),
                pl.BlockSpec((d, d), const), pl.BlockSpec((pd, d), const)]
    args = [h, p, g.reshape(1, d), wg, wp]
    if add is not None:
        in_specs.append(pl.BlockSpec((tm, d), row))
        args.append(add)
    if g_final is not None:
        in_specs.append(pl.BlockSpec((1, d), const))
        args.append(g_final.reshape(1, d))
    return pl.pallas_call(
        functools.partial(_ple_kernel, has_add=add is not None, has_final=g_final is not None),
        grid=(t // tm,),
        in_specs=in_specs,
        out_specs=pl.BlockSpec((tm, d), row),
        out_shape=jax.ShapeDtypeStruct((t, d), F32),
        compiler_params=_cparams(("parallel",)),
        name="ple",
    )(*args)


def _rglru_kernel(x_ref, y_ref, cw_ref, cb_ref, wr_ref, br_ref, wi_ref, bi_ref, lam_ref, o_ref,
                  ext_ref, hc_ref, *, blocks):
    n = x_ref.shape[0]
    first = pl.program_id(1) == 0

    @pl.when(first)
    def _():
        hc_ref[...] = jnp.zeros_like(hc_ref)

    xc = _causal_conv(ext_ref, x_ref[...], cw_ref[...], first) + cb_ref[...]
    bw = xc.shape[1] // blocks
    rowi = lax.broadcasted_iota(jnp.int32, (n, bw), 0)
    at_start = jnp.logical_and(first, rowi == 0)
    gidx = lax.broadcasted_iota(jnp.int32, (n // SCAN_GROUP, SCAN_GROUP, bw), 1)

    for nb in range(blocks):
        sl = slice(nb * bw, (nb + 1) * bw)
        xb = xc[:, sl]
        xbb = xb.astype(BF16)
        r = _sigmoid(_dot(xbb, wr_ref[nb]) + br_ref[:, sl])
        gi = _sigmoid(_dot(xbb, wi_ref[nb]) + bi_ref[:, sl])
        log_a = -RGLRU_C * r * jax.nn.softplus(-lam_ref[:, sl])
        a = jnp.exp(log_a)
        m2 = 1.0 - a * a
        mult = jnp.where(m2 > 0.0, m2 * lax.rsqrt(m2), 0.0)
        mult = jnp.where(at_start, 1.0, mult)
        b = mult * gi * xb
        a = a.reshape(n // SCAN_GROUP, SCAN_GROUP, bw)
        b = b.reshape(n // SCAN_GROUP, SCAN_GROUP, bw)
        sh = 1
        while sh < SCAN_GROUP:
            ok = gidx >= sh
            a_prev = jnp.where(ok, pltpu.roll(a, sh, 1), 1.0)
            b_prev = jnp.where(ok, pltpu.roll(b, sh, 1), 0.0)
            b = b + a * b_prev
            a = a * a_prev
            sh *= 2
        carry = hc_ref[:, sl]
        groups = []
        for gi_ in range(n // SCAN_GROUP):
            hg = b[gi_] + a[gi_] * carry
            groups.append(hg)
            carry = hg[SCAN_GROUP - 1:SCAN_GROUP, :]
        hseq = jnp.concatenate(groups, axis=0)
        hc_ref[:, sl] = carry
        o_ref[:, sl] = (hseq * y_ref[:, sl].astype(F32)).astype(BF16)


def rglru(xr, y, conv_w, conv_b, w_r, b_r, w_i, b_i, lam, *, bsz, seq, rows=256):
    t, cwid = xr.shape
    blocks = w_r.shape[0]
    ns = seq // rows
    row = lambda b, s: (b * ns + s, 0)
    c2 = lambda b, s: (0, 0)
    c3 = lambda b, s: (0, 0, 0)
    vec = lambda a: a.reshape(1, cwid)
    return pl.pallas_call(
        functools.partial(_rglru_kernel, blocks=blocks),
        grid=(bsz, ns),
        in_specs=[pl.BlockSpec((rows, cwid), row), pl.BlockSpec((rows, cwid), row),
                  pl.BlockSpec(conv_w.shape, c2), pl.BlockSpec((1, cwid), c2),
                  pl.BlockSpec(w_r.shape, c3), pl.BlockSpec((1, cwid), c2),
                  pl.BlockSpec(w_i.shape, c3), pl.BlockSpec((1, cwid), c2),
                  pl.BlockSpec((1, cwid), c2)],
        out_specs=pl.BlockSpec((rows, cwid), row),
        out_shape=jax.ShapeDtypeStruct((t, cwid), BF16),
        scratch_shapes=[pltpu.VMEM((rows + CONV_TAIL, cwid), F32), pltpu.VMEM((1, cwid), F32)],
        compiler_params=_cparams(("parallel", "arbitrary")),
        name="rglru",
    )(xr, y, conv_w, vec(conv_b), w_r, vec(b_r), w_i, vec(b_i), vec(lam))


def _router_kernel(h_ref, g_ref, wr_ref, un_ref, pos_ref, gate_ref, cnt_ref):
    tm = h_ref.shape[0]
    ne = wr_ref.shape[0]
    un = _rms(h_ref[...], g_ref[...])
    uh, ul = _split(un)
    un_ref[...] = uh
    wh, wl = _split(wr_ref[...])
    logits = _dot_nt(wh, uh) + _dot_nt(wh, ul) + _dot_nt(wl, uh)
    eidx = lax.broadcasted_iota(jnp.int32, (ne, tm), 0).astype(F32)
    m1 = jnp.max(logits, axis=0, keepdims=True)
    i1 = jnp.min(jnp.where(logits == m1, eidx, float(ne)), axis=0, keepdims=True)
    mask1 = eidx == i1
    rest = jnp.where(mask1, -jnp.inf, logits)
    m2 = jnp.max(rest, axis=0, keepdims=True)
    i2 = jnp.min(jnp.where(rest == m2, eidx, float(ne)), axis=0, keepdims=True)
    mask2 = eidx == i2
    e2 = jnp.exp(m2 - m1)
    g1 = 1.0 / (1.0 + e2)
    g2 = e2 / (1.0 + e2)
    gate_ref[...] = jnp.where(mask1, g1, jnp.where(mask2, g2, 0.0))
    sel = jnp.logical_or(mask1, mask2)
    self32 = jnp.where(sel, 1.0, 0.0)
    before = lax.broadcasted_iota(jnp.int32, (tm, tm), 0) < lax.broadcasted_iota(jnp.int32, (tm, tm), 1)
    rank = _dot(self32.astype(BF16), jnp.where(before, 1.0, 0.0).astype(BF16))
    pos_ref[...] = jnp.where(sel, rank, -1.0)
    cnt = jnp.sum(self32, axis=1, keepdims=True).astype(jnp.int32)
    cnt_ref[0] = jnp.broadcast_to(cnt, cnt_ref.shape[1:])


def router(h, g, wr_t, tm):
    t, d = h.shape
    tm = min(tm, t)
    ne = wr_t.shape[0]
    nt = t // tm
    return pl.pallas_call(
        _router_kernel,
        grid=(nt,),
        in_specs=[pl.BlockSpec((tm, d), lambda i: (i, 0)),
                  pl.BlockSpec((1, d), lambda i: (0, 0)),
                  pl.BlockSpec((ne, d), lambda i: (0, 0))],
        out_specs=[pl.BlockSpec((tm, d), lambda i: (i, 0)),
                   pl.BlockSpec((ne, tm), lambda i: (0, i)),
                   pl.BlockSpec((ne, tm), lambda i: (0, i)),
                   pl.BlockSpec((1, ne, 128), lambda i: (i, 0, 0))],
        out_shape=[jax.ShapeDtypeStruct((t, d), BF16), jax.ShapeDtypeStruct((ne, t), F32),
                   jax.ShapeDtypeStruct((ne, t), F32), jax.ShapeDtypeStruct((nt, ne, 128), jnp.int32)],
        compiler_params=_cparams(("parallel",)),
        name="moe_router",
    )(h, g.reshape(1, d), wr_t)


def _pick(pos, base, rows):
    slot = lax.broadcasted_iota(jnp.int32, (rows, pos.shape[1]), 0).astype(F32)
    return pos == slot + base.astype(F32)


def _moe_gather_kernel(ce_ref, cb_ref, co_ref, nq_ref, un_ref, pos_ref, xs_in_ref, xs_ref, buf_ref, sem, *, rows, qmax):
    del xs_in_ref
    i = pl.program_id(0)
    n = nq_ref[i]

    def copy(slot, off):
        return pltpu.make_async_copy(buf_ref.at[slot], xs_ref.at[pl.ds(off, rows)], sem.at[slot])

    def body(q, carry):
        k = i * qmax + q
        slot = lax.rem(q, 2)

        @pl.when(q >= 2)
        def _():
            copy(slot, 0).wait()

        pos = pos_ref[pl.ds(ce_ref[k], 1), :]
        sel = jnp.where(_pick(pos, cb_ref[k], rows), 1.0, 0.0).astype(BF16)
        buf_ref[slot] = _dot(sel, un_ref[...]).astype(BF16)
        copy(slot, pl.multiple_of(co_ref[k], MOE_GRANULE)).start()
        return carry

    lax.fori_loop(0, n, body, 0)

    @pl.when(n >= 2)
    def _():
        copy(lax.rem(n, 2), 0).wait()

    @pl.when(n >= 1)
    def _():
        copy(lax.rem(n + 1, 2), 0).wait()


def moe_gather(un, pos, tables, n_rows, tm, rows):
    t, d = un.shape
    ne = pos.shape[0]
    ce, cb, co, nq, qmax = tables
    grid_spec = pltpu.PrefetchScalarGridSpec(
        num_scalar_prefetch=4,
        grid=(t // tm,),
        in_specs=[pl.BlockSpec((tm, d), lambda i, *_: (i, 0)),
                  pl.BlockSpec((ne, tm), lambda i, *_: (0, i)),
                  pl.BlockSpec(memory_space=pl.ANY)],
        out_specs=pl.BlockSpec(memory_space=pl.ANY),
        scratch_shapes=[pltpu.VMEM((2, rows, d), BF16), pltpu.SemaphoreType.DMA((2,))],
    )
    return pl.pallas_call(
        functools.partial(_moe_gather_kernel, rows=rows, qmax=qmax),
        grid_spec=grid_spec,
        out_shape=jax.ShapeDtypeStruct((n_rows, d), BF16),
        input_output_aliases={6: 0},
        compiler_params=_cparams(("arbitrary",)),
        name="moe_gather",
    )(ce, cb, co, nq, un, pos, jnp.zeros((n_rows, d), BF16))


def _moe_ffn_kernel(be_ref, nu_ref, x_ref, wg_ref, wu_ref, wd_ref, o_ref, acc_ref):
    del be_ref
    b, f = pl.program_id(0), pl.program_id(1)
    nf = pl.num_programs(1)
    used = b < nu_ref[0]

    @pl.when(f == 0)
    def _():
        acc_ref[...] = jnp.zeros_like(acc_ref)

    @pl.when(used)
    def _():
        x = x_ref[...]
        hb = (_silu(_dot(x, wg_ref[0])) * _dot(x, wu_ref[0])).astype(BF16)
        acc_ref[...] += _dot(hb, wd_ref[0])

    @pl.when(f == nf - 1)
    def _():
        o_ref[...] = acc_ref[...].astype(BF16)


def moe_ffn(xs, blk_e, n_used, wg, wu, wd, bm, tf=512):
    n_rows, d = xs.shape
    ff = wg.shape[2]
    nf = ff // tf

    def fidx(b, f, nu):
        return jnp.where(b < nu[0], f, nf - 1)

    grid_spec = pltpu.PrefetchScalarGridSpec(
        num_scalar_prefetch=2,
        grid=(n_rows // bm, nf),
        in_specs=[pl.BlockSpec((bm, d), lambda b, f, be, nu: (b, 0)),
                  pl.BlockSpec((1, d, tf), lambda b, f, be, nu: (be[b], 0, fidx(b, f, nu))),
                  pl.BlockSpec((1, d, tf), lambda b, f, be, nu: (be[b], 0, fidx(b, f, nu))),
                  pl.BlockSpec((1, tf, d), lambda b, f, be, nu: (be[b], fidx(b, f, nu), 0))],
        out_specs=pl.BlockSpec((bm, d), lambda b, f, be, nu: (b, 0)),
        scratch_shapes=[pltpu.VMEM((bm, d), F32)],
    )
    return pl.pallas_call(
        _moe_ffn_kernel,
        grid_spec=grid_spec,
        out_shape=jax.ShapeDtypeStruct((n_rows, d), BF16),
        compiler_params=_cparams(("parallel", "arbitrary")),
        name="moe_ffn",
    )(blk_e, n_used, xs, wg, wu, wd)


def _moe_combine_kernel(ce_ref, cb_ref, co_ref, nq_ref, pos_ref, gate_ref, y_ref, o_ref, buf_ref, sem, *, rows, qmax):
    i = pl.program_id(0)
    n = nq_ref[i]
    o_ref[...] = jnp.zeros_like(o_ref)

    def copy(slot, off):
        return pltpu.make_async_copy(y_ref.at[pl.ds(off, rows)], buf_ref.at[slot], sem.at[slot])

    def start(q):
        copy(lax.rem(q, 2), pl.multiple_of(co_ref[i * qmax + q], MOE_GRANULE)).start()

    @pl.when(n > 0)
    def _():
        start(0)

    def body(q, carry):
        k = i * qmax + q
        slot = lax.rem(q, 2)

        @pl.when(q + 1 < n)
        def _():
            start(q + 1)

        copy(slot, 0).wait()
        e = ce_ref[k]
        hit = _pick(pos_ref[pl.ds(e, 1), :], cb_ref[k], rows)
        gsub = jnp.sum(jnp.where(hit, gate_ref[pl.ds(e, 1), :], 0.0), axis=1, keepdims=True)
        yb = (buf_ref[slot].astype(F32) * gsub).astype(BF16)
        o_ref[...] += _dot_tn(jnp.where(hit, 1.0, 0.0).astype(BF16), yb)
        return carry

    lax.fori_loop(0, n, body, 0)


def moe_combine(ys, pos, gate, tables, tm, rows):
    ne, t = pos.shape
    d = ys.shape[1]
    ce, cb, co, nq, qmax = tables
    grid_spec = pltpu.PrefetchScalarGridSpec(
        num_scalar_prefetch=4,
        grid=(t // tm,),
        in_specs=[pl.BlockSpec((ne, tm), lambda i, *_: (0, i)),
                  pl.BlockSpec((ne, tm), lambda i, *_: (0, i)),
                  pl.BlockSpec(memory_space=pl.ANY)],
        out_specs=pl.BlockSpec((tm, d), lambda i, *_: (i, 0)),
        scratch_shapes=[pltpu.VMEM((2, rows, d), BF16), pltpu.SemaphoreType.DMA((2,))],
    )
    return pl.pallas_call(
        functools.partial(_moe_combine_kernel, rows=rows, qmax=qmax),
        grid_spec=grid_spec,
        out_shape=jax.ShapeDtypeStruct((t, d), F32),
        compiler_params=_cparams(("arbitrary",)),
        name="moe_combine",
    )(ce, cb, co, nq, pos, gate, ys)


def _chunk_tables(counts, seg, rows, qmax):
    ne = counts.shape[1]
    ns = (counts + rows - 1) // rows
    cs = jnp.cumsum(ns, axis=1)
    q = jnp.arange(qmax, dtype=jnp.int32)
    ce = jnp.minimum(jnp.sum(q[None, :, None] >= cs[:, None, :], axis=-1), ne - 1).astype(jnp.int32)
    cj = q[None, :] - jnp.take_along_axis(cs - ns, ce, axis=1)
    co = jnp.take_along_axis(seg, ce, axis=1) + cj * rows
    flat = lambda a: a.reshape(-1).astype(jnp.int32)
    return flat(ce), flat(cj * rows), flat(co), cs[:, -1].astype(jnp.int32), qmax


def moe(h, g, w_router, wg, wu, wd, tm=MOE_TILE, bm=MOE_BLOCK):
    t = h.shape[0]
    tm = min(tm, t)
    ne = wg.shape[0]
    nt = t // tm
    un, pos, gate, cnt = router(h, g, w_router.T, tm)
    counts = cnt[:, :, 0]
    padded = (counts + MOE_GRANULE - 1) // MOE_GRANULE * MOE_GRANULE
    tot = jnp.sum(padded, axis=0)
    ptot = (tot + MOE_SLACK + bm - 1) // bm * bm
    eend = jnp.cumsum(ptot)
    seg = (eend - ptot)[None, :] + jnp.cumsum(padded, axis=0) - padded
    n_blocks = (TOP_K * t + nt * ne * (MOE_GRANULE - 1) + ne * MOE_SLACK) // bm + ne
    n_used = (eend[-1:] // bm).astype(jnp.int32)
    blk_e = jnp.minimum(jnp.searchsorted(eend, jnp.arange(n_blocks, dtype=jnp.int32) * bm, side="right"),
                        ne - 1).astype(jnp.int32)
    g_tab = _chunk_tables(counts, seg, MOE_GATHER_ROWS, TOP_K * tm // MOE_GATHER_ROWS + ne)
    c_tab = _chunk_tables(counts, seg, MOE_COMBINE_ROWS, TOP_K * tm // MOE_COMBINE_ROWS + ne)
    xs = moe_gather(un, pos, g_tab, n_blocks * bm, tm, MOE_GATHER_ROWS)
    ys = moe_ffn(xs, blk_e, n_used, wg, wu, wd, bm)
    return moe_combine(ys, pos, gate, c_tab, tm, MOE_COMBINE_ROWS)


def kernel(x, p, ln_mix, ln_ffn, ln_ple, ln_final, lb_table, ab_w_in, ab_conv, b_a_log, b_dt_bias, a_gnorm, b_gnorm, ab_w_out, c_w_in, c_conv_w, c_conv_b, c_w_r, c_b_r, c_w_i, c_b_i, c_lambda, c_w_out, ffn_w_gate, ffn_w_up, ffn_w_down, moe_router, moe_w_gate, moe_w_up, moe_w_down, ple_w_proj, ple_w_gate):
    bsz, seq, d = x.shape
    t = bsz * seq
    depth = ln_mix.shape[0]
    a_heads = lb_table.shape[1] // HEAD_DIM
    b_heads = b_a_log.shape[1]
    a_w = a_heads * HEAD_DIM
    b_w = b_heads * HEAD_DIM
    main_w = 4 * a_w + 4 * b_w
    bf = lambda a: a.astype(BF16)

    h = x.reshape(t, d)
    for layer in range(depth):
        j = layer // 2
        if layer % 2 == 0:
            w_in = ab_w_in[j]
            w_small = jnp.pad(w_in[:, main_w:], ((0, 0), (0, 128 - 2 * b_heads)))
            proj, small = norm_proj(h, ln_mix[layer], bf(w_in[:, :main_w]), bf(w_small))
            mixed = mixer_ab(proj, small, lb_table, a_gnorm[j], ab_conv[j], b_a_log[j], b_dt_bias[j], b_gnorm[j],
                             bsz=bsz, seq=seq, heads_a=a_heads, heads_b=b_heads, layer=layer)
            h = out_proj(mixed, bf(ab_w_out[j]), h)
            h = swiglu(h, ln_ffn[layer], bf(ffn_w_gate[j]), bf(ffn_w_up[j]), bf(ffn_w_down[j]))
            add = None
        else:
            xr, yb = norm_proj_gelu(h, ln_mix[layer], bf(c_w_in[j]))
            hy = rglru(xr, yb, c_conv_w[j], c_conv_b[j], bf(c_w_r[j]), c_b_r[j], bf(c_w_i[j]), c_b_i[j],
                       c_lambda[j], bsz=bsz, seq=seq)
            h = out_proj(hy, bf(c_w_out[j]), h)
            add = moe(h, ln_ffn[layer], moe_router[j], bf(moe_w_gate[j]), bf(moe_w_up[j]), bf(moe_w_down[j]))
        g_final = ln_final if layer == depth - 1 else None
        h = ple(h, p[layer].reshape(t, -1), ln_ple[layer], bf(ple_w_gate[layer]), bf(ple_w_proj[layer]),
                add=add, g_final=g_final)
    if depth == 0:
        raise ValueError("depth must be positive")
    return h.reshape(bsz, seq, d)
```

```python
import functools

import jax
import jax.numpy as jnp
from jax import lax
from jax.experimental import pallas as pl
from jax.experimental.pallas import tpu as pltpu

F32 = jnp.float32
BF16 = jnp.bfloat16
EPS = 1e-6
CHUNK = 64
SUB = 8
HEAD_DIM = 128
CONV_WIDTH = 4
GDN_HEAD_GROUP = 8
MIXER_ROWS = 128
FILL_AFTER_PREP = 4
FILL_AFTER_SCORES = 4
FILL_PER_LEVEL = 1
CONV_TAIL = 8
RGLRU_C = 8.0
SCAN_GROUP = 8
TOP_K = 2
MOE_TILE = 1024
MOE_BLOCK = 1024
MOE_PART_ROWS = 256
MOE_GRANULE = 16
MOE_GATHER_ROWS = 128
MOE_COMBINE_ROWS = 256
MOE_SLACK = 256
VMEM_LIMIT = 56 * 1024 * 1024


def _cparams(sem, vmem=VMEM_LIMIT):
    return pltpu.CompilerParams(dimension_semantics=sem, vmem_limit_bytes=vmem)


def _dot(a, b):
    return jnp.dot(a, b, preferred_element_type=F32)


def _dot_nt(a, b):
    return lax.dot_general(a, b, (((1,), (1,)), ((), ())), preferred_element_type=F32)


def _dot_tn(a, b):
    return lax.dot_general(a, b, (((0,), (0,)), ((), ())), preferred_element_type=F32)


def _split(a):
    hi = a.astype(BF16)
    lo = (a - hi.astype(F32)).astype(BF16)
    return hi, lo


def _dot2(a, b):
    ah, al = _split(a)
    bh = b.astype(BF16)
    return _dot(jnp.concatenate([ah, al], axis=1), jnp.concatenate([bh, bh], axis=0))


def _rms(x, g):
    return x * lax.rsqrt(jnp.mean(x * x, axis=-1, keepdims=True) + EPS) * g


def _sigmoid(x):
    return 1.0 / (1.0 + jnp.exp(-x))


def _silu(x):
    return x * _sigmoid(x)


def _norm_proj_kernel(h_ref, g_ref, w_ref, ws_ref, o_ref, os_ref, un_ref):
    j = pl.program_id(1)

    @pl.when(j == 0)
    def _():
        un = _rms(h_ref[...], g_ref[...]).astype(BF16)
        un_ref[...] = un
        os_ref[...] = _dot(un, ws_ref[...])

    o_ref[...] = _dot(un_ref[...], w_ref[...])


def norm_proj(h, g, w, ws, tm=1024, tn=1024):
    t, d = h.shape
    tm = min(tm, t)
    n = w.shape[1]
    return pl.pallas_call(
        _norm_proj_kernel,
        grid=(t // tm, n // tn),
        in_specs=[pl.BlockSpec((tm, d), lambda i, j: (i, 0)),
                  pl.BlockSpec((1, d), lambda i, j: (0, 0)),
                  pl.BlockSpec((d, tn), lambda i, j: (0, j)),
                  pl.BlockSpec((d, ws.shape[1]), lambda i, j: (0, 0))],
        out_specs=[pl.BlockSpec((tm, tn), lambda i, j: (i, j)),
                   pl.BlockSpec((tm, ws.shape[1]), lambda i, j: (i, 0))],
        out_shape=[jax.ShapeDtypeStruct((t, n), F32), jax.ShapeDtypeStruct((t, ws.shape[1]), F32)],
        scratch_shapes=[pltpu.VMEM((tm, d), BF16)],
        compiler_params=_cparams(("parallel", "arbitrary")),
        name="norm_proj",
    )(h, g.reshape(1, d), w, ws)


def _norm_proj_gelu_kernel(h_ref, g_ref, wy_ref, wx_ref, o_ref, y_ref, un_ref):
    @pl.when(pl.program_id(1) == 0)
    def _():
        un_ref[...] = _rms(h_ref[...], g_ref[...]).astype(BF16)

    un = un_ref[...]
    y_ref[...] = jax.nn.gelu(_dot(un, wy_ref[...])).astype(BF16)
    o_ref[...] = _dot(un, wx_ref[...])


def norm_proj_gelu(h, g, w, tm=1024, tn=512):
    t, d = h.shape
    tm = min(tm, t)
    half = w.shape[1] // 2
    nh = half // tn
    return pl.pallas_call(
        _norm_proj_gelu_kernel,
        grid=(t // tm, nh),
        in_specs=[pl.BlockSpec((tm, d), lambda i, j: (i, 0)),
                  pl.BlockSpec((1, d), lambda i, j: (0, 0)),
                  pl.BlockSpec((d, tn), lambda i, j: (0, j)),
                  pl.BlockSpec((d, tn), lambda i, j: (0, nh + j))],
        out_specs=[pl.BlockSpec((tm, tn), lambda i, j: (i, j)),
                   pl.BlockSpec((tm, tn), lambda i, j: (i, j))],
        out_shape=[jax.ShapeDtypeStruct((t, half), F32), jax.ShapeDtypeStruct((t, half), BF16)],
        scratch_shapes=[pltpu.VMEM((tm, d), BF16)],
        compiler_params=_cparams(("parallel", "arbitrary")),
        name="norm_proj_gelu",
    )(h, g.reshape(1, d), w, w)


def _hgrn2_work(q_ref, f_ref, i_ref, g_ref, lbt_ref, gn_ref, o_ref, st_ref, *, layer, heads):
    c = CHUNK

    lbt = lbt_ref[...]
    e = jnp.exp(lbt - jnp.max(lbt, axis=0, keepdims=True))
    lb_all = jnp.sum(e[:layer + 1], axis=0, keepdims=True) / jnp.sum(e, axis=0, keepdims=True)

    row = lax.broadcasted_iota(jnp.int32, (c, c), 0)
    col = lax.broadcasted_iota(jnp.int32, (c, c), 1)
    tril = jnp.where(row >= col, 1.0, 0.0).astype(BF16)
    gn = gn_ref[...]

    levels = []
    ln = c // 2
    while ln >= SUB:
        levels += [(m * 2 * ln, m * 2 * ln + ln, ln) for m in range(c // (2 * ln))]
        ln //= 2
    n_pairs = sum(l[2] for l in levels)

    def seg_id(idx):
        sid = jnp.zeros_like(idx)
        start = 0
        for l in levels[:-1]:
            start += l[2]
            sid = sid + jnp.where(idx >= start, 1, 0)
        return sid

    same_seg = (seg_id(lax.broadcasted_iota(jnp.int32, (n_pairs, n_pairs), 0))
                == seg_id(lax.broadcasted_iota(jnp.int32, (n_pairs, n_pairs), 1)))
    sub_i = lax.broadcasted_iota(jnp.int32, (c // SUB, SUB, HEAD_DIM), 1)

    sls = [slice(h * HEAD_DIM, (h + 1) * HEAD_DIM) for h in range(heads)]

    def gates(h):
        lb = lb_all[:, sls[h]]
        q = q_ref[:, sls[h]] * (HEAD_DIM ** -0.5)
        forget = lb + (1.0 - lb) * _sigmoid(f_ref[:, sls[h]])
        lh, ll = _split(jnp.log(forget))
        b2 = _dot(tril, jnp.concatenate([lh, ll], axis=1))
        return q, 1.0 - forget, i_ref[:, sls[h]], b2[:, :HEAD_DIM] + b2[:, HEAD_DIM:]

    def block_pairs(h, q, k, v, b):
        o = _dot_nt((q * jnp.exp(b)).astype(BF16), st_ref[h].astype(BF16))
        qs, ks, vs = [], [], []
        for k0, q0, ln in levels:
            bref = b[q0 - 1:q0, :]
            qs.append(q[q0:q0 + ln] * jnp.exp(b[q0:q0 + ln] - bref))
            ks.append(k[k0:k0 + ln] * jnp.exp(bref - b[k0:k0 + ln]))
            vs.append(v[k0:k0 + ln])
        s = _dot_nt(jnp.concatenate(qs, axis=0).astype(BF16), jnp.concatenate(ks, axis=0).astype(BF16))
        r = _dot(jnp.where(same_seg, s, 0.0).astype(BF16), jnp.concatenate(vs, axis=0).astype(BF16))
        groups = [None] * (c // SUB)
        start = 0
        for k0, q0, ln in levels:
            for j in range(ln // SUB):
                piece = r[start + j * SUB:start + (j + 1) * SUB]
                gi = q0 // SUB + j
                groups[gi] = piece if groups[gi] is None else groups[gi] + piece
            start += ln
        groups[0] = jnp.zeros((SUB, HEAD_DIM), F32)
        return o + jnp.concatenate(groups, axis=0)

    def near_pairs(q, k, v, b):
        q3, k3, v3, b3 = (a.reshape(c // SUB, SUB, HEAD_DIM) for a in (q, k, v, b))
        o3 = jnp.sum(q3 * k3, axis=2, keepdims=True) * v3
        for d in range(1, SUB):
            dec = jnp.exp(jnp.where(sub_i >= d, b3 - pltpu.roll(b3, d, 1), -jnp.inf))
            w = jnp.sum(q3 * pltpu.roll(k3, d, 1) * dec, axis=2, keepdims=True)
            o3 = o3 + w * pltpu.roll(v3, d, 1)
        return o3.reshape(c, HEAD_DIM)

    qkvb, far = {}, {}

    def state_part(h):
        qkvb[h] = gates(h)
        q, k, v, b = qkvb[h]
        far[h] = block_pairs(h, q, k, v, b)
        blast = b[c - 1:c, :]
        kd = (k * jnp.exp(blast - b)).astype(BF16)
        st_ref[h] = st_ref[h] * jnp.exp(blast) + _dot_tn(v.astype(BF16), kd)

    def block_part(h):
        o = far[h] + near_pairs(*qkvb[h])
        on = o * lax.rsqrt(jnp.mean(o * o, axis=1, keepdims=True) + EPS) * gn
        o_ref[:, sls[h]] = (on * _silu(g_ref[:, sls[h]])).astype(BF16)

    return ([functools.partial(state_part, h) for h in range(heads)]
            + [functools.partial(block_part, h) for h in range(heads)])


def _causal_conv(ext_ref, x, w, first):
    n = x.shape[0]

    @pl.when(first)
    def _():
        ext_ref[0:CONV_TAIL, :] = jnp.zeros((CONV_TAIL, x.shape[1]), F32)

    ext_ref[CONV_TAIL:CONV_TAIL + n, :] = x
    y = x * w[CONV_WIDTH - 1:CONV_WIDTH, :]
    for k in range(CONV_WIDTH - 1):
        off = CONV_TAIL - (CONV_WIDTH - 1) + k
        y = y + ext_ref[off:off + n, :] * w[k:k + 1, :]
    ext_ref[0:CONV_TAIL, :] = ext_ref[n:n + CONV_TAIL, :]
    return y


def _mixer_ab_kernel(qa_ref, fa_ref, ia_ref, ga_ref, q_ref, k_ref, v_ref, z_ref, sm_ref, lbt_ref, gna_ref, cw_ref,
                     alog_ref, dtb_ref, gn_ref, o_ref, sta_ref, st_ref, eq_ref, ek_ref, ev_ref,
                     *, heads_a, heads, layer):
    c = CHUNK
    first = pl.program_id(1) == 0

    @pl.when(first)
    def _():
        sta_ref[...] = jnp.zeros_like(sta_ref)
        st_ref[...] = jnp.zeros_like(st_ref)

    col0 = heads_a * HEAD_DIM
    hw = heads * HEAD_DIM
    cw = cw_ref[...]
    qc_all = _silu(_causal_conv(eq_ref, q_ref[...], cw[:, 0:hw], first))
    kc_all = _silu(_causal_conv(ek_ref, k_ref[...], cw[:, hw:2 * hw], first))
    vc_all = _silu(_causal_conv(ev_ref, v_ref[...], cw[:, 2 * hw:3 * hw], first))

    row = lax.broadcasted_iota(jnp.int32, (c, c), 0)
    col = lax.broadcasted_iota(jnp.int32, (c, c), 1)
    causal = row >= col
    strict = row > col
    eye = row == col
    eye_f = jnp.where(eye, 1.0, 0.0)
    diag_blk = (row // SUB) == (col // SUB)
    merge_masks = []
    s = SUB
    while s < c:
        merge_masks.append(jnp.logical_and((row // (2 * s)) == (col // (2 * s)), (row // s) == (col // s) + 1))
        s *= 2
    gn = gn_ref[...]

    def to_row(colv):
        return jnp.sum(jnp.where(eye, colv, 0.0), axis=0, keepdims=True)

    def head_group(hg, qc, kc, vc, sm, z_ref, o_ref, fill):
        hs = range(len(hg))
        sls = [slice(h * HEAD_DIM, (h + 1) * HEAD_DIM) for h in hg]
        qn, kn, knb, beta, gam_col, egam, decay, kbeta = [], [], [], [], [], [], [], []
        for i, h in enumerate(hg):
            qh, kh = qc[:, sls[i]], kc[:, sls[i]]
            qn.append(qh * lax.rsqrt(jnp.sum(qh * qh, axis=1, keepdims=True) + EPS) * (HEAD_DIM ** -0.5))
            kn.append(kh * lax.rsqrt(jnp.sum(kh * kh, axis=1, keepdims=True) + EPS))
            knb.append(kn[i].astype(BF16))
            beta.append(_sigmoid(sm[:, heads + h:heads + h + 1]))
            g_col = -jnp.exp(alog_ref[:, h:h + 1]) * jax.nn.softplus(sm[:, h:h + 1] + dtb_ref[:, h:h + 1])
            g_row = to_row(g_col)
            gam_col.append(jnp.sum(jnp.where(causal, g_row, 0.0), axis=1, keepdims=True))
            gam_row = jnp.sum(jnp.where(strict, 0.0, g_col), axis=0, keepdims=True)
            decay.append(jnp.exp(jnp.where(causal, gam_col[i] - gam_row, -jnp.inf)))
            egam.append(jnp.exp(gam_col[i]))
            kbeta.append(kn[i] * beta[i])

        fill(FILL_AFTER_PREP)
        a_mat = [jnp.where(strict, _dot_nt(kbeta[i].astype(BF16), knb[i]) * decay[i], 0.0) for i in hs]
        qk = [(_dot_nt(qn[i].astype(BF16), knb[i]) * decay[i]).astype(BF16) for i in hs]
        fill(FILL_AFTER_SCORES)
        d_mat = [jnp.where(diag_blk, a_mat[i], 0.0) for i in hs]
        x = [eye_f - d_mat[i] for i in hs]
        p = [_dot2(d_mat[i], d_mat[i]) for i in hs]
        fill(FILL_PER_LEVEL)
        n_sq = (SUB - 1).bit_length() - 1
        for lvl in range(n_sq):
            if lvl < n_sq - 1:
                y = [_dot2(jnp.concatenate([x[i], p[i]], axis=0), p[i]) for i in hs]
                fill(FILL_PER_LEVEL)
                x = [x[i] + y[i][:c] for i in hs]
                p = [y[i][c:] for i in hs]
            else:
                y = [_dot2(x[i], p[i]) for i in hs]
                fill(FILL_PER_LEVEL)
                x = [x[i] + y[i] for i in hs]
        for below in merge_masks:
            y = [_dot2(x[i], jnp.where(below, a_mat[i], 0.0)) for i in hs]
            fill(FILL_PER_LEVEL)
            y = [_dot2(y[i], x[i]) for i in hs]
            fill(FILL_PER_LEVEL)
            x = [x[i] - y[i] for i in hs]
        rhs = [jnp.concatenate([vc[:, sls[i]] * beta[i], kbeta[i] * egam[i]], axis=1).astype(BF16) for i in hs]
        uw = [_dot(x[i].astype(BF16), rhs[i]) for i in hs]

        st = [st_ref[h] for h in hg]
        stb = [s.astype(BF16) for s in st]
        v_new = [uw[i][:, :HEAD_DIM] - _dot_nt(uw[i][:, HEAD_DIM:].astype(BF16), stb[i]) for i in hs]
        vnb = [v.astype(BF16) for v in v_new]
        o = [_dot_nt((qn[i] * egam[i]).astype(BF16), stb[i]) + _dot(qk[i], vnb[i]) for i in hs]
        for i, h in enumerate(hg):
            glast = gam_col[i][c - 1:c, :]
            kd = (kn[i] * jnp.exp(glast - gam_col[i])).astype(BF16)
            st_ref[h] = st[i] * jnp.exp(glast) + _dot_tn(vnb[i], kd)
        for i in hs:
            on = o[i] * lax.rsqrt(jnp.mean(o[i] * o[i], axis=1, keepdims=True) + EPS) * gn
            o_ref[:, col0 + hg[i] * HEAD_DIM:col0 + (hg[i] + 1) * HEAD_DIM] = (on * _silu(z_ref[:, sls[i]])).astype(BF16)

    for ci in range(q_ref.shape[0] // c):
        rs = pl.ds(ci * c, c)
        pending = _hgrn2_work(qa_ref.at[rs], fa_ref.at[rs], ia_ref.at[rs], ga_ref.at[rs], lbt_ref, gna_ref,
                              o_ref.at[rs], sta_ref, layer=layer, heads=heads_a)

        def fill(n, pending=pending):
            for _ in range(min(n, len(pending))):
                pending.pop(0)()

        r0 = ci * c
        for h0 in range(0, heads, GDN_HEAD_GROUP):
            head_group(list(range(h0, min(h0 + GDN_HEAD_GROUP, heads))), qc_all[r0:r0 + c], kc_all[r0:r0 + c],
                       vc_all[r0:r0 + c], sm_ref[rs, :], z_ref.at[rs], o_ref.at[rs], fill)
        fill(len(pending))


def mixer_ab(proj, small, lb_table, gnorm_a, conv_w, a_log, dt_bias, gnorm_b, *, bsz, seq, heads_a, heads_b, layer):
    t = proj.shape[0]
    wa, wb = heads_a * HEAD_DIM, heads_b * HEAD_DIM
    rows = min(MIXER_ROWS, seq)
    nc = seq // rows
    row = lambda b, s: b * nc + s

    def spec_a(k):
        return pl.BlockSpec((rows, wa), lambda b, s: (row(b, s), k))

    def spec_b(k):
        return pl.BlockSpec((rows, wb), lambda b, s: (row(b, s), 4 * wa // wb + k))

    def const(shape):
        return pl.BlockSpec(shape, lambda b, s: (0, 0))

    return pl.pallas_call(
        functools.partial(_mixer_ab_kernel, heads_a=heads_a, heads=heads_b, layer=layer),
        grid=(bsz, nc),
        in_specs=[spec_a(0), spec_a(1), spec_a(2), spec_a(3), spec_b(0), spec_b(1), spec_b(2), spec_b(3),
                  pl.BlockSpec((rows, small.shape[1]), lambda b, s: (row(b, s), 0)),
                  const(lb_table.shape), const((1, HEAD_DIM)),
                  const(conv_w.shape), const((1, heads_b)), const((1, heads_b)), const((1, HEAD_DIM))],
        out_specs=pl.BlockSpec((rows, wa + wb), lambda b, s: (row(b, s), 0)),
        out_shape=jax.ShapeDtypeStruct((t, wa + wb), BF16),
        scratch_shapes=[pltpu.VMEM((heads_a, HEAD_DIM, HEAD_DIM), F32), pltpu.VMEM((heads_b, HEAD_DIM, HEAD_DIM), F32)]
        + [pltpu.VMEM((rows + CONV_TAIL, wb), F32)] * 3,
        compiler_params=_cparams(("parallel", "arbitrary")),
        name="mixer_ab",
    )(proj, proj, proj, proj, proj, proj, proj, proj, small, lb_table, gnorm_a.reshape(1, HEAD_DIM), conv_w,
      a_log.reshape(1, heads_b), dt_bias.reshape(1, heads_b), gnorm_b.reshape(1, HEAD_DIM))


def _out_proj_kernel(a_ref, w_ref, h_ref, o_ref):
    o_ref[...] = h_ref[...] + _dot(a_ref[...], w_ref[...])


def out_proj(a, w, h, tm=1024, tn=1024):
    t, k = a.shape
    tm = min(tm, t)
    n = w.shape[1]
    return pl.pallas_call(
        _out_proj_kernel,
        grid=(t // tm, n // tn),
        in_specs=[pl.BlockSpec((tm, k), lambda i, j: (i, 0)),
                  pl.BlockSpec((k, tn), lambda i, j: (0, j)),
                  pl.BlockSpec((tm, tn), lambda i, j: (i, j))],
        out_specs=pl.BlockSpec((tm, tn), lambda i, j: (i, j)),
        out_shape=jax.ShapeDtypeStruct((t, n), F32),
        compiler_params=_cparams(("parallel", "arbitrary")),
        name="out_proj",
    )(a, w, h)


def _swiglu_kernel(h_ref, g_ref, wg_ref, wu_ref, wd_ref, o_ref, un_ref):
    f = pl.program_id(1)

    @pl.when(f == 0)
    def _():
        un_ref[...] = _rms(h_ref[...], g_ref[...]).astype(BF16)
        o_ref[...] = h_ref[...]

    un = un_ref[...]
    hb = (_silu(_dot(un, wg_ref[...])) * _dot(un, wu_ref[...])).astype(BF16)
    o_ref[...] += _dot(hb, wd_ref[...])


def swiglu(h, g, wg, wu, wd, tm=1024, tf=512):
    t, d = h.shape
    tm = min(tm, t)
    ff = wg.shape[1]
    return pl.pallas_call(
        _swiglu_kernel,
        grid=(t // tm, ff // tf),
        in_specs=[pl.BlockSpec((tm, d), lambda i, f: (i, 0), pipeline_mode=pl.Buffered(1)),
                  pl.BlockSpec((1, d), lambda i, f: (0, 0)),
                  pl.BlockSpec((d, tf), lambda i, f: (0, f)),
                  pl.BlockSpec((d, tf), lambda i, f: (0, f)),
                  pl.BlockSpec((tf, d), lambda i, f: (f, 0))],
        out_specs=pl.BlockSpec((tm, d), lambda i, f: (i, 0)),
        out_shape=jax.ShapeDtypeStruct((t, d), F32),
        scratch_shapes=[pltpu.VMEM((tm, d), BF16)],
        compiler_params=_cparams(("parallel", "arbitrary")),
        name="swiglu",
    )(h, g.reshape(1, d), wg, wu, wd)


def _ple_kernel(*refs, has_add, has_final):
    h_ref, p_ref, g_ref, wg_ref, wp_ref = refs[:5]
    k = 5
    add_ref = gf_ref = None
    if has_add:
        add_ref = refs[k]
        k += 1
    if has_final:
        gf_ref = refs[k]
        k += 1
    o_ref = refs[k]
    h = h_ref[...]
    if has_add:
        h = h + add_ref[...]
    un = _rms(h, g_ref[...]).astype(BF16)
    gate = _sigmoid(_dot(un, wg_ref[...]))
    out = h + gate * _dot(p_ref[...].astype(BF16), wp_ref[...])
    if has_final:
        out = _rms(out, gf_ref[...])
    o_ref[...] = out


def ple(h, p, g, wg, wp, add=None, g_final=None, tm=512):
    t, d = h.shape
    tm = min(tm, t)
    pd = p.shape[1]
    row = lambda i: (i, 0)
    const = lambda i: (0, 0)
    in_specs = [pl.BlockSpec((tm, d), row), pl.BlockSpec((tm, pd), row), pl.BlockSpec((1, d), const),
                pl.BlockSpec((d, d), const), pl.BlockSpec((pd, d), const)]
    args = [h, p, g.reshape(1, d), wg, wp]
    if add is not None:
        in_specs.append(pl.BlockSpec((tm, d), row))
        args.append(add)
    if g_final is not None:
        in_specs.append(pl.BlockSpec((1, d), const))
        args.append(g_final.reshape(1, d))
    return pl.pallas_call(
        functools.partial(_ple_kernel, has_add=add is not None, has_final=g_final is not None),
        grid=(t // tm,),
        in_specs=in_specs,
        out_specs=pl.BlockSpec((tm, d), row),
        out_shape=jax.ShapeDtypeStruct((t, d), F32),
        compiler_params=_cparams(("parallel",)),
        name="ple",
    )(*args)


def _rglru_kernel(x_ref, y_ref, cw_ref, cb_ref, wr_ref, br_ref, wi_ref, bi_ref, lam_ref, o_ref,
                  ext_ref, hc_ref, *, blocks):
    n = x_ref.shape[0]
    first = pl.program_id(1) == 0

    @pl.when(first)
    def _():
        hc_ref[...] = jnp.zeros_like(hc_ref)

    xc = _causal_conv(ext_ref, x_ref[...], cw_ref[...], first) + cb_ref[...]
    bw = xc.shape[1] // blocks
    rowi = lax.broadcasted_iota(jnp.int32, (n, bw), 0)
    at_start = jnp.logical_and(first, rowi == 0)
    gidx = lax.broadcasted_iota(jnp.int32, (n // SCAN_GROUP, SCAN_GROUP, bw), 1)

    for nb in range(blocks):
        sl = slice(nb * bw, (nb + 1) * bw)
        xb = xc[:, sl]
        xbb = xb.astype(BF16)
        r = _sigmoid(_dot(xbb, wr_ref[nb]) + br_ref[:, sl])
        gi = _sigmoid(_dot(xbb, wi_ref[nb]) + bi_ref[:, sl])
        log_a = -RGLRU_C * r * jax.nn.softplus(-lam_ref[:, sl])
        a = jnp.exp(log_a)
        m2 = 1.0 - a * a
        mult = jnp.where(m2 > 0.0, m2 * lax.rsqrt(m2), 0.0)
        mult = jnp.where(at_start, 1.0, mult)
        b = mult * gi * xb
        a = a.reshape(n // SCAN_GROUP, SCAN_GROUP, bw)
        b = b.reshape(n // SCAN_GROUP, SCAN_GROUP, bw)
        sh = 1
        while sh < SCAN_GROUP:
            ok = gidx >= sh
            a_prev = jnp.where(ok, pltpu.roll(a, sh, 1), 1.0)
            b_prev = jnp.where(ok, pltpu.roll(b, sh, 1), 0.0)
            b = b + a * b_prev
            a = a * a_prev
            sh *= 2
        carry = hc_ref[:, sl]
        groups = []
        for gi_ in range(n // SCAN_GROUP):
            hg = b[gi_] + a[gi_] * carry
            groups.append(hg)
            carry = hg[SCAN_GROUP - 1:SCAN_GROUP, :]
        hseq = jnp.concatenate(groups, axis=0)
        hc_ref[:, sl] = carry
        o_ref[:, sl] = (hseq * y_ref[:, sl].astype(F32)).astype(BF16)


def rglru(xr, y, conv_w, conv_b, w_r, b_r, w_i, b_i, lam, *, bsz, seq, rows=256):
    t, cwid = xr.shape
    blocks = w_r.shape[0]
    ns = seq // rows
    row = lambda b, s: (b * ns + s, 0)
    c2 = lambda b, s: (0, 0)
    c3 = lambda b, s: (0, 0, 0)
    vec = lambda a: a.reshape(1, cwid)
    return pl.pallas_call(
        functools.partial(_rglru_kernel, blocks=blocks),
        grid=(bsz, ns),
        in_specs=[pl.BlockSpec((rows, cwid), row), pl.BlockSpec((rows, cwid), row),
                  pl.BlockSpec(conv_w.shape, c2), pl.BlockSpec((1, cwid), c2),
                  pl.BlockSpec(w_r.shape, c3), pl.BlockSpec((1, cwid), c2),
                  pl.BlockSpec(w_i.shape, c3), pl.BlockSpec((1, cwid), c2),
                  pl.BlockSpec((1, cwid), c2)],
        out_specs=pl.BlockSpec((rows, cwid), row),
        out_shape=jax.ShapeDtypeStruct((t, cwid), BF16),
        scratch_shapes=[pltpu.VMEM((rows + CONV_TAIL, cwid), F32), pltpu.VMEM((1, cwid), F32)],
        compiler_params=_cparams(("parallel", "arbitrary")),
        name="rglru",
    )(xr, y, conv_w, vec(conv_b), w_r, vec(b_r), w_i, vec(b_i), vec(lam))


def _router_kernel(h_ref, g_ref, wr_ref, un_ref, pos_ref, gate_ref, cnt_ref):
    tm = h_ref.shape[0]
    ne = wr_ref.shape[0]
    un = _rms(h_ref[...], g_ref[...])
    uh, ul = _split(un)
    un_ref[...] = uh
    wh, wl = _split(wr_ref[...])
    logits = _dot_nt(wh, uh) + _dot_nt(wh, ul) + _dot_nt(wl, uh)
    eidx = lax.broadcasted_iota(jnp.int32, (ne, tm), 0).astype(F32)
    m1 = jnp.max(logits, axis=0, keepdims=True)
    i1 = jnp.min(jnp.where(logits == m1, eidx, float(ne)), axis=0, keepdims=True)
    mask1 = eidx == i1
    rest = jnp.where(mask1, -jnp.inf, logits)
    m2 = jnp.max(rest, axis=0, keepdims=True)
    i2 = jnp.min(jnp.where(rest == m2, eidx, float(ne)), axis=0, keepdims=True)
    mask2 = eidx == i2
    e2 = jnp.exp(m2 - m1)
    g1 = 1.0 / (1.0 + e2)
    g2 = e2 / (1.0 + e2)
    gate_ref[...] = jnp.where(mask1, g1, jnp.where(mask2, g2, 0.0))
    sel = jnp.logical_or(mask1, mask2)
    self32 = jnp.where(sel, 1.0, 0.0)
    before = lax.broadcasted_iota(jnp.int32, (tm, tm), 0) < lax.broadcasted_iota(jnp.int32, (tm, tm), 1)
    rank = _dot(self32.astype(BF16), jnp.where(before, 1.0, 0.0).astype(BF16))
    pos_ref[...] = jnp.where(sel, rank, -1.0)
    cnt = jnp.sum(self32, axis=1, keepdims=True).astype(jnp.int32)
    cnt_ref[0] = jnp.broadcast_to(cnt, cnt_ref.shape[1:])


def router(h, g, wr_t, tm):
    t, d = h.shape
    tm = min(tm, t)
    ne = wr_t.shape[0]
    nt = t // tm
    return pl.pallas_call(
        _router_kernel,
        grid=(nt,),
        in_specs=[pl.BlockSpec((tm, d), lambda i: (i, 0)),
                  pl.BlockSpec((1, d), lambda i: (0, 0)),
                  pl.BlockSpec((ne, d), lambda i: (0, 0))],
        out_specs=[pl.BlockSpec((tm, d), lambda i: (i, 0)),
                   pl.BlockSpec((ne, tm), lambda i: (0, i)),
                   pl.BlockSpec((ne, tm), lambda i: (0, i)),
                   pl.BlockSpec((1, ne, 128), lambda i: (i, 0, 0))],
        out_shape=[jax.ShapeDtypeStruct((t, d), BF16), jax.ShapeDtypeStruct((ne, t), F32),
                   jax.ShapeDtypeStruct((ne, t), F32), jax.ShapeDtypeStruct((nt, ne, 128), jnp.int32)],
        compiler_params=_cparams(("parallel",)),
        name="moe_router",
    )(h, g.reshape(1, d), wr_t)


def _pick(pos, base, rows):
    slot = lax.broadcasted_iota(jnp.int32, (rows, pos.shape[1]), 0).astype(F32)
    return pos == slot + base.astype(F32)


def _moe_gather_kernel(ce_ref, cb_ref, co_ref, nq_ref, un_ref, pos_ref, xs_in_ref, xs_ref, buf_ref, sem, *, rows, qmax):
    del xs_in_ref
    i = pl.program_id(0)
    n = nq_ref[i]

    def copy(slot, off):
        return pltpu.make_async_copy(buf_ref.at[slot], xs_ref.at[pl.ds(off, rows)], sem.at[slot])

    def body(q, carry):
        k = i * qmax + q
        slot = lax.rem(q, 2)

        @pl.when(q >= 2)
        def _():
            copy(slot, 0).wait()

        pos = pos_ref[pl.ds(ce_ref[k], 1), :]
        sel = jnp.where(_pick(pos, cb_ref[k], rows), 1.0, 0.0).astype(BF16)
        buf_ref[slot] = _dot(sel, un_ref[...]).astype(BF16)
        copy(slot, pl.multiple_of(co_ref[k], MOE_GRANULE)).start()
        return carry

    lax.fori_loop(0, n, body, 0)

    @pl.when(n >= 2)
    def _():
        copy(lax.rem(n, 2), 0).wait()

    @pl.when(n >= 1)
    def _():
        copy(lax.rem(n + 1, 2), 0).wait()


def moe_gather(un, pos, tables, n_rows, tm, rows):
    t, d = un.shape
    ne = pos.shape[0]
    ce, cb, co, nq, qmax = tables
    grid_spec = pltpu.PrefetchScalarGridSpec(
        num_scalar_prefetch=4,
        grid=(t // tm,),
        in_specs=[pl.BlockSpec((tm, d), lambda i, *_: (i, 0)),
                  pl.BlockSpec((ne, tm), lambda i, *_: (0, i)),
                  pl.BlockSpec(memory_space=pl.ANY)],
        out_specs=pl.BlockSpec(memory_space=pl.ANY),
        scratch_shapes=[pltpu.VMEM((2, rows, d), BF16), pltpu.SemaphoreType.DMA((2,))],
    )
    return pl.pallas_call(
        functools.partial(_moe_gather_kernel, rows=rows, qmax=qmax),
        grid_spec=grid_spec,
        out_shape=jax.ShapeDtypeStruct((n_rows, d), BF16),
        input_output_aliases={6: 0},
        compiler_params=_cparams(("arbitrary",)),
        name="moe_gather",
    )(ce, cb, co, nq, un, pos, jnp.zeros((n_rows, d), BF16))


def _moe_ffn_kernel(be_ref, nv_ref, x_ref, wg_ref, wu_ref, wd_ref, o_ref, acc_ref):
    del be_ref
    b, f = pl.program_id(0), pl.program_id(1)
    nf = pl.num_programs(1)
    bm = x_ref.shape[0]
    nv = nv_ref[b]

    @pl.when(f == 0)
    def _():
        acc_ref[...] = jnp.zeros_like(acc_ref)

    def ffn(rows):
        x = x_ref[rows, :]
        hb = (_silu(_dot(x, wg_ref[0])) * _dot(x, wu_ref[0])).astype(BF16)
        acc_ref[rows, :] += _dot(hb, wd_ref[0])

    @pl.when(nv > bm - MOE_PART_ROWS)
    def _():
        ffn(pl.ds(0, bm))

    for part in range(bm // MOE_PART_ROWS):
        @pl.when(jnp.logical_and(nv <= bm - MOE_PART_ROWS, nv > part * MOE_PART_ROWS))
        def _():
            ffn(pl.ds(part * MOE_PART_ROWS, MOE_PART_ROWS))

    @pl.when(f == nf - 1)
    def _():
        o_ref[...] = acc_ref[...].astype(BF16)


def moe_ffn(xs, blk_e, blk_rows, wg, wu, wd, bm, tf=512):
    n_rows, d = xs.shape
    ff = wg.shape[2]
    nf = ff // tf

    def fidx(b, f, nv):
        return jnp.where(nv[b] > 0, f, nf - 1)

    grid_spec = pltpu.PrefetchScalarGridSpec(
        num_scalar_prefetch=2,
        grid=(n_rows // bm, nf),
        in_specs=[pl.BlockSpec((bm, d), lambda b, f, be, nv: (b, 0)),
                  pl.BlockSpec((1, d, tf), lambda b, f, be, nv: (be[b], 0, fidx(b, f, nv))),
                  pl.BlockSpec((1, d, tf), lambda b, f, be, nv: (be[b], 0, fidx(b, f, nv))),
                  pl.BlockSpec((1, tf, d), lambda b, f, be, nv: (be[b], fidx(b, f, nv), 0))],
        out_specs=pl.BlockSpec((bm, d), lambda b, f, be, nv: (b, 0)),
        scratch_shapes=[pltpu.VMEM((bm, d), F32)],
    )
    return pl.pallas_call(
        _moe_ffn_kernel,
        grid_spec=grid_spec,
        out_shape=jax.ShapeDtypeStruct((n_rows, d), BF16),
        compiler_params=_cparams(("parallel", "arbitrary")),
        name="moe_ffn",
    )(blk_e, blk_rows, xs, wg, wu, wd)


def _moe_combine_kernel(ce_ref, cb_ref, co_ref, nq_ref, pos_ref, gate_ref, y_ref, o_ref, buf_ref, sem, *, rows, qmax):
    i = pl.program_id(0)
    n = nq_ref[i]
    o_ref[...] = jnp.zeros_like(o_ref)

    def copy(slot, off):
        return pltpu.make_async_copy(y_ref.at[pl.ds(off, rows)], buf_ref.at[slot], sem.at[slot])

    def start(q):
        copy(lax.rem(q, 2), pl.multiple_of(co_ref[i * qmax + q], MOE_GRANULE)).start()

    @pl.when(n > 0)
    def _():
        start(0)

    def body(q, carry):
        k = i * qmax + q
        slot = lax.rem(q, 2)

        @pl.when(q + 1 < n)
        def _():
            start(q + 1)

        copy(slot, 0).wait()
        e = ce_ref[k]
        hit = _pick(pos_ref[pl.ds(e, 1), :], cb_ref[k], rows)
        gsub = jnp.sum(jnp.where(hit, gate_ref[pl.ds(e, 1), :], 0.0), axis=1, keepdims=True)
        yb = (buf_ref[slot].astype(F32) * gsub).astype(BF16)
        o_ref[...] += _dot_tn(jnp.where(hit, 1.0, 0.0).astype(BF16), yb)
        return carry

    lax.fori_loop(0, n, body, 0)


def moe_combine(ys, pos, gate, tables, tm, rows):
    ne, t = pos.shape
    d = ys.shape[1]
    ce, cb, co, nq, qmax = tables
    grid_spec = pltpu.PrefetchScalarGridSpec(
        num_scalar_prefetch=4,
        grid=(t // tm,),
        in_specs=[pl.BlockSpec((ne, tm), lambda i, *_: (0, i)),
                  pl.BlockSpec((ne, tm), lambda i, *_: (0, i)),
                  pl.BlockSpec(memory_space=pl.ANY)],
        out_specs=pl.BlockSpec((tm, d), lambda i, *_: (i, 0)),
        scratch_shapes=[pltpu.VMEM((2, rows, d), BF16), pltpu.SemaphoreType.DMA((2,))],
    )
    return pl.pallas_call(
        functools.partial(_moe_combine_kernel, rows=rows, qmax=qmax),
        grid_spec=grid_spec,
        out_shape=jax.ShapeDtypeStruct((t, d), F32),
        compiler_params=_cparams(("arbitrary",)),
        name="moe_combine",
    )(ce, cb, co, nq, pos, gate, ys)


def _chunk_tables(counts, seg, rows, qmax):
    ne = counts.shape[1]
    ns = (counts + rows - 1) // rows
    cs = jnp.cumsum(ns, axis=1)
    q = jnp.arange(qmax, dtype=jnp.int32)
    ce = jnp.minimum(jnp.sum(q[None, :, None] >= cs[:, None, :], axis=-1), ne - 1).astype(jnp.int32)
    cj = q[None, :] - jnp.take_along_axis(cs - ns, ce, axis=1)
    co = jnp.take_along_axis(seg, ce, axis=1) + cj * rows
    flat = lambda a: a.reshape(-1).astype(jnp.int32)
    return flat(ce), flat(cj * rows), flat(co), cs[:, -1].astype(jnp.int32), qmax


def moe(h, g, w_router, wg, wu, wd, tm=MOE_TILE, bm=MOE_BLOCK):
    t = h.shape[0]
    tm = min(tm, t)
    ne = wg.shape[0]
    nt = t // tm
    un, pos, gate, cnt = router(h, g, w_router.T, tm)
    counts = cnt[:, :, 0]
    padded = (counts + MOE_GRANULE - 1) // MOE_GRANULE * MOE_GRANULE
    tot = jnp.sum(padded, axis=0)
    ptot = (tot + MOE_SLACK + bm - 1) // bm * bm
    eend = jnp.cumsum(ptot)
    seg = (eend - ptot)[None, :] + jnp.cumsum(padded, axis=0) - padded
    n_blocks = (TOP_K * t + nt * ne * (MOE_GRANULE - 1) + ne * MOE_SLACK) // bm + ne
    blk_row0 = jnp.arange(n_blocks, dtype=jnp.int32) * bm
    blk_e = jnp.minimum(jnp.searchsorted(eend, blk_row0, side="right"), ne - 1).astype(jnp.int32)
    blk_rows = jnp.clip((eend - ptot + tot)[blk_e] - blk_row0, 0, bm).astype(jnp.int32)
    g_tab = _chunk_tables(counts, seg, MOE_GATHER_ROWS, TOP_K * tm // MOE_GATHER_ROWS + ne)
    c_tab = _chunk_tables(counts, seg, MOE_COMBINE_ROWS, TOP_K * tm // MOE_COMBINE_ROWS + ne)
    xs = moe_gather(un, pos, g_tab, n_blocks * bm, tm, MOE_GATHER_ROWS)
    ys = moe_ffn(xs, blk_e, blk_rows, wg, wu, wd, bm)
    return moe_combine(ys, pos, gate, c_tab, tm, MOE_COMBINE_ROWS)


def kernel(x, p, ln_mix, ln_ffn, ln_ple, ln_final, lb_table, ab_w_in, ab_conv, b_a_log, b_dt_bias, a_gnorm, b_gnorm, ab_w_out, c_w_in, c_conv_w, c_conv_b, c_w_r, c_b_r, c_w_i, c_b_i, c_lambda, c_w_out, ffn_w_gate, ffn_w_up, ffn_w_down, moe_router, moe_w_gate, moe_w_up, moe_w_down, ple_w_proj, ple_w_gate):
    bsz, seq, d = x.shape
    t = bsz * seq
    depth = ln_mix.shape[0]
    a_heads = lb_table.shape[1] // HEAD_DIM
    b_heads = b_a_log.shape[1]
    a_w = a_heads * HEAD_DIM
    b_w = b_heads * HEAD_DIM
    main_w = 4 * a_w + 4 * b_w
    bf = lambda a: a.astype(BF16)

    h = x.reshape(t, d)
    for layer in range(depth):
        j = layer // 2
        if layer % 2 == 0:
            w_in = ab_w_in[j]
            w_small = jnp.pad(w_in[:, main_w:], ((0, 0), (0, 128 - 2 * b_heads)))
            proj, small = norm_proj(h, ln_mix[layer], bf(w_in[:, :main_w]), bf(w_small))
            mixed = mixer_ab(proj, small, lb_table, a_gnorm[j], ab_conv[j], b_a_log[j], b_dt_bias[j], b_gnorm[j],
                             bsz=bsz, seq=seq, heads_a=a_heads, heads_b=b_heads, layer=layer)
            h = out_proj(mixed, bf(ab_w_out[j]), h)
            h = swiglu(h, ln_ffn[layer], bf(ffn_w_gate[j]), bf(ffn_w_up[j]), bf(ffn_w_down[j]))
            add = None
        else:
            xr, yb = norm_proj_gelu(h, ln_mix[layer], bf(c_w_in[j]))
            hy = rglru(xr, yb, c_conv_w[j], c_conv_b[j], bf(c_w_r[j]), c_b_r[j], bf(c_w_i[j]), c_b_i[j],
                       c_lambda[j], bsz=bsz, seq=seq)
            h = out_proj(hy, bf(c_w_out[j]), h)
            add = moe(h, ln_ffn[layer], moe_router[j], bf(moe_w_gate[j]), bf(moe_w_up[j]), bf(moe_w_down[j]))
        g_final = ln_final if layer == depth - 1 else None
        h = ple(h, p[layer].reshape(t, -1), ln_ple[layer], bf(ple_w_gate[layer]), bf(ple_w_proj[layer]),
                add=add, g_final=g_final)
    if depth == 0:
        raise ValueError("depth must be positive")
    return h.reshape(bsz, seq, d)
```

```python
import functools

import jax
import jax.numpy as jnp
from jax import lax
from jax.experimental import pallas as pl
from jax.experimental.pallas import tpu as pltpu

F32 = jnp.float32
BF16 = jnp.bfloat16
EPS = 1e-6
CHUNK = 64
SUB = 8
HEAD_DIM = 128
CONV_WIDTH = 4
GDN_HEAD_GROUP = 8
MIXER_ROWS = 128
FILL_AFTER_PREP = 4
FILL_AFTER_SCORES = 4
FILL_PER_LEVEL = 1
CONV_TAIL = 8
RGLRU_C = 8.0
SCAN_GROUP = 8
TOP_K = 2
MOE_TILE = 512
MOE_BLOCK = 1024
MOE_PART_ROWS = 256
MOE_GRANULE = 16
MOE_GATHER_ROWS = 128
MOE_COMBINE_ROWS = 256
MOE_SLACK = 256
VMEM_LIMIT = 56 * 1024 * 1024


def _cparams(sem, vmem=VMEM_LIMIT):
    return pltpu.CompilerParams(dimension_semantics=sem, vmem_limit_bytes=vmem)


def _dot(a, b):
    return jnp.dot(a, b, preferred_element_type=F32)


def _dot_nt(a, b):
    return lax.dot_general(a, b, (((1,), (1,)), ((), ())), preferred_element_type=F32)


def _dot_tn(a, b):
    return lax.dot_general(a, b, (((0,), (0,)), ((), ())), preferred_element_type=F32)


def _split(a):
    hi = a.astype(BF16)
    lo = (a - hi.astype(F32)).astype(BF16)
    return hi, lo


def _dot2(a, b):
    ah, al = _split(a)
    bh = b.astype(BF16)
    return _dot(jnp.concatenate([ah, al], axis=1), jnp.concatenate([bh, bh], axis=0))


def _rms(x, g):
    return x * lax.rsqrt(jnp.mean(x * x, axis=-1, keepdims=True) + EPS) * g


def _sigmoid(x):
    return 1.0 / (1.0 + jnp.exp(-x))


def _silu(x):
    return x * _sigmoid(x)


def _norm_proj_kernel(h_ref, g_ref, w_ref, ws_ref, o_ref, os_ref, un_ref):
    j = pl.program_id(1)

    @pl.when(j == 0)
    def _():
        un = _rms(h_ref[...], g_ref[...]).astype(BF16)
        un_ref[...] = un
        os_ref[...] = _dot(un, ws_ref[...])

    o_ref[...] = _dot(un_ref[...], w_ref[...])


def norm_proj(h, g, w, ws, tm=1024, tn=1024):
    t, d = h.shape
    tm = min(tm, t)
    n = w.shape[1]
    return pl.pallas_call(
        _norm_proj_kernel,
        grid=(t // tm, n // tn),
        in_specs=[pl.BlockSpec((tm, d), lambda i, j: (i, 0)),
                  pl.BlockSpec((1, d), lambda i, j: (0, 0)),
                  pl.BlockSpec((d, tn), lambda i, j: (0, j)),
                  pl.BlockSpec((d, ws.shape[1]), lambda i, j: (0, 0))],
        out_specs=[pl.BlockSpec((tm, tn), lambda i, j: (i, j)),
                   pl.BlockSpec((tm, ws.shape[1]), lambda i, j: (i, 0))],
        out_shape=[jax.ShapeDtypeStruct((t, n), F32), jax.ShapeDtypeStruct((t, ws.shape[1]), F32)],
        scratch_shapes=[pltpu.VMEM((tm, d), BF16)],
        compiler_params=_cparams(("parallel", "arbitrary")),
        name="norm_proj",
    )(h, g.reshape(1, d), w, ws)


def _norm_proj_gelu_kernel(h_ref, g_ref, wy_ref, wx_ref, o_ref, y_ref, un_ref):
    @pl.when(pl.program_id(1) == 0)
    def _():
        un_ref[...] = _rms(h_ref[...], g_ref[...]).astype(BF16)

    un = un_ref[...]
    y_ref[...] = jax.nn.gelu(_dot(un, wy_ref[...])).astype(BF16)
    o_ref[...] = _dot(un, wx_ref[...])


def norm_proj_gelu(h, g, w, tm=1024, tn=512):
    t, d = h.shape
    tm = min(tm, t)
    half = w.shape[1] // 2
    nh = half // tn
    return pl.pallas_call(
        _norm_proj_gelu_kernel,
        grid=(t // tm, nh),
        in_specs=[pl.BlockSpec((tm, d), lambda i, j: (i, 0)),
                  pl.BlockSpec((1, d), lambda i, j: (0, 0)),
                  pl.BlockSpec((d, tn), lambda i, j: (0, j)),
                  pl.BlockSpec((d, tn), lambda i, j: (0, nh + j))],
        out_specs=[pl.BlockSpec((tm, tn), lambda i, j: (i, j)),
                   pl.BlockSpec((tm, tn), lambda i, j: (i, j))],
        out_shape=[jax.ShapeDtypeStruct((t, half), F32), jax.ShapeDtypeStruct((t, half), BF16)],
        scratch_shapes=[pltpu.VMEM((tm, d), BF16)],
        compiler_params=_cparams(("parallel", "arbitrary")),
        name="norm_proj_gelu",
    )(h, g.reshape(1, d), w, w)


def _hgrn2_work(q_ref, f_ref, i_ref, g_ref, lbt_ref, gn_ref, o_ref, st_ref, *, layer, heads):
    c = CHUNK

    lbt = lbt_ref[...]
    e = jnp.exp(lbt - jnp.max(lbt, axis=0, keepdims=True))
    lb_all = jnp.sum(e[:layer + 1], axis=0, keepdims=True) / jnp.sum(e, axis=0, keepdims=True)

    row = lax.broadcasted_iota(jnp.int32, (c, c), 0)
    col = lax.broadcasted_iota(jnp.int32, (c, c), 1)
    tril = jnp.where(row >= col, 1.0, 0.0).astype(BF16)
    gn = gn_ref[...]

    levels = []
    ln = c // 2
    while ln >= SUB:
        levels += [(m * 2 * ln, m * 2 * ln + ln, ln) for m in range(c // (2 * ln))]
        ln //= 2
    n_pairs = sum(l[2] for l in levels)

    def seg_id(idx):
        sid = jnp.zeros_like(idx)
        start = 0
        for l in levels[:-1]:
            start += l[2]
            sid = sid + jnp.where(idx >= start, 1, 0)
        return sid

    same_seg = (seg_id(lax.broadcasted_iota(jnp.int32, (n_pairs, n_pairs), 0))
                == seg_id(lax.broadcasted_iota(jnp.int32, (n_pairs, n_pairs), 1)))
    sub_i = lax.broadcasted_iota(jnp.int32, (c // SUB, SUB, HEAD_DIM), 1)

    sls = [slice(h * HEAD_DIM, (h + 1) * HEAD_DIM) for h in range(heads)]

    def gates(h):
        lb = lb_all[:, sls[h]]
        q = q_ref[:, sls[h]] * (HEAD_DIM ** -0.5)
        forget = lb + (1.0 - lb) * _sigmoid(f_ref[:, sls[h]])
        lh, ll = _split(jnp.log(forget))
        b2 = _dot(tril, jnp.concatenate([lh, ll], axis=1))
        return q, 1.0 - forget, i_ref[:, sls[h]], b2[:, :HEAD_DIM] + b2[:, HEAD_DIM:]

    def block_pairs(h, q, k, v, b):
        o = _dot_nt((q * jnp.exp(b)).astype(BF16), st_ref[h].astype(BF16))
        qs, ks, vs = [], [], []
        for k0, q0, ln in levels:
            bref = b[q0 - 1:q0, :]
            qs.append(q[q0:q0 + ln] * jnp.exp(b[q0:q0 + ln] - bref))
            ks.append(k[k0:k0 + ln] * jnp.exp(bref - b[k0:k0 + ln]))
            vs.append(v[k0:k0 + ln])
        s = _dot_nt(jnp.concatenate(qs, axis=0).astype(BF16), jnp.concatenate(ks, axis=0).astype(BF16))
        r = _dot(jnp.where(same_seg, s, 0.0).astype(BF16), jnp.concatenate(vs, axis=0).astype(BF16))
        groups = [None] * (c // SUB)
        start = 0
        for k0, q0, ln in levels:
            for j in range(ln // SUB):
                piece = r[start + j * SUB:start + (j + 1) * SUB]
                gi = q0 // SUB + j
                groups[gi] = piece if groups[gi] is None else groups[gi] + piece
            start += ln
        groups[0] = jnp.zeros((SUB, HEAD_DIM), F32)
        return o + jnp.concatenate(groups, axis=0)

    def near_pairs(q, k, v, b):
        q3, k3, v3, b3 = (a.reshape(c // SUB, SUB, HEAD_DIM) for a in (q, k, v, b))
        o3 = jnp.sum(q3 * k3, axis=2, keepdims=True) * v3
        for d in range(1, SUB):
            dec = jnp.exp(jnp.where(sub_i >= d, b3 - pltpu.roll(b3, d, 1), -jnp.inf))
            w = jnp.sum(q3 * pltpu.roll(k3, d, 1) * dec, axis=2, keepdims=True)
            o3 = o3 + w * pltpu.roll(v3, d, 1)
        return o3.reshape(c, HEAD_DIM)

    qkvb, far = {}, {}

    def state_part(h):
        qkvb[h] = gates(h)
        q, k, v, b = qkvb[h]
        far[h] = block_pairs(h, q, k, v, b)
        blast = b[c - 1:c, :]
        kd = (k * jnp.exp(blast - b)).astype(BF16)
        st_ref[h] = st_ref[h] * jnp.exp(blast) + _dot_tn(v.astype(BF16), kd)

    def block_part(h):
        o = far[h] + near_pairs(*qkvb[h])
        on = o * lax.rsqrt(jnp.mean(o * o, axis=1, keepdims=True) + EPS) * gn
        o_ref[:, sls[h]] = (on * _silu(g_ref[:, sls[h]])).astype(BF16)

    return ([functools.partial(state_part, h) for h in range(heads)]
            + [functools.partial(block_part, h) for h in range(heads)])


def _causal_conv(ext_ref, x, w, first):
    n = x.shape[0]

    @pl.when(first)
    def _():
        ext_ref[0:CONV_TAIL, :] = jnp.zeros((CONV_TAIL, x.shape[1]), F32)

    ext_ref[CONV_TAIL:CONV_TAIL + n, :] = x
    y = x * w[CONV_WIDTH - 1:CONV_WIDTH, :]
    for k in range(CONV_WIDTH - 1):
        off = CONV_TAIL - (CONV_WIDTH - 1) + k
        y = y + ext_ref[off:off + n, :] * w[k:k + 1, :]
    ext_ref[0:CONV_TAIL, :] = ext_ref[n:n + CONV_TAIL, :]
    return y


def _mixer_ab_kernel(qa_ref, fa_ref, ia_ref, ga_ref, q_ref, k_ref, v_ref, z_ref, sm_ref, lbt_ref, gna_ref, cw_ref,
                     alog_ref, dtb_ref, gn_ref, o_ref, sta_ref, st_ref, eq_ref, ek_ref, ev_ref,
                     *, heads_a, heads, layer):
    c = CHUNK
    first = pl.program_id(1) == 0

    @pl.when(first)
    def _():
        sta_ref[...] = jnp.zeros_like(sta_ref)
        st_ref[...] = jnp.zeros_like(st_ref)

    col0 = heads_a * HEAD_DIM
    hw = heads * HEAD_DIM
    cw = cw_ref[...]
    qc_all = _silu(_causal_conv(eq_ref, q_ref[...], cw[:, 0:hw], first))
    kc_all = _silu(_causal_conv(ek_ref, k_ref[...], cw[:, hw:2 * hw], first))
    vc_all = _silu(_causal_conv(ev_ref, v_ref[...], cw[:, 2 * hw:3 * hw], first))

    row = lax.broadcasted_iota(jnp.int32, (c, c), 0)
    col = lax.broadcasted_iota(jnp.int32, (c, c), 1)
    causal = row >= col
    strict = row > col
    eye = row == col
    eye_f = jnp.where(eye, 1.0, 0.0)
    diag_blk = (row // SUB) == (col // SUB)
    merge_masks = []
    s = SUB
    while s < c:
        merge_masks.append(jnp.logical_and((row // (2 * s)) == (col // (2 * s)), (row // s) == (col // s) + 1))
        s *= 2
    gn = gn_ref[...]

    def to_row(colv):
        return jnp.sum(jnp.where(eye, colv, 0.0), axis=0, keepdims=True)

    def head_group(hg, qc, kc, vc, sm, z_ref, o_ref, fill):
        hs = range(len(hg))
        sls = [slice(h * HEAD_DIM, (h + 1) * HEAD_DIM) for h in hg]
        qn, kn, knb, beta, gam_col, egam, decay, kbeta = [], [], [], [], [], [], [], []
        for i, h in enumerate(hg):
            qh, kh = qc[:, sls[i]], kc[:, sls[i]]
            qn.append(qh * lax.rsqrt(jnp.sum(qh * qh, axis=1, keepdims=True) + EPS) * (HEAD_DIM ** -0.5))
            kn.append(kh * lax.rsqrt(jnp.sum(kh * kh, axis=1, keepdims=True) + EPS))
            knb.append(kn[i].astype(BF16))
            beta.append(_sigmoid(sm[:, heads + h:heads + h + 1]))
            g_col = -jnp.exp(alog_ref[:, h:h + 1]) * jax.nn.softplus(sm[:, h:h + 1] + dtb_ref[:, h:h + 1])
            g_row = to_row(g_col)
            gam_col.append(jnp.sum(jnp.where(causal, g_row, 0.0), axis=1, keepdims=True))
            gam_row = jnp.sum(jnp.where(strict, 0.0, g_col), axis=0, keepdims=True)
            decay.append(jnp.exp(jnp.where(causal, gam_col[i] - gam_row, -jnp.inf)))
            egam.append(jnp.exp(gam_col[i]))
            kbeta.append(kn[i] * beta[i])

        fill(FILL_AFTER_PREP)
        a_mat = [jnp.where(strict, _dot_nt(kbeta[i].astype(BF16), knb[i]) * decay[i], 0.0) for i in hs]
        qk = [(_dot_nt(qn[i].astype(BF16), knb[i]) * decay[i]).astype(BF16) for i in hs]
        fill(FILL_AFTER_SCORES)
        d_mat = [jnp.where(diag_blk, a_mat[i], 0.0) for i in hs]
        x = [eye_f - d_mat[i] for i in hs]
        p = [_dot2(d_mat[i], d_mat[i]) for i in hs]
        fill(FILL_PER_LEVEL)
        n_sq = (SUB - 1).bit_length() - 1
        for lvl in range(n_sq):
            if lvl < n_sq - 1:
                y = [_dot2(jnp.concatenate([x[i], p[i]], axis=0), p[i]) for i in hs]
                fill(FILL_PER_LEVEL)
                x = [x[i] + y[i][:c] for i in hs]
                p = [y[i][c:] for i in hs]
            else:
                y = [_dot2(x[i], p[i]) for i in hs]
                fill(FILL_PER_LEVEL)
                x = [x[i] + y[i] for i in hs]
        for below in merge_masks:
            y = [_dot2(x[i], jnp.where(below, a_mat[i], 0.0)) for i in hs]
            fill(FILL_PER_LEVEL)
            y = [_dot2(y[i], x[i]) for i in hs]
            fill(FILL_PER_LEVEL)
            x = [x[i] - y[i] for i in hs]
        rhs = [jnp.concatenate([vc[:, sls[i]] * beta[i], kbeta[i] * egam[i]], axis=1).astype(BF16) for i in hs]
        uw = [_dot(x[i].astype(BF16), rhs[i]) for i in hs]

        st = [st_ref[h] for h in hg]
        stb = [s.astype(BF16) for s in st]
        v_new = [uw[i][:, :HEAD_DIM] - _dot_nt(uw[i][:, HEAD_DIM:].astype(BF16), stb[i]) for i in hs]
        vnb = [v.astype(BF16) for v in v_new]
        o = [_dot_nt((qn[i] * egam[i]).astype(BF16), stb[i]) + _dot(qk[i], vnb[i]) for i in hs]
        for i, h in enumerate(hg):
            glast = gam_col[i][c - 1:c, :]
            kd = (kn[i] * jnp.exp(glast - gam_col[i])).astype(BF16)
            st_ref[h] = st[i] * jnp.exp(glast) + _dot_tn(vnb[i], kd)
        for i in hs:
            on = o[i] * lax.rsqrt(jnp.mean(o[i] * o[i], axis=1, keepdims=True) + EPS) * gn
            o_ref[:, col0 + hg[i] * HEAD_DIM:col0 + (hg[i] + 1) * HEAD_DIM] = (on * _silu(z_ref[:, sls[i]])).astype(BF16)

    for ci in range(q_ref.shape[0] // c):
        rs = pl.ds(ci * c, c)
        pending = _hgrn2_work(qa_ref.at[rs], fa_ref.at[rs], ia_ref.at[rs], ga_ref.at[rs], lbt_ref, gna_ref,
                              o_ref.at[rs], sta_ref, layer=layer, heads=heads_a)

        def fill(n, pending=pending):
            for _ in range(min(n, len(pending))):
                pending.pop(0)()

        r0 = ci * c
        for h0 in range(0, heads, GDN_HEAD_GROUP):
            head_group(list(range(h0, min(h0 + GDN_HEAD_GROUP, heads))), qc_all[r0:r0 + c], kc_all[r0:r0 + c],
                       vc_all[r0:r0 + c], sm_ref[rs, :], z_ref.at[rs], o_ref.at[rs], fill)
        fill(len(pending))


def mixer_ab(proj, small, lb_table, gnorm_a, conv_w, a_log, dt_bias, gnorm_b, *, bsz, seq, heads_a, heads_b, layer):
    t = proj.shape[0]
    wa, wb = heads_a * HEAD_DIM, heads_b * HEAD_DIM
    rows = min(MIXER_ROWS, seq)
    nc = seq // rows
    row = lambda b, s: b * nc + s

    def spec_a(k):
        return pl.BlockSpec((rows, wa), lambda b, s: (row(b, s), k))

    def spec_b(k):
        return pl.BlockSpec((rows, wb), lambda b, s: (row(b, s), 4 * wa // wb + k))

    def const(shape):
        return pl.BlockSpec(shape, lambda b, s: (0, 0))

    return pl.pallas_call(
        functools.partial(_mixer_ab_kernel, heads_a=heads_a, heads=heads_b, layer=layer),
        grid=(bsz, nc),
        in_specs=[spec_a(0), spec_a(1), spec_a(2), spec_a(3), spec_b(0), spec_b(1), spec_b(2), spec_b(3),
                  pl.BlockSpec((rows, small.shape[1]), lambda b, s: (row(b, s), 0)),
                  const(lb_table.shape), const((1, HEAD_DIM)),
                  const(conv_w.shape), const((1, heads_b)), const((1, heads_b)), const((1, HEAD_DIM))],
        out_specs=pl.BlockSpec((rows, wa + wb), lambda b, s: (row(b, s), 0)),
        out_shape=jax.ShapeDtypeStruct((t, wa + wb), BF16),
        scratch_shapes=[pltpu.VMEM((heads_a, HEAD_DIM, HEAD_DIM), F32), pltpu.VMEM((heads_b, HEAD_DIM, HEAD_DIM), F32)]
        + [pltpu.VMEM((rows + CONV_TAIL, wb), F32)] * 3,
        compiler_params=_cparams(("parallel", "arbitrary")),
        name="mixer_ab",
    )(proj, proj, proj, proj, proj, proj, proj, proj, small, lb_table, gnorm_a.reshape(1, HEAD_DIM), conv_w,
      a_log.reshape(1, heads_b), dt_bias.reshape(1, heads_b), gnorm_b.reshape(1, HEAD_DIM))


def _out_proj_kernel(a_ref, w_ref, h_ref, o_ref):
    o_ref[...] = h_ref[...] + _dot(a_ref[...], w_ref[...])


def out_proj(a, w, h, tm=1024, tn=1024):
    t, k = a.shape
    tm = min(tm, t)
    n = w.shape[1]
    return pl.pallas_call(
        _out_proj_kernel,
        grid=(t // tm, n // tn),
        in_specs=[pl.BlockSpec((tm, k), lambda i, j: (i, 0)),
                  pl.BlockSpec((k, tn), lambda i, j: (0, j)),
                  pl.BlockSpec((tm, tn), lambda i, j: (i, j))],
        out_specs=pl.BlockSpec((tm, tn), lambda i, j: (i, j)),
        out_shape=jax.ShapeDtypeStruct((t, n), F32),
        compiler_params=_cparams(("parallel", "arbitrary")),
        name="out_proj",
    )(a, w, h)


def _swiglu_kernel(h_ref, g_ref, wg_ref, wu_ref, wd_ref, o_ref, un_ref):
    f = pl.program_id(1)

    @pl.when(f == 0)
    def _():
        un_ref[...] = _rms(h_ref[...], g_ref[...]).astype(BF16)
        o_ref[...] = h_ref[...]

    un = un_ref[...]
    hb = (_silu(_dot(un, wg_ref[...])) * _dot(un, wu_ref[...])).astype(BF16)
    o_ref[...] += _dot(hb, wd_ref[...])


def swiglu(h, g, wg, wu, wd, tm=1024, tf=512):
    t, d = h.shape
    tm = min(tm, t)
    ff = wg.shape[1]
    return pl.pallas_call(
        _swiglu_kernel,
        grid=(t // tm, ff // tf),
        in_specs=[pl.BlockSpec((tm, d), lambda i, f: (i, 0), pipeline_mode=pl.Buffered(1)),
                  pl.BlockSpec((1, d), lambda i, f: (0, 0)),
                  pl.BlockSpec((d, tf), lambda i, f: (0, f)),
                  pl.BlockSpec((d, tf), lambda i, f: (0, f)),
                  pl.BlockSpec((tf, d), lambda i, f: (f, 0))],
        out_specs=pl.BlockSpec((tm, d), lambda i, f: (i, 0)),
        out_shape=jax.ShapeDtypeStruct((t, d), F32),
        scratch_shapes=[pltpu.VMEM((tm, d), BF16)],
        compiler_params=_cparams(("parallel", "arbitrary")),
        name="swiglu",
    )(h, g.reshape(1, d), wg, wu, wd)


def _ple_kernel(*refs, has_add, has_final):
    h_ref, p_ref, g_ref, wg_ref, wp_ref = refs[:5]
    k = 5
    add_ref = gf_ref = None
    if has_add:
        add_ref = refs[k]
        k += 1
    if has_final:
        gf_ref = refs[k]
        k += 1
    o_ref = refs[k]
    h = h_ref[...]
    if has_add:
        h = h + add_ref[...]
    un = _rms(h, g_ref[...]).astype(BF16)
    gate = _sigmoid(_dot(un, wg_ref[...]))
    out = h + gate * _dot(p_ref[...].astype(BF16), wp_ref[...])
    if has_final:
        out = _rms(out, gf_ref[...])
    o_ref[...] = out


def ple(h, p, g, wg, wp, add=None, g_final=None, tm=512):
    t, d = h.shape
    tm = min(tm, t)
    pd = p.shape[1]
    row = lambda i: (i, 0)
    const = lambda i: (0, 0)
    in_specs = [pl.BlockSpec((tm, d), row), pl.BlockSpec((tm, pd), row), pl.BlockSpec((1, d), const),
                pl.BlockSpec((d, d), const), pl.BlockSpec((pd, d), const)]
    args = [h, p, g.reshape(1, d), wg, wp]
    if add is not None:
        in_specs.append(pl.BlockSpec((tm, d), row))
        args.append(add)
    if g_final is not None:
        in_specs.append(pl.BlockSpec((1, d), const))
        args.append(g_final.reshape(1, d))
    return pl.pallas_call(
        functools.partial(_ple_kernel, has_add=add is not None, has_final=g_final is not None),
        grid=(t // tm,),
        in_specs=in_specs,
        out_specs=pl.BlockSpec((tm, d), row),
        out_shape=jax.ShapeDtypeStruct((t, d), F32),
        compiler_params=_cparams(("parallel",)),
        name="ple",
    )(*args)


def _rglru_kernel(x_ref, y_ref, cw_ref, cb_ref, wr_ref, br_ref, wi_ref, bi_ref, lam_ref, o_ref,
                  ext_ref, hc_ref, *, blocks):
    n = x_ref.shape[0]
    first = pl.program_id(1) == 0

    @pl.when(first)
    def _():
        hc_ref[...] = jnp.zeros_like(hc_ref)

    xc = _causal_conv(ext_ref, x_ref[...], cw_ref[...], first) + cb_ref[...]
    bw = xc.shape[1] // blocks
    rowi = lax.broadcasted_iota(jnp.int32, (n, bw), 0)
    at_start = jnp.logical_and(first, rowi == 0)
    gidx = lax.broadcasted_iota(jnp.int32, (n // SCAN_GROUP, SCAN_GROUP, bw), 1)

    for nb in range(blocks):
        sl = slice(nb * bw, (nb + 1) * bw)
        xb = xc[:, sl]
        xbb = xb.astype(BF16)
        r = _sigmoid(_dot(xbb, wr_ref[nb]) + br_ref[:, sl])
        gi = _sigmoid(_dot(xbb, wi_ref[nb]) + bi_ref[:, sl])
        log_a = -RGLRU_C * r * jax.nn.softplus(-lam_ref[:, sl])
        a = jnp.exp(log_a)
        m2 = 1.0 - a * a
        mult = jnp.where(m2 > 0.0, m2 * lax.rsqrt(m2), 0.0)
        mult = jnp.where(at_start, 1.0, mult)
        b = mult * gi * xb
        a = a.reshape(n // SCAN_GROUP, SCAN_GROUP, bw)
        b = b.reshape(n // SCAN_GROUP, SCAN_GROUP, bw)
        sh = 1
        while sh < SCAN_GROUP:
            ok = gidx >= sh
            a_prev = jnp.where(ok, pltpu.roll(a, sh, 1), 1.0)
            b_prev = jnp.where(ok, pltpu.roll(b, sh, 1), 0.0)
            b = b + a * b_prev
            a = a * a_prev
            sh *= 2
        carry = hc_ref[:, sl]
        groups = []
        for gi_ in range(n // SCAN_GROUP):
            hg = b[gi_] + a[gi_] * carry
            groups.append(hg)
            carry = hg[SCAN_GROUP - 1:SCAN_GROUP, :]
        hseq = jnp.concatenate(groups, axis=0)
        hc_ref[:, sl] = carry
        o_ref[:, sl] = (hseq * y_ref[:, sl].astype(F32)).astype(BF16)


def rglru(xr, y, conv_w, conv_b, w_r, b_r, w_i, b_i, lam, *, bsz, seq, rows=256):
    t, cwid = xr.shape
    blocks = w_r.shape[0]
    ns = seq // rows
    row = lambda b, s: (b * ns + s, 0)
    c2 = lambda b, s: (0, 0)
    c3 = lambda b, s: (0, 0, 0)
    vec = lambda a: a.reshape(1, cwid)
    return pl.pallas_call(
        functools.partial(_rglru_kernel, blocks=blocks),
        grid=(bsz, ns),
        in_specs=[pl.BlockSpec((rows, cwid), row), pl.BlockSpec((rows, cwid), row),
                  pl.BlockSpec(conv_w.shape, c2), pl.BlockSpec((1, cwid), c2),
                  pl.BlockSpec(w_r.shape, c3), pl.BlockSpec((1, cwid), c2),
                  pl.BlockSpec(w_i.shape, c3), pl.BlockSpec((1, cwid), c2),
                  pl.BlockSpec((1, cwid), c2)],
        out_specs=pl.BlockSpec((rows, cwid), row),
        out_shape=jax.ShapeDtypeStruct((t, cwid), BF16),
        scratch_shapes=[pltpu.VMEM((rows + CONV_TAIL, cwid), F32), pltpu.VMEM((1, cwid), F32)],
        compiler_params=_cparams(("parallel", "arbitrary")),
        name="rglru",
    )(xr, y, conv_w, vec(conv_b), w_r, vec(b_r), w_i, vec(b_i), vec(lam))


def _router_kernel(h_ref, g_ref, wr_ref, un_ref, pos_ref, gate_ref, cnt_ref):
    tm = h_ref.shape[0]
    ne = wr_ref.shape[0]
    un = _rms(h_ref[...], g_ref[...])
    uh, ul = _split(un)
    un_ref[...] = uh
    wh, wl = _split(wr_ref[...])
    logits = _dot_nt(wh, uh) + _dot_nt(wh, ul) + _dot_nt(wl, uh)
    eidx = lax.broadcasted_iota(jnp.int32, (ne, tm), 0).astype(F32)
    m1 = jnp.max(logits, axis=0, keepdims=True)
    i1 = jnp.min(jnp.where(logits == m1, eidx, float(ne)), axis=0, keepdims=True)
    mask1 = eidx == i1
    rest = jnp.where(mask1, -jnp.inf, logits)
    m2 = jnp.max(rest, axis=0, keepdims=True)
    i2 = jnp.min(jnp.where(rest == m2, eidx, float(ne)), axis=0, keepdims=True)
    mask2 = eidx == i2
    e2 = jnp.exp(m2 - m1)
    g1 = 1.0 / (1.0 + e2)
    g2 = e2 / (1.0 + e2)
    gate_ref[...] = jnp.where(mask1, g1, jnp.where(mask2, g2, 0.0))
    sel = jnp.logical_or(mask1, mask2)
    self32 = jnp.where(sel, 1.0, 0.0)
    before = lax.broadcasted_iota(jnp.int32, (tm, tm), 0) < lax.broadcasted_iota(jnp.int32, (tm, tm), 1)
    rank = _dot(self32.astype(BF16), jnp.where(before, 1.0, 0.0).astype(BF16))
    pos_ref[...] = jnp.where(sel, rank, -1.0)
    cnt = jnp.sum(self32, axis=1, keepdims=True).astype(jnp.int32)
    cnt_ref[0] = jnp.broadcast_to(cnt, cnt_ref.shape[1:])


def router(h, g, wr_t, tm):
    t, d = h.shape
    tm = min(tm, t)
    ne = wr_t.shape[0]
    nt = t // tm
    return pl.pallas_call(
        _router_kernel,
        grid=(nt,),
        in_specs=[pl.BlockSpec((tm, d), lambda i: (i, 0)),
                  pl.BlockSpec((1, d), lambda i: (0, 0)),
                  pl.BlockSpec((ne, d), lambda i: (0, 0))],
        out_specs=[pl.BlockSpec((tm, d), lambda i: (i, 0)),
                   pl.BlockSpec((ne, tm), lambda i: (0, i)),
                   pl.BlockSpec((ne, tm), lambda i: (0, i)),
                   pl.BlockSpec((1, ne, 128), lambda i: (i, 0, 0))],
        out_shape=[jax.ShapeDtypeStruct((t, d), BF16), jax.ShapeDtypeStruct((ne, t), F32),
                   jax.ShapeDtypeStruct((ne, t), F32), jax.ShapeDtypeStruct((nt, ne, 128), jnp.int32)],
        compiler_params=_cparams(("parallel",)),
        name="moe_router",
    )(h, g.reshape(1, d), wr_t)


def _pick(pos, base, rows):
    slot = lax.broadcasted_iota(jnp.int32, (rows, pos.shape[1]), 0).astype(F32)
    return pos == slot + base.astype(F32)


def _moe_gather_kernel(ce_ref, cb_ref, co_ref, nq_ref, un_ref, pos_ref, xs_in_ref, xs_ref, buf_ref, sem, *, rows, qmax):
    del xs_in_ref
    i = pl.program_id(0)
    n = nq_ref[i]

    def copy(slot, off):
        return pltpu.make_async_copy(buf_ref.at[slot], xs_ref.at[pl.ds(off, rows)], sem.at[slot])

    def body(q, carry):
        k = i * qmax + q
        slot = lax.rem(q, 2)

        @pl.when(q >= 2)
        def _():
            copy(slot, 0).wait()

        pos = pos_ref[pl.ds(ce_ref[k], 1), :]
        sel = jnp.where(_pick(pos, cb_ref[k], rows), 1.0, 0.0).astype(BF16)
        buf_ref[slot] = _dot(sel, un_ref[...]).astype(BF16)
        copy(slot, pl.multiple_of(co_ref[k], MOE_GRANULE)).start()
        return carry

    lax.fori_loop(0, n, body, 0)

    @pl.when(n >= 2)
    def _():
        copy(lax.rem(n, 2), 0).wait()

    @pl.when(n >= 1)
    def _():
        copy(lax.rem(n + 1, 2), 0).wait()


def moe_gather(un, pos, tables, n_rows, tm, rows):
    t, d = un.shape
    ne = pos.shape[0]
    ce, cb, co, nq, qmax = tables
    grid_spec = pltpu.PrefetchScalarGridSpec(
        num_scalar_prefetch=4,
        grid=(t // tm,),
        in_specs=[pl.BlockSpec((tm, d), lambda i, *_: (i, 0)),
                  pl.BlockSpec((ne, tm), lambda i, *_: (0, i)),
                  pl.BlockSpec(memory_space=pl.ANY)],
        out_specs=pl.BlockSpec(memory_space=pl.ANY),
        scratch_shapes=[pltpu.VMEM((2, rows, d), BF16), pltpu.SemaphoreType.DMA((2,))],
    )
    return pl.pallas_call(
        functools.partial(_moe_gather_kernel, rows=rows, qmax=qmax),
        grid_spec=grid_spec,
        out_shape=jax.ShapeDtypeStruct((n_rows, d), BF16),
        input_output_aliases={6: 0},
        compiler_params=_cparams(("arbitrary",)),
        name="moe_gather",
    )(ce, cb, co, nq, un, pos, jnp.zeros((n_rows, d), BF16))


def _moe_ffn_kernel(be_ref, nv_ref, x_ref, wg_ref, wu_ref, wd_ref, o_ref, acc_ref):
    del be_ref
    b, f = pl.program_id(0), pl.program_id(1)
    nf = pl.num_programs(1)
    bm = x_ref.shape[0]
    nv = nv_ref[b]

    @pl.when(f == 0)
    def _():
        acc_ref[...] = jnp.zeros_like(acc_ref)

    def ffn(rows):
        x = x_ref[rows, :]
        hb = (_silu(_dot(x, wg_ref[0])) * _dot(x, wu_ref[0])).astype(BF16)
        acc_ref[rows, :] += _dot(hb, wd_ref[0])

    @pl.when(nv > bm - MOE_PART_ROWS)
    def _():
        ffn(pl.ds(0, bm))

    for part in range(bm // MOE_PART_ROWS):
        @pl.when(jnp.logical_and(nv <= bm - MOE_PART_ROWS, nv > part * MOE_PART_ROWS))
        def _():
            ffn(pl.ds(part * MOE_PART_ROWS, MOE_PART_ROWS))

    @pl.when(f == nf - 1)
    def _():
        o_ref[...] = acc_ref[...].astype(BF16)


def moe_ffn(xs, blk_e, blk_rows, wg, wu, wd, bm, tf=512):
    n_rows, d = xs.shape
    ff = wg.shape[2]
    nf = ff // tf

    def fidx(b, f, nv):
        return jnp.where(nv[b] > 0, f, nf - 1)

    grid_spec = pltpu.PrefetchScalarGridSpec(
        num_scalar_prefetch=2,
        grid=(n_rows // bm, nf),
        in_specs=[pl.BlockSpec((bm, d), lambda b, f, be, nv: (b, 0)),
                  pl.BlockSpec((1, d, tf), lambda b, f, be, nv: (be[b], 0, fidx(b, f, nv))),
                  pl.BlockSpec((1, d, tf), lambda b, f, be, nv: (be[b], 0, fidx(b, f, nv))),
                  pl.BlockSpec((1, tf, d), lambda b, f, be, nv: (be[b], fidx(b, f, nv), 0))],
        out_specs=pl.BlockSpec((bm, d), lambda b, f, be, nv: (b, 0)),
        scratch_shapes=[pltpu.VMEM((bm, d), F32)],
    )
    return pl.pallas_call(
        _moe_ffn_kernel,
        grid_spec=grid_spec,
        out_shape=jax.ShapeDtypeStruct((n_rows, d), BF16),
        compiler_params=_cparams(("parallel", "arbitrary")),
        name="moe_ffn",
    )(blk_e, blk_rows, xs, wg, wu, wd)


def _moe_combine_kernel(ce_ref, cb_ref, co_ref, nq_ref, pos_ref, gate_ref, y_ref, o_ref, buf_ref, sem, *, rows, qmax):
    i = pl.program_id(0)
    n = nq_ref[i]
    o_ref[...] = jnp.zeros_like(o_ref)

    def copy(slot, off):
        return pltpu.make_async_copy(y_ref.at[pl.ds(off, rows)], buf_ref.at[slot], sem.at[slot])

    def start(q):
        copy(lax.rem(q, 2), pl.multiple_of(co_ref[i * qmax + q], MOE_GRANULE)).start()

    @pl.when(n > 0)
    def _():
        start(0)

    def body(q, carry):
        k = i * qmax + q
        slot = lax.rem(q, 2)

        @pl.when(q + 1 < n)
        def _():
            start(q + 1)

        copy(slot, 0).wait()
        e = ce_ref[k]
        hit = _pick(pos_ref[pl.ds(e, 1), :], cb_ref[k], rows)
        gsub = jnp.sum(jnp.where(hit, gate_ref[pl.ds(e, 1), :], 0.0), axis=1, keepdims=True)
        yb = (buf_ref[slot].astype(F32) * gsub).astype(BF16)
        o_ref[...] += _dot_tn(jnp.where(hit, 1.0, 0.0).astype(BF16), yb)
        return carry

    lax.fori_loop(0, n, body, 0)


def moe_combine(ys, pos, gate, tables, tm, rows):
    ne, t = pos.shape
    d = ys.shape[1]
    ce, cb, co, nq, qmax = tables
    grid_spec = pltpu.PrefetchScalarGridSpec(
        num_scalar_prefetch=4,
        grid=(t // tm,),
        in_specs=[pl.BlockSpec((ne, tm), lambda i, *_: (0, i)),
                  pl.BlockSpec((ne, tm), lambda i, *_: (0, i)),
                  pl.BlockSpec(memory_space=pl.ANY)],
        out_specs=pl.BlockSpec((tm, d), lambda i, *_: (i, 0)),
        scratch_shapes=[pltpu.VMEM((2, rows, d), BF16), pltpu.SemaphoreType.DMA((2,))],
    )
    return pl.pallas_call(
        functools.partial(_moe_combine_kernel, rows=rows, qmax=qmax),
        grid_spec=grid_spec,
        out_shape=jax.ShapeDtypeStruct((t, d), F32),
        compiler_params=_cparams(("arbitrary",)),
        name="moe_combine",
    )(ce, cb, co, nq, pos, gate, ys)


def _chunk_tables(counts, seg, rows, qmax):
    ne = counts.shape[1]
    ns = (counts + rows - 1) // rows
    cs = jnp.cumsum(ns, axis=1)
    q = jnp.arange(qmax, dtype=jnp.int32)
    ce = jnp.minimum(jnp.sum(q[None, :, None] >= cs[:, None, :], axis=-1), ne - 1).astype(jnp.int32)
    cj = q[None, :] - jnp.take_along_axis(cs - ns, ce, axis=1)
    co = jnp.take_along_axis(seg, ce, axis=1) + cj * rows
    flat = lambda a: a.reshape(-1).astype(jnp.int32)
    return flat(ce), flat(cj * rows), flat(co), cs[:, -1].astype(jnp.int32), qmax


def moe(h, g, w_router, wg, wu, wd, tm=MOE_TILE, bm=MOE_BLOCK):
    t = h.shape[0]
    tm = min(tm, t)
    ne = wg.shape[0]
    nt = t // tm
    un, pos, gate, cnt = router(h, g, w_router.T, tm)
    counts = cnt[:, :, 0]
    padded = (counts + MOE_GRANULE - 1) // MOE_GRANULE * MOE_GRANULE
    tot = jnp.sum(padded, axis=0)
    ptot = (tot + MOE_SLACK + bm - 1) // bm * bm
    eend = jnp.cumsum(ptot)
    seg = (eend - ptot)[None, :] + jnp.cumsum(padded, axis=0) - padded
    n_blocks = (TOP_K * t + nt * ne * (MOE_GRANULE - 1) + ne * MOE_SLACK) // bm + ne
    blk_row0 = jnp.arange(n_blocks, dtype=jnp.int32) * bm
    blk_e = jnp.minimum(jnp.searchsorted(eend, blk_row0, side="right"), ne - 1).astype(jnp.int32)
    blk_rows = jnp.clip((eend - ptot + tot)[blk_e] - blk_row0, 0, bm).astype(jnp.int32)
    g_tab = _chunk_tables(counts, seg, MOE_GATHER_ROWS, TOP_K * tm // MOE_GATHER_ROWS + ne)
    c_tab = _chunk_tables(counts, seg, MOE_COMBINE_ROWS, TOP_K * tm // MOE_COMBINE_ROWS + ne)
    xs = moe_gather(un, pos, g_tab, n_blocks * bm, tm, MOE_GATHER_ROWS)
    ys = moe_ffn(xs, blk_e, blk_rows, wg, wu, wd, bm)
    return moe_combine(ys, pos, gate, c_tab, tm, MOE_COMBINE_ROWS)


def kernel(x, p, ln_mix, ln_ffn, ln_ple, ln_final, lb_table, ab_w_in, ab_conv, b_a_log, b_dt_bias, a_gnorm, b_gnorm, ab_w_out, c_w_in, c_conv_w, c_conv_b, c_w_r, c_b_r, c_w_i, c_b_i, c_lambda, c_w_out, ffn_w_gate, ffn_w_up, ffn_w_down, moe_router, moe_w_gate, moe_w_up, moe_w_down, ple_w_proj, ple_w_gate):
    bsz, seq, d = x.shape
    t = bsz * seq
    depth = ln_mix.shape[0]
    a_heads = lb_table.shape[1] // HEAD_DIM
    b_heads = b_a_log.shape[1]
    a_w = a_heads * HEAD_DIM
    b_w = b_heads * HEAD_DIM
    main_w = 4 * a_w + 4 * b_w
    bf = lambda a: a.astype(BF16)

    h = x.reshape(t, d)
    for layer in range(depth):
        j = layer // 2
        if layer % 2 == 0:
            w_in = ab_w_in[j]
            w_small = jnp.pad(w_in[:, main_w:], ((0, 0), (0, 128 - 2 * b_heads)))
            proj, small = norm_proj(h, ln_mix[layer], bf(w_in[:, :main_w]), bf(w_small))
            mixed = mixer_ab(proj, small, lb_table, a_gnorm[j], ab_conv[j], b_a_log[j], b_dt_bias[j], b_gnorm[j],
                             bsz=bsz, seq=seq, heads_a=a_heads, heads_b=b_heads, layer=layer)
            h = out_proj(mixed, bf(ab_w_out[j]), h)
            h = swiglu(h, ln_ffn[layer], bf(ffn_w_gate[j]), bf(ffn_w_up[j]), bf(ffn_w_down[j]))
            add = None
        else:
            xr, yb = norm_proj_gelu(h, ln_mix[layer], bf(c_w_in[j]))
            hy = rglru(xr, yb, c_conv_w[j], c_conv_b[j], bf(c_w_r[j]), c_b_r[j], bf(c_w_i[j]), c_b_i[j],
                       c_lambda[j], bsz=bsz, seq=seq)
            h = out_proj(hy, bf(c_w_out[j]), h)
            add = moe(h, ln_ffn[layer], moe_router[j], bf(moe_w_gate[j]), bf(moe_w_up[j]), bf(moe_w_down[j]))
        g_final = ln_final if layer == depth - 1 else None
        h = ple(h, p[layer].reshape(t, -1), ln_ple[layer], bf(ple_w_gate[layer]), bf(ple_w_proj[layer]),
                add=add, g_final=g_final)
    if depth == 0:
        raise ValueError("depth must be positive")
    return h.reshape(bsz, seq, d)
```

```python
import functools

import jax
import jax.numpy as jnp
from jax import lax
from jax.experimental import pallas as pl
from jax.experimental.pallas import tpu as pltpu

F32 = jnp.float32
BF16 = jnp.bfloat16
EPS = 1e-6
CHUNK = 64
SUB = 8
HEAD_DIM = 128
CONV_WIDTH = 4
GDN_HEAD_GROUP = 8
MIXER_ROWS = 128
FILL_AFTER_PREP = 4
FILL_AFTER_SCORES = 4
FILL_PER_LEVEL = 1
CONV_TAIL = 8
RGLRU_C = 8.0
SCAN_GROUP = 8
TOP_K = 2
MOE_TILE = 1024
MOE_BLOCK = 1024
MOE_PART_ROWS = 256
MOE_GRANULE = 16
MOE_GATHER_ROWS = 128
MOE_COMBINE_ROWS = 256
MOE_SLACK = 256
VMEM_LIMIT = 56 * 1024 * 1024


def _cparams(sem, vmem=VMEM_LIMIT):
    return pltpu.CompilerParams(dimension_semantics=sem, vmem_limit_bytes=vmem)


def _dot(a, b):
    return jnp.dot(a, b, preferred_element_type=F32)


def _dot_nt(a, b):
    return lax.dot_general(a, b, (((1,), (1,)), ((), ())), preferred_element_type=F32)


def _dot_tn(a, b):
    return lax.dot_general(a, b, (((0,), (0,)), ((), ())), preferred_element_type=F32)


def _split(a):
    hi = a.astype(BF16)
    lo = (a - hi.astype(F32)).astype(BF16)
    return hi, lo


def _dot2(a, b):
    ah, al = _split(a)
    bh = b.astype(BF16)
    return _dot(jnp.concatenate([ah, al], axis=1), jnp.concatenate([bh, bh], axis=0))


def _rms(x, g):
    return x * lax.rsqrt(jnp.mean(x * x, axis=-1, keepdims=True) + EPS) * g


def _sigmoid(x):
    return 1.0 / (1.0 + jnp.exp(-x))


def _silu(x):
    return x * _sigmoid(x)


def _norm_proj_kernel(h_ref, g_ref, w_ref, ws_ref, o_ref, os_ref, un_ref):
    j = pl.program_id(1)

    @pl.when(j == 0)
    def _():
        un = _rms(h_ref[...], g_ref[...]).astype(BF16)
        un_ref[...] = un
        os_ref[...] = _dot(un, ws_ref[...])

    o_ref[...] = _dot(un_ref[...], w_ref[...])


def norm_proj(h, g, w, ws, tm=1024, tn=1024):
    t, d = h.shape
    tm = min(tm, t)
    n = w.shape[1]
    return pl.pallas_call(
        _norm_proj_kernel,
        grid=(t // tm, n // tn),
        in_specs=[pl.BlockSpec((tm, d), lambda i, j: (i, 0)),
                  pl.BlockSpec((1, d), lambda i, j: (0, 0)),
                  pl.BlockSpec((d, tn), lambda i, j: (0, j)),
                  pl.BlockSpec((d, ws.shape[1]), lambda i, j: (0, 0))],
        out_specs=[pl.BlockSpec((tm, tn), lambda i, j: (i, j)),
                   pl.BlockSpec((tm, ws.shape[1]), lambda i, j: (i, 0))],
        out_shape=[jax.ShapeDtypeStruct((t, n), F32), jax.ShapeDtypeStruct((t, ws.shape[1]), F32)],
        scratch_shapes=[pltpu.VMEM((tm, d), BF16)],
        compiler_params=_cparams(("parallel", "arbitrary")),
        name="norm_proj",
    )(h, g.reshape(1, d), w, ws)


def _norm_proj_gelu_kernel(h_ref, g_ref, wy_ref, wx_ref, o_ref, y_ref, un_ref):
    @pl.when(pl.program_id(1) == 0)
    def _():
        un_ref[...] = _rms(h_ref[...], g_ref[...]).astype(BF16)

    un = un_ref[...]
    y_ref[...] = jax.nn.gelu(_dot(un, wy_ref[...])).astype(BF16)
    o_ref[...] = _dot(un, wx_ref[...])


def norm_proj_gelu(h, g, w, tm=1024, tn=512):
    t, d = h.shape
    tm = min(tm, t)
    half = w.shape[1] // 2
    nh = half // tn
    return pl.pallas_call(
        _norm_proj_gelu_kernel,
        grid=(t // tm, nh),
        in_specs=[pl.BlockSpec((tm, d), lambda i, j: (i, 0)),
                  pl.BlockSpec((1, d), lambda i, j: (0, 0)),
                  pl.BlockSpec((d, tn), lambda i, j: (0, j)),
                  pl.BlockSpec((d, tn), lambda i, j: (0, nh + j))],
        out_specs=[pl.BlockSpec((tm, tn), lambda i, j: (i, j)),
                   pl.BlockSpec((tm, tn), lambda i, j: (i, j))],
        out_shape=[jax.ShapeDtypeStruct((t, half), F32), jax.ShapeDtypeStruct((t, half), BF16)],
        scratch_shapes=[pltpu.VMEM((tm, d), BF16)],
        compiler_params=_cparams(("parallel", "arbitrary")),
        name="norm_proj_gelu",
    )(h, g.reshape(1, d), w, w)


def _hgrn2_work(q_ref, f_ref, i_ref, g_ref, lbt_ref, gn_ref, o_ref, st_ref, *, layer, heads):
    c = CHUNK

    lbt = lbt_ref[...]
    e = jnp.exp(lbt - jnp.max(lbt, axis=0, keepdims=True))
    lb_all = jnp.sum(e[:layer + 1], axis=0, keepdims=True) / jnp.sum(e, axis=0, keepdims=True)

    row = lax.broadcasted_iota(jnp.int32, (c, c), 0)
    col = lax.broadcasted_iota(jnp.int32, (c, c), 1)
    tril = jnp.where(row >= col, 1.0, 0.0).astype(BF16)
    gn = gn_ref[...]

    levels = []
    ln = c // 2
    while ln >= SUB:
        levels += [(m * 2 * ln, m * 2 * ln + ln, ln) for m in range(c // (2 * ln))]
        ln //= 2
    n_pairs = sum(l[2] for l in levels)

    def seg_id(idx):
        sid = jnp.zeros_like(idx)
        start = 0
        for l in levels[:-1]:
            start += l[2]
            sid = sid + jnp.where(idx >= start, 1, 0)
        return sid

    same_seg = (seg_id(lax.broadcasted_iota(jnp.int32, (n_pairs, n_pairs), 0))
                == seg_id(lax.broadcasted_iota(jnp.int32, (n_pairs, n_pairs), 1)))
    sub_i = lax.broadcasted_iota(jnp.int32, (c // SUB, SUB, HEAD_DIM), 1)

    sls = [slice(h * HEAD_DIM, (h + 1) * HEAD_DIM) for h in range(heads)]

    def gates(h):
        lb = lb_all[:, sls[h]]
        q = q_ref[:, sls[h]] * (HEAD_DIM ** -0.5)
        forget = lb + (1.0 - lb) * _sigmoid(f_ref[:, sls[h]])
        lh, ll = _split(jnp.log(forget))
        b2 = _dot(tril, jnp.concatenate([lh, ll], axis=1))
        return q, 1.0 - forget, i_ref[:, sls[h]], b2[:, :HEAD_DIM] + b2[:, HEAD_DIM:]

    def block_pairs(h, q, k, v, b):
        o = _dot_nt((q * jnp.exp(b)).astype(BF16), st_ref[h].astype(BF16))
        qs, ks, vs = [], [], []
        for k0, q0, ln in levels:
            bref = b[q0 - 1:q0, :]
            qs.append(q[q0:q0 + ln] * jnp.exp(b[q0:q0 + ln] - bref))
            ks.append(k[k0:k0 + ln] * jnp.exp(bref - b[k0:k0 + ln]))
            vs.append(v[k0:k0 + ln])
        s = _dot_nt(jnp.concatenate(qs, axis=0).astype(BF16), jnp.concatenate(ks, axis=0).astype(BF16))
        r = _dot(jnp.where(same_seg, s, 0.0).astype(BF16), jnp.concatenate(vs, axis=0).astype(BF16))
        groups = [None] * (c // SUB)
        start = 0
        for k0, q0, ln in levels:
            for j in range(ln // SUB):
                piece = r[start + j * SUB:start + (j + 1) * SUB]
                gi = q0 // SUB + j
                groups[gi] = piece if groups[gi] is None else groups[gi] + piece
            start += ln
        groups[0] = jnp.zeros((SUB, HEAD_DIM), F32)
        return o + jnp.concatenate(groups, axis=0)

    def near_pairs(q, k, v, b):
        q3, k3, v3, b3 = (a.reshape(c // SUB, SUB, HEAD_DIM) for a in (q, k, v, b))
        o3 = jnp.sum(q3 * k3, axis=2, keepdims=True) * v3
        for d in range(1, SUB):
            dec = jnp.exp(jnp.where(sub_i >= d, b3 - pltpu.roll(b3, d, 1), -jnp.inf))
            w = jnp.sum(q3 * pltpu.roll(k3, d, 1) * dec, axis=2, keepdims=True)
            o3 = o3 + w * pltpu.roll(v3, d, 1)
        return o3.reshape(c, HEAD_DIM)

    qkvb, far = {}, {}

    def state_part(h):
        qkvb[h] = gates(h)
        q, k, v, b = qkvb[h]
        far[h] = block_pairs(h, q, k, v, b)
        blast = b[c - 1:c, :]
        kd = (k * jnp.exp(blast - b)).astype(BF16)
        st_ref[h] = st_ref[h] * jnp.exp(blast) + _dot_tn(v.astype(BF16), kd)

    def block_part(h):
        o = far[h] + near_pairs(*qkvb[h])
        on = o * lax.rsqrt(jnp.mean(o * o, axis=1, keepdims=True) + EPS) * gn
        o_ref[:, sls[h]] = (on * _silu(g_ref[:, sls[h]])).astype(BF16)

    return ([functools.partial(state_part, h) for h in range(heads)]
            + [functools.partial(block_part, h) for h in range(heads)])


def _causal_conv(ext_ref, x, w, first):
    n = x.shape[0]

    @pl.when(first)
    def _():
        ext_ref[0:CONV_TAIL, :] = jnp.zeros((CONV_TAIL, x.shape[1]), F32)

    ext_ref[CONV_TAIL:CONV_TAIL + n, :] = x
    y = x * w[CONV_WIDTH - 1:CONV_WIDTH, :]
    for k in range(CONV_WIDTH - 1):
        off = CONV_TAIL - (CONV_WIDTH - 1) + k
        y = y + ext_ref[off:off + n, :] * w[k:k + 1, :]
    ext_ref[0:CONV_TAIL, :] = ext_ref[n:n + CONV_TAIL, :]
    return y


def _mixer_ab_kernel(qa_ref, fa_ref, ia_ref, ga_ref, q_ref, k_ref, v_ref, z_ref, sm_ref, lbt_ref, gna_ref, cw_ref,
                     alog_ref, dtb_ref, gn_ref, o_ref, sta_ref, st_ref, eq_ref, ek_ref, ev_ref,
                     *, heads_a, heads, layer):
    c = CHUNK
    first = pl.program_id(1) == 0

    @pl.when(first)
    def _():
        sta_ref[...] = jnp.zeros_like(sta_ref)
        st_ref[...] = jnp.zeros_like(st_ref)

    col0 = heads_a * HEAD_DIM
    hw = heads * HEAD_DIM
    cw = cw_ref[...]
    qc_all = _silu(_causal_conv(eq_ref, q_ref[...], cw[:, 0:hw], first))
    kc_all = _silu(_causal_conv(ek_ref, k_ref[...], cw[:, hw:2 * hw], first))
    vc_all = _silu(_causal_conv(ev_ref, v_ref[...], cw[:, 2 * hw:3 * hw], first))

    row = lax.broadcasted_iota(jnp.int32, (c, c), 0)
    col = lax.broadcasted_iota(jnp.int32, (c, c), 1)
    causal = row >= col
    strict = row > col
    eye = row == col
    eye_f = jnp.where(eye, 1.0, 0.0)
    diag_blk = (row // SUB) == (col // SUB)
    merge_masks = []
    s = SUB
    while s < c:
        merge_masks.append(jnp.logical_and((row // (2 * s)) == (col // (2 * s)), (row // s) == (col // s) + 1))
        s *= 2
    gn = gn_ref[...]

    def to_row(colv):
        return jnp.sum(jnp.where(eye, colv, 0.0), axis=0, keepdims=True)

    def head_group(hg, qc, kc, vc, sm, z_ref, o_ref, fill):
        hs = range(len(hg))
        sls = [slice(h * HEAD_DIM, (h + 1) * HEAD_DIM) for h in hg]
        qn, kn, knb, beta, gam_col, egam, decay, kbeta = [], [], [], [], [], [], [], []
        for i, h in enumerate(hg):
            qh, kh = qc[:, sls[i]], kc[:, sls[i]]
            qn.append(qh * lax.rsqrt(jnp.sum(qh * qh, axis=1, keepdims=True) + EPS) * (HEAD_DIM ** -0.5))
            kn.append(kh * lax.rsqrt(jnp.sum(kh * kh, axis=1, keepdims=True) + EPS))
            knb.append(kn[i].astype(BF16))
            beta.append(_sigmoid(sm[:, heads + h:heads + h + 1]))
            g_col = -jnp.exp(alog_ref[:, h:h + 1]) * jax.nn.softplus(sm[:, h:h + 1] + dtb_ref[:, h:h + 1])
            g_row = to_row(g_col)
            gam_col.append(jnp.sum(jnp.where(causal, g_row, 0.0), axis=1, keepdims=True))
            gam_row = jnp.sum(jnp.where(strict, 0.0, g_col), axis=0, keepdims=True)
            decay.append(jnp.exp(jnp.where(causal, gam_col[i] - gam_row, -jnp.inf)))
            egam.append(jnp.exp(gam_col[i]))
            kbeta.append(kn[i] * beta[i])

        fill(FILL_AFTER_PREP)
        a_mat = [jnp.where(strict, _dot_nt(kbeta[i].astype(BF16), knb[i]) * decay[i], 0.0) for i in hs]
        qk = [(_dot_nt(qn[i].astype(BF16), knb[i]) * decay[i]).astype(BF16) for i in hs]
        fill(FILL_AFTER_SCORES)
        d_mat = [jnp.where(diag_blk, a_mat[i], 0.0) for i in hs]
        x = [eye_f - d_mat[i] for i in hs]
        p = [_dot2(d_mat[i], d_mat[i]) for i in hs]
        fill(FILL_PER_LEVEL)
        n_sq = (SUB - 1).bit_length() - 1
        for lvl in range(n_sq):
            if lvl < n_sq - 1:
                y = [_dot2(jnp.concatenate([x[i], p[i]], axis=0), p[i]) for i in hs]
                fill(FILL_PER_LEVEL)
                x = [x[i] + y[i][:c] for i in hs]
                p = [y[i][c:] for i in hs]
            else:
                y = [_dot2(x[i], p[i]) for i in hs]
                fill(FILL_PER_LEVEL)
                x = [x[i] + y[i] for i in hs]
        for below in merge_masks:
            y = [_dot2(x[i], jnp.where(below, a_mat[i], 0.0)) for i in hs]
            fill(FILL_PER_LEVEL)
            y = [_dot2(y[i], x[i]) for i in hs]
            fill(FILL_PER_LEVEL)
            x = [x[i] - y[i] for i in hs]
        rhs = [jnp.concatenate([vc[:, sls[i]] * beta[i], kbeta[i] * egam[i]], axis=1).astype(BF16) for i in hs]
        uw = [_dot(x[i].astype(BF16), rhs[i]) for i in hs]

        st = [st_ref[h] for h in hg]
        stb = [s.astype(BF16) for s in st]
        v_new = [uw[i][:, :HEAD_DIM] - _dot_nt(uw[i][:, HEAD_DIM:].astype(BF16), stb[i]) for i in hs]
        vnb = [v.astype(BF16) for v in v_new]
        o = [_dot_nt((qn[i] * egam[i]).astype(BF16), stb[i]) + _dot(qk[i], vnb[i]) for i in hs]
        for i, h in enumerate(hg):
            glast = gam_col[i][c - 1:c, :]
            kd = (kn[i] * jnp.exp(glast - gam_col[i])).astype(BF16)
            st_ref[h] = st[i] * jnp.exp(glast) + _dot_tn(vnb[i], kd)
        for i in hs:
            on = o[i] * lax.rsqrt(jnp.mean(o[i] * o[i], axis=1, keepdims=True) + EPS) * gn
            o_ref[:, col0 + hg[i] * HEAD_DIM:col0 + (hg[i] + 1) * HEAD_DIM] = (on * _silu(z_ref[:, sls[i]])).astype(BF16)

    for ci in range(q_ref.shape[0] // c):
        rs = pl.ds(ci * c, c)
        pending = _hgrn2_work(qa_ref.at[rs], fa_ref.at[rs], ia_ref.at[rs], ga_ref.at[rs], lbt_ref, gna_ref,
                              o_ref.at[rs], sta_ref, layer=layer, heads=heads_a)

        def fill(n, pending=pending):
            for _ in range(min(n, len(pending))):
                pending.pop(0)()

        r0 = ci * c
        for h0 in range(0, heads, GDN_HEAD_GROUP):
            head_group(list(range(h0, min(h0 + GDN_HEAD_GROUP, heads))), qc_all[r0:r0 + c], kc_all[r0:r0 + c],
                       vc_all[r0:r0 + c], sm_ref[rs, :], z_ref.at[rs], o_ref.at[rs], fill)
        fill(len(pending))


def mixer_ab(proj, small, lb_table, gnorm_a, conv_w, a_log, dt_bias, gnorm_b, *, bsz, seq, heads_a, heads_b, layer):
    t = proj.shape[0]
    wa, wb = heads_a * HEAD_DIM, heads_b * HEAD_DIM
    rows = min(MIXER_ROWS, seq)
    nc = seq // rows
    row = lambda b, s: b * nc + s

    def spec_a(k):
        return pl.BlockSpec((rows, wa), lambda b, s: (row(b, s), k))

    def spec_b(k):
        return pl.BlockSpec((rows, wb), lambda b, s: (row(b, s), 4 * wa // wb + k))

    def const(shape):
        return pl.BlockSpec(shape, lambda b, s: (0, 0))

    return pl.pallas_call(
        functools.partial(_mixer_ab_kernel, heads_a=heads_a, heads=heads_b, layer=layer),
        grid=(bsz, nc),
        in_specs=[spec_a(0), spec_a(1), spec_a(2), spec_a(3), spec_b(0), spec_b(1), spec_b(2), spec_b(3),
                  pl.BlockSpec((rows, small.shape[1]), lambda b, s: (row(b, s), 0)),
                  const(lb_table.shape), const((1, HEAD_DIM)),
                  const(conv_w.shape), const((1, heads_b)), const((1, heads_b)), const((1, HEAD_DIM))],
        out_specs=pl.BlockSpec((rows, wa + wb), lambda b, s: (row(b, s), 0)),
        out_shape=jax.ShapeDtypeStruct((t, wa + wb), BF16),
        scratch_shapes=[pltpu.VMEM((heads_a, HEAD_DIM, HEAD_DIM), F32), pltpu.VMEM((heads_b, HEAD_DIM, HEAD_DIM), F32)]
        + [pltpu.VMEM((rows + CONV_TAIL, wb), F32)] * 3,
        compiler_params=_cparams(("parallel", "arbitrary")),
        name="mixer_ab",
    )(proj, proj, proj, proj, proj, proj, proj, proj, small, lb_table, gnorm_a.reshape(1, HEAD_DIM), conv_w,
      a_log.reshape(1, heads_b), dt_bias.reshape(1, heads_b), gnorm_b.reshape(1, HEAD_DIM))


def _out_proj_kernel(a_ref, w_ref, h_ref, o_ref):
    o_ref[...] = h_ref[...] + _dot(a_ref[...], w_ref[...])


def out_proj(a, w, h, tm=1024, tn=1024):
    t, k = a.shape
    tm = min(tm, t)
    n = w.shape[1]
    return pl.pallas_call(
        _out_proj_kernel,
        grid=(t // tm, n // tn),
        in_specs=[pl.BlockSpec((tm, k), lambda i, j: (i, 0)),
                  pl.BlockSpec((k, tn), lambda i, j: (0, j)),
                  pl.BlockSpec((tm, tn), lambda i, j: (i, j))],
        out_specs=pl.BlockSpec((tm, tn), lambda i, j: (i, j)),
        out_shape=jax.ShapeDtypeStruct((t, n), F32),
        compiler_params=_cparams(("parallel", "arbitrary")),
        name="out_proj",
    )(a, w, h)


def _swiglu_kernel(h_ref, g_ref, wg_ref, wu_ref, wd_ref, o_ref, un_ref):
    f = pl.program_id(1)

    @pl.when(f == 0)
    def _():
        un_ref[...] = _rms(h_ref[...], g_ref[...]).astype(BF16)
        o_ref[...] = h_ref[...]

    un = un_ref[...]
    hb = (_silu(_dot(un, wg_ref[...])) * _dot(un, wu_ref[...])).astype(BF16)
    o_ref[...] += _dot(hb, wd_ref[...])


def swiglu(h, g, wg, wu, wd, tm=1024, tf=512):
    t, d = h.shape
    tm = min(tm, t)
    ff = wg.shape[1]
    return pl.pallas_call(
        _swiglu_kernel,
        grid=(t // tm, ff // tf),
        in_specs=[pl.BlockSpec((tm, d), lambda i, f: (i, 0), pipeline_mode=pl.Buffered(1)),
                  pl.BlockSpec((1, d), lambda i, f: (0, 0)),
                  pl.BlockSpec((d, tf), lambda i, f: (0, f)),
                  pl.BlockSpec((d, tf), lambda i, f: (0, f)),
                  pl.BlockSpec((tf, d), lambda i, f: (f, 0))],
        out_specs=pl.BlockSpec((tm, d), lambda i, f: (i, 0)),
        out_shape=jax.ShapeDtypeStruct((t, d), F32),
        scratch_shapes=[pltpu.VMEM((tm, d), BF16)],
        compiler_params=_cparams(("parallel", "arbitrary")),
        name="swiglu",
    )(h, g.reshape(1, d), wg, wu, wd)


def _ple_kernel(*refs, has_add, has_final):
    h_ref, p_ref, g_ref, wg_ref, wp_ref = refs[:5]
    k = 5
    add_ref = gf_ref = None
    if has_add:
        add_ref = refs[k]
        k += 1
    if has_final:
        gf_ref = refs[k]
        k += 1
    o_ref = refs[k]
    h = h_ref[...]
    if has_add:
        h = h + add_ref[...]
    un = _rms(h, g_ref[...]).astype(BF16)
    gate = _sigmoid(_dot(un, wg_ref[...]))
    out = h + gate * _dot(p_ref[...].astype(BF16), wp_ref[...])
    if has_final:
        out = _rms(out, gf_ref[...])
    o_ref[...] = out


def ple(h, p, g, wg, wp, add=None, g_final=None, tm=512):
    t, d = h.shape
    tm = min(tm, t)
    pd = p.shape[1]
    row = lambda i: (i, 0)
    const = lambda i: (0, 0)
    in_specs = [pl.BlockSpec((tm, d), row), pl.BlockSpec((tm, pd), row), pl.BlockSpec((1, d), const),
                pl.BlockSpec((d, d), const), pl.BlockSpec((pd, d), const)]
    args = [h, p, g.reshape(1, d), wg, wp]
    if add is not None:
        in_specs.append(pl.BlockSpec((tm, d), row))
        args.append(add)
    if g_final is not None:
        in_specs.append(pl.BlockSpec((1, d), const))
        args.append(g_final.reshape(1, d))
    return pl.pallas_call(
        functools.partial(_ple_kernel, has_add=add is not None, has_final=g_final is not None),
        grid=(t // tm,),
        in_specs=in_specs,
        out_specs=pl.BlockSpec((tm, d), row),
        out_shape=jax.ShapeDtypeStruct((t, d), F32),
        compiler_params=_cparams(("parallel",)),
        name="ple",
    )(*args)


def _rglru_kernel(x_ref, y_ref, cw_ref, cb_ref, wr_ref, br_ref, wi_ref, bi_ref, lam_ref, o_ref,
                  ext_ref, hc_ref, *, blocks):
    n = x_ref.shape[0]
    first = pl.program_id(1) == 0

    @pl.when(first)
    def _():
        hc_ref[...] = jnp.zeros_like(hc_ref)

    xc = _causal_conv(ext_ref, x_ref[...], cw_ref[...], first) + cb_ref[...]
    bw = xc.shape[1] // blocks
    rowi = lax.broadcasted_iota(jnp.int32, (n, bw), 0)
    at_start = jnp.logical_and(first, rowi == 0)
    gidx = lax.broadcasted_iota(jnp.int32, (n // SCAN_GROUP, SCAN_GROUP, bw), 1)

    for nb in range(blocks):
        sl = slice(nb * bw, (nb + 1) * bw)
        xb = xc[:, sl]
        xbb = xb.astype(BF16)
        r = _sigmoid(_dot(xbb, wr_ref[nb]) + br_ref[:, sl])
        gi = _sigmoid(_dot(xbb, wi_ref[nb]) + bi_ref[:, sl])
        log_a = -RGLRU_C * r * jax.nn.softplus(-lam_ref[:, sl])
        a = jnp.exp(log_a)
        m2 = 1.0 - a * a
        mult = jnp.where(m2 > 0.0, m2 * lax.rsqrt(m2), 0.0)
        mult = jnp.where(at_start, 1.0, mult)
        b = mult * gi * xb
        a = a.reshape(n // SCAN_GROUP, SCAN_GROUP, bw)
        b = b.reshape(n // SCAN_GROUP, SCAN_GROUP, bw)
        sh = 1
        while sh < SCAN_GROUP:
            ok = gidx >= sh
            a_prev = jnp.where(ok, pltpu.roll(a, sh, 1), 1.0)
            b_prev = jnp.where(ok, pltpu.roll(b, sh, 1), 0.0)
            b = b + a * b_prev
            a = a * a_prev
            sh *= 2
        carry = hc_ref[:, sl]
        groups = []
        for gi_ in range(n // SCAN_GROUP):
            hg = b[gi_] + a[gi_] * carry
            groups.append(hg)
            carry = hg[SCAN_GROUP - 1:SCAN_GROUP, :]
        hseq = jnp.concatenate(groups, axis=0)
        hc_ref[:, sl] = carry
        o_ref[:, sl] = (hseq * y_ref[:, sl].astype(F32)).astype(BF16)


def rglru(xr, y, conv_w, conv_b, w_r, b_r, w_i, b_i, lam, *, bsz, seq, rows=256):
    t, cwid = xr.shape
    blocks = w_r.shape[0]
    ns = seq // rows
    row = lambda b, s: (b * ns + s, 0)
    c2 = lambda b, s: (0, 0)
    c3 = lambda b, s: (0, 0, 0)
    vec = lambda a: a.reshape(1, cwid)
    return pl.pallas_call(
        functools.partial(_rglru_kernel, blocks=blocks),
        grid=(bsz, ns),
        in_specs=[pl.BlockSpec((rows, cwid), row), pl.BlockSpec((rows, cwid), row),
                  pl.BlockSpec(conv_w.shape, c2), pl.BlockSpec((1, cwid), c2),
                  pl.BlockSpec(w_r.shape, c3), pl.BlockSpec((1, cwid), c2),
                  pl.BlockSpec(w_i.shape, c3), pl.BlockSpec((1, cwid), c2),
                  pl.BlockSpec((1, cwid), c2)],
        out_specs=pl.BlockSpec((rows, cwid), row),
        out_shape=jax.ShapeDtypeStruct((t, cwid), BF16),
        scratch_shapes=[pltpu.VMEM((rows + CONV_TAIL, cwid), F32), pltpu.VMEM((1, cwid), F32)],
        compiler_params=_cparams(("parallel", "arbitrary")),
        name="rglru",
    )(xr, y, conv_w, vec(conv_b), w_r, vec(b_r), w_i, vec(b_i), vec(lam))


def _router_kernel(h_ref, g_ref, wr_ref, un_ref, pos_ref, gate_ref, cnt_ref):
    tm = h_ref.shape[0]
    ne = wr_ref.shape[0]
    un = _rms(h_ref[...], g_ref[...])
    uh, ul = _split(un)
    un_ref[...] = uh
    wh, wl = _split(wr_ref[...])
    logits = _dot_nt(wh, uh) + _dot_nt(wh, ul) + _dot_nt(wl, uh)
    eidx = lax.broadcasted_iota(jnp.int32, (ne, tm), 0).astype(F32)
    m1 = jnp.max(logits, axis=0, keepdims=True)
    i1 = jnp.min(jnp.where(logits == m1, eidx, float(ne)), axis=0, keepdims=True)
    mask1 = eidx == i1
    rest = jnp.where(mask1, -jnp.inf, logits)
    m2 = jnp.max(rest, axis=0, keepdims=True)
    i2 = jnp.min(jnp.where(rest == m2, eidx, float(ne)), axis=0, keepdims=True)
    mask2 = eidx == i2
    e2 = jnp.exp(m2 - m1)
    g1 = 1.0 / (1.0 + e2)
    g2 = e2 / (1.0 + e2)
    gate_ref[...] = jnp.where(mask1, g1, jnp.where(mask2, g2, 0.0))
    sel = jnp.logical_or(mask1, mask2)
    self32 = jnp.where(sel, 1.0, 0.0)
    before = lax.broadcasted_iota(jnp.int32, (tm, tm), 0) < lax.broadcasted_iota(jnp.int32, (tm, tm), 1)
    rank = _dot(self32.astype(BF16), jnp.where(before, 1.0, 0.0).astype(BF16))
    pos_ref[...] = jnp.where(sel, rank, -1.0)
    cnt = jnp.sum(self32, axis=1, keepdims=True).astype(jnp.int32)
    cnt_ref[0] = jnp.broadcast_to(cnt, cnt_ref.shape[1:])


def router(h, g, wr_t, tm):
    t, d = h.shape
    tm = min(tm, t)
    ne = wr_t.shape[0]
    nt = t // tm
    return pl.pallas_call(
        _router_kernel,
        grid=(nt,),
        in_specs=[pl.BlockSpec((tm, d), lambda i: (i, 0)),
                  pl.BlockSpec((1, d), lambda i: (0, 0)),
                  pl.BlockSpec((ne, d), lambda i: (0, 0))],
        out_specs=[pl.BlockSpec((tm, d), lambda i: (i, 0)),
                   pl.BlockSpec((ne, tm), lambda i: (0, i)),
                   pl.BlockSpec((ne, tm), lambda i: (0, i)),
                   pl.BlockSpec((1, ne, 128), lambda i: (i, 0, 0))],
        out_shape=[jax.ShapeDtypeStruct((t, d), BF16), jax.ShapeDtypeStruct((ne, t), F32),
                   jax.ShapeDtypeStruct((ne, t), F32), jax.ShapeDtypeStruct((nt, ne, 128), jnp.int32)],
        compiler_params=_cparams(("parallel",)),
        name="moe_router",
    )(h, g.reshape(1, d), wr_t)


def _pick(pos, base, rows):
    slot = lax.broadcasted_iota(jnp.int32, (rows, pos.shape[1]), 0).astype(F32)
    return pos == slot + base.astype(F32)


def _moe_gather_kernel(ce_ref, cb_ref, co_ref, nq_ref, un_ref, pos_ref, xs_in_ref, xs_ref, buf_ref, sem, *, rows, qmax):
    del xs_in_ref
    i = pl.program_id(0)
    n = nq_ref[i]

    def copy(slot, off):
        return pltpu.make_async_copy(buf_ref.at[slot], xs_ref.at[pl.ds(off, rows)], sem.at[slot])

    def body(q, carry):
        k = i * qmax + q
        slot = lax.rem(q, 2)

        @pl.when(q >= 2)
        def _():
            copy(slot, 0).wait()

        pos = pos_ref[pl.ds(ce_ref[k], 1), :]
        sel = jnp.where(_pick(pos, cb_ref[k], rows), 1.0, 0.0).astype(BF16)
        buf_ref[slot] = _dot(sel, un_ref[...]).astype(BF16)
        copy(slot, pl.multiple_of(co_ref[k], MOE_GRANULE)).start()
        return carry

    lax.fori_loop(0, n, body, 0)

    @pl.when(n >= 2)
    def _():
        copy(lax.rem(n, 2), 0).wait()

    @pl.when(n >= 1)
    def _():
        copy(lax.rem(n + 1, 2), 0).wait()


def moe_gather(un, pos, tables, n_rows, tm, rows):
    t, d = un.shape
    ne = pos.shape[0]
    ce, cb, co, nq, qmax = tables
    grid_spec = pltpu.PrefetchScalarGridSpec(
        num_scalar_prefetch=4,
        grid=(t // tm,),
        in_specs=[pl.BlockSpec((tm, d), lambda i, *_: (i, 0)),
                  pl.BlockSpec((ne, tm), lambda i, *_: (0, i)),
                  pl.BlockSpec(memory_space=pl.ANY)],
        out_specs=pl.BlockSpec(memory_space=pl.ANY),
        scratch_shapes=[pltpu.VMEM((2, rows, d), BF16), pltpu.SemaphoreType.DMA((2,))],
    )
    return pl.pallas_call(
        functools.partial(_moe_gather_kernel, rows=rows, qmax=qmax),
        grid_spec=grid_spec,
        out_shape=jax.ShapeDtypeStruct((n_rows, d), BF16),
        input_output_aliases={6: 0},
        compiler_params=_cparams(("arbitrary",)),
        name="moe_gather",
    )(ce, cb, co, nq, un, pos, jnp.zeros((n_rows, d), BF16))


def _moe_ffn_kernel(be_ref, nv_ref, x_ref, wg_ref, wu_ref, wd_ref, o_ref, acc_ref):
    del be_ref
    b, f = pl.program_id(0), pl.program_id(1)
    nf = pl.num_programs(1)
    bm = x_ref.shape[0]
    nv = nv_ref[b]

    @pl.when(f == 0)
    def _():
        acc_ref[...] = jnp.zeros_like(acc_ref)

    def ffn(rows):
        x = x_ref[rows, :]
        hb = (_silu(_dot(x, wg_ref[0])) * _dot(x, wu_ref[0])).astype(BF16)
        acc_ref[rows, :] += _dot(hb, wd_ref[0])

    @pl.when(nv > bm - MOE_PART_ROWS)
    def _():
        ffn(pl.ds(0, bm))

    for part in range(bm // MOE_PART_ROWS):
        @pl.when(jnp.logical_and(nv <= bm - MOE_PART_ROWS, nv > part * MOE_PART_ROWS))
        def _():
            ffn(pl.ds(part * MOE_PART_ROWS, MOE_PART_ROWS))

    @pl.when(f == nf - 1)
    def _():
        o_ref[...] = acc_ref[...].astype(BF16)


def moe_ffn(xs, blk_e, blk_rows, wg, wu, wd, bm, tf=512):
    n_rows, d = xs.shape
    ff = wg.shape[2]
    nf = ff // tf

    def fidx(b, f, nv):
        return jnp.where(nv[b] > 0, f, nf - 1)

    grid_spec = pltpu.PrefetchScalarGridSpec(
        num_scalar_prefetch=2,
        grid=(n_rows // bm, nf),
        in_specs=[pl.BlockSpec((bm, d), lambda b, f, be, nv: (b, 0)),
                  pl.BlockSpec((1, d, tf), lambda b, f, be, nv: (be[b], 0, fidx(b, f, nv))),
                  pl.BlockSpec((1, d, tf), lambda b, f, be, nv: (be[b], 0, fidx(b, f, nv))),
                  pl.BlockSpec((1, tf, d), lambda b, f, be, nv: (be[b], fidx(b, f, nv), 0))],
        out_specs=pl.BlockSpec((bm, d), lambda b, f, be, nv: (b, 0)),
        scratch_shapes=[pltpu.VMEM((bm, d), F32)],
    )
    return pl.pallas_call(
        _moe_ffn_kernel,
        grid_spec=grid_spec,
        out_shape=jax.ShapeDtypeStruct((n_rows, d), BF16),
        compiler_params=_cparams(("parallel", "arbitrary")),
        name="moe_ffn",
    )(blk_e, blk_rows, xs, wg, wu, wd)


def _moe_combine_kernel(ce_ref, cb_ref, co_ref, nq_ref, pos_ref, gate_ref, y_ref, o_ref, buf_ref, sem, *, rows, qmax):
    i = pl.program_id(0)
    n = nq_ref[i]
    o_ref[...] = jnp.zeros_like(o_ref)

    def copy(slot, off):
        return pltpu.make_async_copy(y_ref.at[pl.ds(off, rows)], buf_ref.at[slot], sem.at[slot])

    def chunk(q):
        return i * qmax + jnp.minimum(q, n - 1)

    def start_pair(j):
        for t in range(2):
            copy(2 * lax.rem(j, 2) + t, pl.multiple_of(co_ref[chunk(2 * j + t)], MOE_GRANULE)).start()

    @pl.when(n > 0)
    def _():
        start_pair(0)

    n_pairs = (n + 1) // 2

    def body(j, carry):
        @pl.when(j + 1 < n_pairs)
        def _():
            start_pair(j + 1)

        picks, ys = [], []
        for t in range(2):
            q = 2 * j + t
            slot = 2 * lax.rem(j, 2) + t
            copy(slot, 0).wait()
            k = chunk(q)
            e = ce_ref[k]
            hit = jnp.logical_and(_pick(pos_ref[pl.ds(e, 1), :], cb_ref[k], rows), q < n)
            gsub = jnp.sum(jnp.where(hit, gate_ref[pl.ds(e, 1), :], 0.0), axis=1, keepdims=True)
            picks.append(jnp.where(hit, 1.0, 0.0).astype(BF16))
            ys.append((buf_ref[slot].astype(F32) * gsub).astype(BF16))
        o_ref[...] += _dot_tn(jnp.concatenate(picks, axis=0), jnp.concatenate(ys, axis=0))
        return carry

    lax.fori_loop(0, n_pairs, body, 0)


def moe_combine(ys, pos, gate, tables, tm, rows):
    ne, t = pos.shape
    d = ys.shape[1]
    ce, cb, co, nq, qmax = tables
    grid_spec = pltpu.PrefetchScalarGridSpec(
        num_scalar_prefetch=4,
        grid=(t // tm,),
        in_specs=[pl.BlockSpec((ne, tm), lambda i, *_: (0, i)),
                  pl.BlockSpec((ne, tm), lambda i, *_: (0, i)),
                  pl.BlockSpec(memory_space=pl.ANY)],
        out_specs=pl.BlockSpec((tm, d), lambda i, *_: (i, 0)),
        scratch_shapes=[pltpu.VMEM((4, rows, d), BF16), pltpu.SemaphoreType.DMA((4,))],
    )
    return pl.pallas_call(
        functools.partial(_moe_combine_kernel, rows=rows, qmax=qmax),
        grid_spec=grid_spec,
        out_shape=jax.ShapeDtypeStruct((t, d), F32),
        compiler_params=_cparams(("arbitrary",)),
        name="moe_combine",
    )(ce, cb, co, nq, pos, gate, ys)


def _chunk_tables(counts, seg, rows, qmax):
    ne = counts.shape[1]
    ns = (counts + rows - 1) // rows
    cs = jnp.cumsum(ns, axis=1)
    q = jnp.arange(qmax, dtype=jnp.int32)
    ce = jnp.minimum(jnp.sum(q[None, :, None] >= cs[:, None, :], axis=-1), ne - 1).astype(jnp.int32)
    cj = q[None, :] - jnp.take_along_axis(cs - ns, ce, axis=1)
    co = jnp.take_along_axis(seg, ce, axis=1) + cj * rows
    flat = lambda a: a.reshape(-1).astype(jnp.int32)
    return flat(ce), flat(cj * rows), flat(co), cs[:, -1].astype(jnp.int32), qmax


def moe(h, g, w_router, wg, wu, wd, tm=MOE_TILE, bm=MOE_BLOCK):
    t = h.shape[0]
    tm = min(tm, t)
    ne = wg.shape[0]
    nt = t // tm
    un, pos, gate, cnt = router(h, g, w_router.T, tm)
    counts = cnt[:, :, 0]
    padded = (counts + MOE_GRANULE - 1) // MOE_GRANULE * MOE_GRANULE
    tot = jnp.sum(padded, axis=0)
    ptot = (tot + MOE_SLACK + bm - 1) // bm * bm
    eend = jnp.cumsum(ptot)
    seg = (eend - ptot)[None, :] + jnp.cumsum(padded, axis=0) - padded
    n_blocks = (TOP_K * t + nt * ne * (MOE_GRANULE - 1) + ne * MOE_SLACK) // bm + ne
    blk_row0 = jnp.arange(n_blocks, dtype=jnp.int32) * bm
    blk_e = jnp.minimum(jnp.searchsorted(eend, blk_row0, side="right"), ne - 1).astype(jnp.int32)
    blk_rows = jnp.clip((eend - ptot + tot)[blk_e] - blk_row0, 0, bm).astype(jnp.int32)
    g_tab = _chunk_tables(counts, seg, MOE_GATHER_ROWS, TOP_K * tm // MOE_GATHER_ROWS + ne)
    c_tab = _chunk_tables(counts, seg, MOE_COMBINE_ROWS, TOP_K * tm // MOE_COMBINE_ROWS + ne)
    xs = moe_gather(un, pos, g_tab, n_blocks * bm, tm, MOE_GATHER_ROWS)
    ys = moe_ffn(xs, blk_e, blk_rows, wg, wu, wd, bm)
    return moe_combine(ys, pos, gate, c_tab, tm, MOE_COMBINE_ROWS)


def kernel(x, p, ln_mix, ln_ffn, ln_ple, ln_final, lb_table, ab_w_in, ab_conv, b_a_log, b_dt_bias, a_gnorm, b_gnorm, ab_w_out, c_w_in, c_conv_w, c_conv_b, c_w_r, c_b_r, c_w_i, c_b_i, c_lambda, c_w_out, ffn_w_gate, ffn_w_up, ffn_w_down, moe_router, moe_w_gate, moe_w_up, moe_w_down, ple_w_proj, ple_w_gate):
    bsz, seq, d = x.shape
    t = bsz * seq
    depth = ln_mix.shape[0]
    a_heads = lb_table.shape[1] // HEAD_DIM
    b_heads = b_a_log.shape[1]
    a_w = a_heads * HEAD_DIM
    b_w = b_heads * HEAD_DIM
    main_w = 4 * a_w + 4 * b_w
    bf = lambda a: a.astype(BF16)

    h = x.reshape(t, d)
    for layer in range(depth):
        j = layer // 2
        if layer % 2 == 0:
            w_in = ab_w_in[j]
            w_small = jnp.pad(w_in[:, main_w:], ((0, 0), (0, 128 - 2 * b_heads)))
            proj, small = norm_proj(h, ln_mix[layer], bf(w_in[:, :main_w]), bf(w_small))
            mixed = mixer_ab(proj, small, lb_table, a_gnorm[j], ab_conv[j], b_a_log[j], b_dt_bias[j], b_gnorm[j],
                             bsz=bsz, seq=seq, heads_a=a_heads, heads_b=b_heads, layer=layer)
            h = out_proj(mixed, bf(ab_w_out[j]), h)
            h = swiglu(h, ln_ffn[layer], bf(ffn_w_gate[j]), bf(ffn_w_up[j]), bf(ffn_w_down[j]))
            add = None
        else:
            xr, yb = norm_proj_gelu(h, ln_mix[layer], bf(c_w_in[j]))
            hy = rglru(xr, yb, c_conv_w[j], c_conv_b[j], bf(c_w_r[j]), c_b_r[j], bf(c_w_i[j]), c_b_i[j],
                       c_lambda[j], bsz=bsz, seq=seq)
            h = out_proj(hy, bf(c_w_out[j]), h)
            add = moe(h, ln_ffn[layer], moe_router[j], bf(moe_w_gate[j]), bf(moe_w_up[j]), bf(moe_w_down[j]))
        g_final = ln_final if layer == depth - 1 else None
        h = ple(h, p[layer].reshape(t, -1), ln_ple[layer], bf(ple_w_gate[layer]), bf(ple_w_proj[layer]),
                add=add, g_final=g_final)
    if depth == 0:
        raise ValueError("depth must be positive")
    return h.reshape(bsz, seq, d)
```

```python
import functools

import jax
import jax.numpy as jnp
from jax import lax
from jax.experimental import pallas as pl
from jax.experimental.pallas import tpu as pltpu

F32 = jnp.float32
BF16 = jnp.bfloat16
EPS = 1e-6
CHUNK = 64
SUB = 8
HEAD_DIM = 128
CONV_WIDTH = 4
GDN_HEAD_GROUP = 8
MIXER_ROWS = 128
FILL_AFTER_PREP = 4
FILL_AFTER_SCORES = 4
FILL_PER_LEVEL = 1
CONV_TAIL = 8
RGLRU_C = 8.0
SCAN_GROUP = 8
TOP_K = 2
MOE_TILE = 1024
MOE_BLOCK = 1024
MOE_FFN_TILE = 256
MOE_PART_ROWS = 256
MOE_GRANULE = 16
MOE_GATHER_ROWS = 128
MOE_COMBINE_ROWS = 256
MOE_SLACK = 256
VMEM_LIMIT = 56 * 1024 * 1024


def _cparams(sem, vmem=VMEM_LIMIT):
    return pltpu.CompilerParams(dimension_semantics=sem, vmem_limit_bytes=vmem)


def _dot(a, b):
    return jnp.dot(a, b, preferred_element_type=F32)


def _dot_nt(a, b):
    return lax.dot_general(a, b, (((1,), (1,)), ((), ())), preferred_element_type=F32)


def _dot_tn(a, b):
    return lax.dot_general(a, b, (((0,), (0,)), ((), ())), preferred_element_type=F32)


def _split(a):
    hi = a.astype(BF16)
    lo = (a - hi.astype(F32)).astype(BF16)
    return hi, lo


def _dot2(a, b):
    ah, al = _split(a)
    bh = b.astype(BF16)
    return _dot(jnp.concatenate([ah, al], axis=1), jnp.concatenate([bh, bh], axis=0))


def _rms(x, g):
    return x * lax.rsqrt(jnp.mean(x * x, axis=-1, keepdims=True) + EPS) * g


def _sigmoid(x):
    return 1.0 / (1.0 + jnp.exp(-x))


def _silu(x):
    return x * _sigmoid(x)


def _norm_proj_kernel(h_ref, g_ref, w_ref, ws_ref, o_ref, os_ref, un_ref):
    j = pl.program_id(1)

    @pl.when(j == 0)
    def _():
        un = _rms(h_ref[...], g_ref[...]).astype(BF16)
        un_ref[...] = un
        os_ref[...] = _dot(un, ws_ref[...])

    o_ref[...] = _dot(un_ref[...], w_ref[...])


def norm_proj(h, g, w, ws, tm=1024, tn=1024):
    t, d = h.shape
    tm = min(tm, t)
    n = w.shape[1]
    return pl.pallas_call(
        _norm_proj_kernel,
        grid=(t // tm, n // tn),
        in_specs=[pl.BlockSpec((tm, d), lambda i, j: (i, 0)),
                  pl.BlockSpec((1, d), lambda i, j: (0, 0)),
                  pl.BlockSpec((d, tn), lambda i, j: (0, j)),
                  pl.BlockSpec((d, ws.shape[1]), lambda i, j: (0, 0))],
        out_specs=[pl.BlockSpec((tm, tn), lambda i, j: (i, j)),
                   pl.BlockSpec((tm, ws.shape[1]), lambda i, j: (i, 0))],
        out_shape=[jax.ShapeDtypeStruct((t, n), F32), jax.ShapeDtypeStruct((t, ws.shape[1]), F32)],
        scratch_shapes=[pltpu.VMEM((tm, d), BF16)],
        compiler_params=_cparams(("parallel", "arbitrary")),
        name="norm_proj",
    )(h, g.reshape(1, d), w, ws)


def _norm_proj_gelu_kernel(h_ref, g_ref, wy_ref, wx_ref, o_ref, y_ref, un_ref):
    @pl.when(pl.program_id(1) == 0)
    def _():
        un_ref[...] = _rms(h_ref[...], g_ref[...]).astype(BF16)

    un = un_ref[...]
    y_ref[...] = jax.nn.gelu(_dot(un, wy_ref[...])).astype(BF16)
    o_ref[...] = _dot(un, wx_ref[...])


def norm_proj_gelu(h, g, w, tm=1024, tn=512):
    t, d = h.shape
    tm = min(tm, t)
    half = w.shape[1] // 2
    nh = half // tn
    return pl.pallas_call(
        _norm_proj_gelu_kernel,
        grid=(t // tm, nh),
        in_specs=[pl.BlockSpec((tm, d), lambda i, j: (i, 0)),
                  pl.BlockSpec((1, d), lambda i, j: (0, 0)),
                  pl.BlockSpec((d, tn), lambda i, j: (0, j)),
                  pl.BlockSpec((d, tn), lambda i, j: (0, nh + j))],
        out_specs=[pl.BlockSpec((tm, tn), lambda i, j: (i, j)),
                   pl.BlockSpec((tm, tn), lambda i, j: (i, j))],
        out_shape=[jax.ShapeDtypeStruct((t, half), F32), jax.ShapeDtypeStruct((t, half), BF16)],
        scratch_shapes=[pltpu.VMEM((tm, d), BF16)],
        compiler_params=_cparams(("parallel", "arbitrary")),
        name="norm_proj_gelu",
    )(h, g.reshape(1, d), w, w)


def _hgrn2_work(q_ref, f_ref, i_ref, g_ref, lbt_ref, gn_ref, o_ref, st_ref, *, layer, heads):
    c = CHUNK

    lbt = lbt_ref[...]
    e = jnp.exp(lbt - jnp.max(lbt, axis=0, keepdims=True))
    lb_all = jnp.sum(e[:layer + 1], axis=0, keepdims=True) / jnp.sum(e, axis=0, keepdims=True)

    row = lax.broadcasted_iota(jnp.int32, (c, c), 0)
    col = lax.broadcasted_iota(jnp.int32, (c, c), 1)
    tril = jnp.where(row >= col, 1.0, 0.0).astype(BF16)
    gn = gn_ref[...]

    levels = []
    ln = c // 2
    while ln >= SUB:
        levels += [(m * 2 * ln, m * 2 * ln + ln, ln) for m in range(c // (2 * ln))]
        ln //= 2
    n_pairs = sum(l[2] for l in levels)

    def seg_id(idx):
        sid = jnp.zeros_like(idx)
        start = 0
        for l in levels[:-1]:
            start += l[2]
            sid = sid + jnp.where(idx >= start, 1, 0)
        return sid

    same_seg = (seg_id(lax.broadcasted_iota(jnp.int32, (n_pairs, n_pairs), 0))
                == seg_id(lax.broadcasted_iota(jnp.int32, (n_pairs, n_pairs), 1)))
    sub_i = lax.broadcasted_iota(jnp.int32, (c // SUB, SUB, HEAD_DIM), 1)

    sls = [slice(h * HEAD_DIM, (h + 1) * HEAD_DIM) for h in range(heads)]

    def gates(h):
        lb = lb_all[:, sls[h]]
        q = q_ref[:, sls[h]] * (HEAD_DIM ** -0.5)
        forget = lb + (1.0 - lb) * _sigmoid(f_ref[:, sls[h]])
        lh, ll = _split(jnp.log(forget))
        b2 = _dot(tril, jnp.concatenate([lh, ll], axis=1))
        return q, 1.0 - forget, i_ref[:, sls[h]], b2[:, :HEAD_DIM] + b2[:, HEAD_DIM:]

    def block_pairs(h, q, k, v, b):
        o = _dot_nt((q * jnp.exp(b)).astype(BF16), st_ref[h].astype(BF16))
        qs, ks, vs = [], [], []
        for k0, q0, ln in levels:
            bref = b[q0 - 1:q0, :]
            qs.append(q[q0:q0 + ln] * jnp.exp(b[q0:q0 + ln] - bref))
            ks.append(k[k0:k0 + ln] * jnp.exp(bref - b[k0:k0 + ln]))
            vs.append(v[k0:k0 + ln])
        s = _dot_nt(jnp.concatenate(qs, axis=0).astype(BF16), jnp.concatenate(ks, axis=0).astype(BF16))
        r = _dot(jnp.where(same_seg, s, 0.0).astype(BF16), jnp.concatenate(vs, axis=0).astype(BF16))
        groups = [None] * (c // SUB)
        start = 0
        for k0, q0, ln in levels:
            for j in range(ln // SUB):
                piece = r[start + j * SUB:start + (j + 1) * SUB]
                gi = q0 // SUB + j
                groups[gi] = piece if groups[gi] is None else groups[gi] + piece
            start += ln
        groups[0] = jnp.zeros((SUB, HEAD_DIM), F32)
        return o + jnp.concatenate(groups, axis=0)

    def near_pairs(q, k, v, b):
        q3, k3, v3, b3 = (a.reshape(c // SUB, SUB, HEAD_DIM) for a in (q, k, v, b))
        o3 = jnp.sum(q3 * k3, axis=2, keepdims=True) * v3
        for d in range(1, SUB):
            dec = jnp.exp(jnp.where(sub_i >= d, b3 - pltpu.roll(b3, d, 1), -jnp.inf))
            w = jnp.sum(q3 * pltpu.roll(k3, d, 1) * dec, axis=2, keepdims=True)
            o3 = o3 + w * pltpu.roll(v3, d, 1)
        return o3.reshape(c, HEAD_DIM)

    qkvb, far = {}, {}

    def state_part(h):
        qkvb[h] = gates(h)
        q, k, v, b = qkvb[h]
        far[h] = block_pairs(h, q, k, v, b)
        blast = b[c - 1:c, :]
        kd = (k * jnp.exp(blast - b)).astype(BF16)
        st_ref[h] = st_ref[h] * jnp.exp(blast) + _dot_tn(v.astype(BF16), kd)

    def block_part(h):
        o = far[h] + near_pairs(*qkvb[h])
        on = o * lax.rsqrt(jnp.mean(o * o, axis=1, keepdims=True) + EPS) * gn
        o_ref[:, sls[h]] = (on * _silu(g_ref[:, sls[h]])).astype(BF16)

    return ([functools.partial(state_part, h) for h in range(heads)]
            + [functools.partial(block_part, h) for h in range(heads)])


def _causal_conv(ext_ref, x, w, first):
    n = x.shape[0]

    @pl.when(first)
    def _():
        ext_ref[0:CONV_TAIL, :] = jnp.zeros((CONV_TAIL, x.shape[1]), F32)

    ext_ref[CONV_TAIL:CONV_TAIL + n, :] = x
    y = x * w[CONV_WIDTH - 1:CONV_WIDTH, :]
    for k in range(CONV_WIDTH - 1):
        off = CONV_TAIL - (CONV_WIDTH - 1) + k
        y = y + ext_ref[off:off + n, :] * w[k:k + 1, :]
    ext_ref[0:CONV_TAIL, :] = ext_ref[n:n + CONV_TAIL, :]
    return y


def _mixer_ab_kernel(qa_ref, fa_ref, ia_ref, ga_ref, q_ref, k_ref, v_ref, z_ref, sm_ref, lbt_ref, gna_ref, cw_ref,
                     alog_ref, dtb_ref, gn_ref, o_ref, sta_ref, st_ref, eq_ref, ek_ref, ev_ref,
                     *, heads_a, heads, layer):
    c = CHUNK
    first = pl.program_id(1) == 0

    @pl.when(first)
    def _():
        sta_ref[...] = jnp.zeros_like(sta_ref)
        st_ref[...] = jnp.zeros_like(st_ref)

    col0 = heads_a * HEAD_DIM
    hw = heads * HEAD_DIM
    cw = cw_ref[...]
    qc_all = _silu(_causal_conv(eq_ref, q_ref[...], cw[:, 0:hw], first))
    kc_all = _silu(_causal_conv(ek_ref, k_ref[...], cw[:, hw:2 * hw], first))
    vc_all = _silu(_causal_conv(ev_ref, v_ref[...], cw[:, 2 * hw:3 * hw], first))

    row = lax.broadcasted_iota(jnp.int32, (c, c), 0)
    col = lax.broadcasted_iota(jnp.int32, (c, c), 1)
    causal = row >= col
    strict = row > col
    eye = row == col
    eye_f = jnp.where(eye, 1.0, 0.0)
    diag_blk = (row // SUB) == (col // SUB)
    merge_masks = []
    s = SUB
    while s < c:
        merge_masks.append(jnp.logical_and((row // (2 * s)) == (col // (2 * s)), (row // s) == (col // s) + 1))
        s *= 2
    gn = gn_ref[...]

    def to_row(colv):
        return jnp.sum(jnp.where(eye, colv, 0.0), axis=0, keepdims=True)

    def head_group(hg, qc, kc, vc, sm, z_ref, o_ref, fill):
        hs = range(len(hg))
        sls = [slice(h * HEAD_DIM, (h + 1) * HEAD_DIM) for h in hg]
        qn, kn, knb, beta, gam_col, egam, decay, kbeta = [], [], [], [], [], [], [], []
        for i, h in enumerate(hg):
            qh, kh = qc[:, sls[i]], kc[:, sls[i]]
            qn.append(qh * lax.rsqrt(jnp.sum(qh * qh, axis=1, keepdims=True) + EPS) * (HEAD_DIM ** -0.5))
            kn.append(kh * lax.rsqrt(jnp.sum(kh * kh, axis=1, keepdims=True) + EPS))
            knb.append(kn[i].astype(BF16))
            beta.append(_sigmoid(sm[:, heads + h:heads + h + 1]))
            g_col = -jnp.exp(alog_ref[:, h:h + 1]) * jax.nn.softplus(sm[:, h:h + 1] + dtb_ref[:, h:h + 1])
            g_row = to_row(g_col)
            gam_col.append(jnp.sum(jnp.where(causal, g_row, 0.0), axis=1, keepdims=True))
            gam_row = jnp.sum(jnp.where(strict, 0.0, g_col), axis=0, keepdims=True)
            decay.append(jnp.exp(jnp.where(causal, gam_col[i] - gam_row, -jnp.inf)))
            egam.append(jnp.exp(gam_col[i]))
            kbeta.append(kn[i] * beta[i])

        fill(FILL_AFTER_PREP)
        a_mat = [jnp.where(strict, _dot_nt(kbeta[i].astype(BF16), knb[i]) * decay[i], 0.0) for i in hs]
        qk = [(_dot_nt(qn[i].astype(BF16), knb[i]) * decay[i]).astype(BF16) for i in hs]
        fill(FILL_AFTER_SCORES)
        d_mat = [jnp.where(diag_blk, a_mat[i], 0.0) for i in hs]
        x = [eye_f - d_mat[i] for i in hs]
        p = [_dot2(d_mat[i], d_mat[i]) for i in hs]
        fill(FILL_PER_LEVEL)
        n_sq = (SUB - 1).bit_length() - 1
        for lvl in range(n_sq):
            if lvl < n_sq - 1:
                y = [_dot2(jnp.concatenate([x[i], p[i]], axis=0), p[i]) for i in hs]
                fill(FILL_PER_LEVEL)
                x = [x[i] + y[i][:c] for i in hs]
                p = [y[i][c:] for i in hs]
            else:
                y = [_dot2(x[i], p[i]) for i in hs]
                fill(FILL_PER_LEVEL)
                x = [x[i] + y[i] for i in hs]
        for below in merge_masks:
            y = [_dot2(x[i], jnp.where(below, a_mat[i], 0.0)) for i in hs]
            fill(FILL_PER_LEVEL)
            y = [_dot2(y[i], x[i]) for i in hs]
            fill(FILL_PER_LEVEL)
            x = [x[i] - y[i] for i in hs]
        rhs = [jnp.concatenate([vc[:, sls[i]] * beta[i], kbeta[i] * egam[i]], axis=1).astype(BF16) for i in hs]
        uw = [_dot(x[i].astype(BF16), rhs[i]) for i in hs]

        st = [st_ref[h] for h in hg]
        stb = [s.astype(BF16) for s in st]
        v_new = [uw[i][:, :HEAD_DIM] - _dot_nt(uw[i][:, HEAD_DIM:].astype(BF16), stb[i]) for i in hs]
        vnb = [v.astype(BF16) for v in v_new]
        o = [_dot_nt((qn[i] * egam[i]).astype(BF16), stb[i]) + _dot(qk[i], vnb[i]) for i in hs]
        for i, h in enumerate(hg):
            glast = gam_col[i][c - 1:c, :]
            kd = (kn[i] * jnp.exp(glast - gam_col[i])).astype(BF16)
            st_ref[h] = st[i] * jnp.exp(glast) + _dot_tn(vnb[i], kd)
        for i in hs:
            on = o[i] * lax.rsqrt(jnp.mean(o[i] * o[i], axis=1, keepdims=True) + EPS) * gn
            o_ref[:, col0 + hg[i] * HEAD_DIM:col0 + (hg[i] + 1) * HEAD_DIM] = (on * _silu(z_ref[:, sls[i]])).astype(BF16)

    for ci in range(q_ref.shape[0] // c):
        rs = pl.ds(ci * c, c)
        pending = _hgrn2_work(qa_ref.at[rs], fa_ref.at[rs], ia_ref.at[rs], ga_ref.at[rs], lbt_ref, gna_ref,
                              o_ref.at[rs], sta_ref, layer=layer, heads=heads_a)

        def fill(n, pending=pending):
            for _ in range(min(n, len(pending))):
                pending.pop(0)()

        r0 = ci * c
        for h0 in range(0, heads, GDN_HEAD_GROUP):
            head_group(list(range(h0, min(h0 + GDN_HEAD_GROUP, heads))), qc_all[r0:r0 + c], kc_all[r0:r0 + c],
                       vc_all[r0:r0 + c], sm_ref[rs, :], z_ref.at[rs], o_ref.at[rs], fill)
        fill(len(pending))


def mixer_ab(proj, small, lb_table, gnorm_a, conv_w, a_log, dt_bias, gnorm_b, *, bsz, seq, heads_a, heads_b, layer):
    t = proj.shape[0]
    wa, wb = heads_a * HEAD_DIM, heads_b * HEAD_DIM
    rows = min(MIXER_ROWS, seq)
    nc = seq // rows
    row = lambda b, s: b * nc + s

    def spec_a(k):
        return pl.BlockSpec((rows, wa), lambda b, s: (row(b, s), k))

    def spec_b(k):
        return pl.BlockSpec((rows, wb), lambda b, s: (row(b, s), 4 * wa // wb + k))

    def const(shape):
        return pl.BlockSpec(shape, lambda b, s: (0, 0))

    return pl.pallas_call(
        functools.partial(_mixer_ab_kernel, heads_a=heads_a, heads=heads_b, layer=layer),
        grid=(bsz, nc),
        in_specs=[spec_a(0), spec_a(1), spec_a(2), spec_a(3), spec_b(0), spec_b(1), spec_b(2), spec_b(3),
                  pl.BlockSpec((rows, small.shape[1]), lambda b, s: (row(b, s), 0)),
                  const(lb_table.shape), const((1, HEAD_DIM)),
                  const(conv_w.shape), const((1, heads_b)), const((1, heads_b)), const((1, HEAD_DIM))],
        out_specs=pl.BlockSpec((rows, wa + wb), lambda b, s: (row(b, s), 0)),
        out_shape=jax.ShapeDtypeStruct((t, wa + wb), BF16),
        scratch_shapes=[pltpu.VMEM((heads_a, HEAD_DIM, HEAD_DIM), F32), pltpu.VMEM((heads_b, HEAD_DIM, HEAD_DIM), F32)]
        + [pltpu.VMEM((rows + CONV_TAIL, wb), F32)] * 3,
        compiler_params=_cparams(("parallel", "arbitrary")),
        name="mixer_ab",
    )(proj, proj, proj, proj, proj, proj, proj, proj, small, lb_table, gnorm_a.reshape(1, HEAD_DIM), conv_w,
      a_log.reshape(1, heads_b), dt_bias.reshape(1, heads_b), gnorm_b.reshape(1, HEAD_DIM))


def _out_proj_kernel(a_ref, w_ref, h_ref, o_ref):
    o_ref[...] = h_ref[...] + _dot(a_ref[...], w_ref[...])


def out_proj(a, w, h, tm=1024, tn=1024):
    t, k = a.shape
    tm = min(tm, t)
    n = w.shape[1]
    return pl.pallas_call(
        _out_proj_kernel,
        grid=(t // tm, n // tn),
        in_specs=[pl.BlockSpec((tm, k), lambda i, j: (i, 0)),
                  pl.BlockSpec((k, tn), lambda i, j: (0, j)),
                  pl.BlockSpec((tm, tn), lambda i, j: (i, j))],
        out_specs=pl.BlockSpec((tm, tn), lambda i, j: (i, j)),
        out_shape=jax.ShapeDtypeStruct((t, n), F32),
        compiler_params=_cparams(("parallel", "arbitrary")),
        name="out_proj",
    )(a, w, h)


def _swiglu_kernel(h_ref, g_ref, wg_ref, wu_ref, wd_ref, o_ref, un_ref):
    f = pl.program_id(1)

    @pl.when(f == 0)
    def _():
        un_ref[...] = _rms(h_ref[...], g_ref[...]).astype(BF16)
        o_ref[...] = h_ref[...]

    un = un_ref[...]
    hb = (_silu(_dot(un, wg_ref[...])) * _dot(un, wu_ref[...])).astype(BF16)
    o_ref[...] += _dot(hb, wd_ref[...])


def swiglu(h, g, wg, wu, wd, tm=1024, tf=512):
    t, d = h.shape
    tm = min(tm, t)
    ff = wg.shape[1]
    return pl.pallas_call(
        _swiglu_kernel,
        grid=(t // tm, ff // tf),
        in_specs=[pl.BlockSpec((tm, d), lambda i, f: (i, 0), pipeline_mode=pl.Buffered(1)),
                  pl.BlockSpec((1, d), lambda i, f: (0, 0)),
                  pl.BlockSpec((d, tf), lambda i, f: (0, f)),
                  pl.BlockSpec((d, tf), lambda i, f: (0, f)),
                  pl.BlockSpec((tf, d), lambda i, f: (f, 0))],
        out_specs=pl.BlockSpec((tm, d), lambda i, f: (i, 0)),
        out_shape=jax.ShapeDtypeStruct((t, d), F32),
        scratch_shapes=[pltpu.VMEM((tm, d), BF16)],
        compiler_params=_cparams(("parallel", "arbitrary")),
        name="swiglu",
    )(h, g.reshape(1, d), wg, wu, wd)


def _ple_kernel(*refs, has_add, has_final):
    h_ref, p_ref, g_ref, wg_ref, wp_ref = refs[:5]
    k = 5
    add_ref = gf_ref = None
    if has_add:
        add_ref = refs[k]
        k += 1
    if has_final:
        gf_ref = refs[k]
        k += 1
    o_ref = refs[k]
    h = h_ref[...]
    if has_add:
        h = h + add_ref[...]
    un = _rms(h, g_ref[...]).astype(BF16)
    gate = _sigmoid(_dot(un, wg_ref[...]))
    out = h + gate * _dot(p_ref[...].astype(BF16), wp_ref[...])
    if has_final:
        out = _rms(out, gf_ref[...])
    o_ref[...] = out


def ple(h, p, g, wg, wp, add=None, g_final=None, tm=512):
    t, d = h.shape
    tm = min(tm, t)
    pd = p.shape[1]
    row = lambda i: (i, 0)
    const = lambda i: (0, 0)
    in_specs = [pl.BlockSpec((tm, d), row), pl.BlockSpec((tm, pd), row), pl.BlockSpec((1, d), const),
                pl.BlockSpec((d, d), const), pl.BlockSpec((pd, d), const)]
    args = [h, p, g.reshape(1, d), wg, wp]
    if add is not None:
        in_specs.append(pl.BlockSpec((tm, d), row))
        args.append(add)
    if g_final is not None:
        in_specs.append(pl.BlockSpec((1, d), const))
        args.append(g_final.reshape(1, d))
    return pl.pallas_call(
        functools.partial(_ple_kernel, has_add=add is not None, has_final=g_final is not None),
        grid=(t // tm,),
        in_specs=in_specs,
        out_specs=pl.BlockSpec((tm, d), row),
        out_shape=jax.ShapeDtypeStruct((t, d), F32),
        compiler_params=_cparams(("parallel",)),
        name="ple",
    )(*args)


def _rglru_kernel(x_ref, y_ref, cw_ref, cb_ref, wr_ref, br_ref, wi_ref, bi_ref, lam_ref, o_ref,
                  ext_ref, hc_ref, *, blocks):
    n = x_ref.shape[0]
    first = pl.program_id(1) == 0

    @pl.when(first)
    def _():
        hc_ref[...] = jnp.zeros_like(hc_ref)

    xc = _causal_conv(ext_ref, x_ref[...], cw_ref[...], first) + cb_ref[...]
    bw = xc.shape[1] // blocks
    rowi = lax.broadcasted_iota(jnp.int32, (n, bw), 0)
    at_start = jnp.logical_and(first, rowi == 0)
    gidx = lax.broadcasted_iota(jnp.int32, (n // SCAN_GROUP, SCAN_GROUP, bw), 1)

    for nb in range(blocks):
        sl = slice(nb * bw, (nb + 1) * bw)
        xb = xc[:, sl]
        xbb = xb.astype(BF16)
        r = _sigmoid(_dot(xbb, wr_ref[nb]) + br_ref[:, sl])
        gi = _sigmoid(_dot(xbb, wi_ref[nb]) + bi_ref[:, sl])
        log_a = -RGLRU_C * r * jax.nn.softplus(-lam_ref[:, sl])
        a = jnp.exp(log_a)
        m2 = 1.0 - a * a
        mult = jnp.where(m2 > 0.0, m2 * lax.rsqrt(m2), 0.0)
        mult = jnp.where(at_start, 1.0, mult)
        b = mult * gi * xb
        a = a.reshape(n // SCAN_GROUP, SCAN_GROUP, bw)
        b = b.reshape(n // SCAN_GROUP, SCAN_GROUP, bw)
        sh = 1
        while sh < SCAN_GROUP:
            ok = gidx >= sh
            a_prev = jnp.where(ok, pltpu.roll(a, sh, 1), 1.0)
            b_prev = jnp.where(ok, pltpu.roll(b, sh, 1), 0.0)
            b = b + a * b_prev
            a = a * a_prev
            sh *= 2
        carry = hc_ref[:, sl]
        groups = []
        for gi_ in range(n // SCAN_GROUP):
            hg = b[gi_] + a[gi_] * carry
            groups.append(hg)
            carry = hg[SCAN_GROUP - 1:SCAN_GROUP, :]
        hseq = jnp.concatenate(groups, axis=0)
        hc_ref[:, sl] = carry
        o_ref[:, sl] = (hseq * y_ref[:, sl].astype(F32)).astype(BF16)


def rglru(xr, y, conv_w, conv_b, w_r, b_r, w_i, b_i, lam, *, bsz, seq, rows=256):
    t, cwid = xr.shape
    blocks = w_r.shape[0]
    ns = seq // rows
    row = lambda b, s: (b * ns + s, 0)
    c2 = lambda b, s: (0, 0)
    c3 = lambda b, s: (0, 0, 0)
    vec = lambda a: a.reshape(1, cwid)
    return pl.pallas_call(
        functools.partial(_rglru_kernel, blocks=blocks),
        grid=(bsz, ns),
        in_specs=[pl.BlockSpec((rows, cwid), row), pl.BlockSpec((rows, cwid), row),
                  pl.BlockSpec(conv_w.shape, c2), pl.BlockSpec((1, cwid), c2),
                  pl.BlockSpec(w_r.shape, c3), pl.BlockSpec((1, cwid), c2),
                  pl.BlockSpec(w_i.shape, c3), pl.BlockSpec((1, cwid), c2),
                  pl.BlockSpec((1, cwid), c2)],
        out_specs=pl.BlockSpec((rows, cwid), row),
        out_shape=jax.ShapeDtypeStruct((t, cwid), BF16),
        scratch_shapes=[pltpu.VMEM((rows + CONV_TAIL, cwid), F32), pltpu.VMEM((1, cwid), F32)],
        compiler_params=_cparams(("parallel", "arbitrary")),
        name="rglru",
    )(xr, y, conv_w, vec(conv_b), w_r, vec(b_r), w_i, vec(b_i), vec(lam))


def _router_kernel(h_ref, g_ref, wr_ref, un_ref, pos_ref, gate_ref, cnt_ref):
    tm = h_ref.shape[0]
    ne = wr_ref.shape[0]
    un = _rms(h_ref[...], g_ref[...])
    uh, ul = _split(un)
    un_ref[...] = uh
    wh, wl = _split(wr_ref[...])
    logits = _dot_nt(wh, uh) + _dot_nt(wh, ul) + _dot_nt(wl, uh)
    eidx = lax.broadcasted_iota(jnp.int32, (ne, tm), 0).astype(F32)
    m1 = jnp.max(logits, axis=0, keepdims=True)
    i1 = jnp.min(jnp.where(logits == m1, eidx, float(ne)), axis=0, keepdims=True)
    mask1 = eidx == i1
    rest = jnp.where(mask1, -jnp.inf, logits)
    m2 = jnp.max(rest, axis=0, keepdims=True)
    i2 = jnp.min(jnp.where(rest == m2, eidx, float(ne)), axis=0, keepdims=True)
    mask2 = eidx == i2
    e2 = jnp.exp(m2 - m1)
    g1 = 1.0 / (1.0 + e2)
    g2 = e2 / (1.0 + e2)
    gate_ref[...] = jnp.where(mask1, g1, jnp.where(mask2, g2, 0.0))
    sel = jnp.logical_or(mask1, mask2)
    self32 = jnp.where(sel, 1.0, 0.0)
    before = lax.broadcasted_iota(jnp.int32, (tm, tm), 0) < lax.broadcasted_iota(jnp.int32, (tm, tm), 1)
    rank = _dot(self32.astype(BF16), jnp.where(before, 1.0, 0.0).astype(BF16))
    pos_ref[...] = jnp.where(sel, rank, -1.0)
    cnt = jnp.sum(self32, axis=1, keepdims=True).astype(jnp.int32)
    cnt_ref[0] = jnp.broadcast_to(cnt, cnt_ref.shape[1:])


def router(h, g, wr_t, tm):
    t, d = h.shape
    tm = min(tm, t)
    ne = wr_t.shape[0]
    nt = t // tm
    return pl.pallas_call(
        _router_kernel,
        grid=(nt,),
        in_specs=[pl.BlockSpec((tm, d), lambda i: (i, 0)),
                  pl.BlockSpec((1, d), lambda i: (0, 0)),
                  pl.BlockSpec((ne, d), lambda i: (0, 0))],
        out_specs=[pl.BlockSpec((tm, d), lambda i: (i, 0)),
                   pl.BlockSpec((ne, tm), lambda i: (0, i)),
                   pl.BlockSpec((ne, tm), lambda i: (0, i)),
                   pl.BlockSpec((1, ne, 128), lambda i: (i, 0, 0))],
        out_shape=[jax.ShapeDtypeStruct((t, d), BF16), jax.ShapeDtypeStruct((ne, t), F32),
                   jax.ShapeDtypeStruct((ne, t), F32), jax.ShapeDtypeStruct((nt, ne, 128), jnp.int32)],
        compiler_params=_cparams(("parallel",)),
        name="moe_router",
    )(h, g.reshape(1, d), wr_t)


def _pick(pos, base, rows):
    slot = lax.broadcasted_iota(jnp.int32, (rows, pos.shape[1]), 0).astype(F32)
    return pos == slot + base.astype(F32)


def _moe_gather_kernel(ce_ref, cb_ref, co_ref, nq_ref, un_ref, pos_ref, xs_in_ref, xs_ref, buf_ref, sem, *, rows, qmax):
    del xs_in_ref
    i = pl.program_id(0)
    n = nq_ref[i]

    def copy(slot, off):
        return pltpu.make_async_copy(buf_ref.at[slot], xs_ref.at[pl.ds(off, rows)], sem.at[slot])

    def body(q, carry):
        k = i * qmax + q
        slot = lax.rem(q, 2)

        @pl.when(q >= 2)
        def _():
            copy(slot, 0).wait()

        pos = pos_ref[pl.ds(ce_ref[k], 1), :]
        sel = jnp.where(_pick(pos, cb_ref[k], rows), 1.0, 0.0).astype(BF16)
        buf_ref[slot] = _dot(sel, un_ref[...]).astype(BF16)
        copy(slot, pl.multiple_of(co_ref[k], MOE_GRANULE)).start()
        return carry

    lax.fori_loop(0, n, body, 0)

    @pl.when(n >= 2)
    def _():
        copy(lax.rem(n, 2), 0).wait()

    @pl.when(n >= 1)
    def _():
        copy(lax.rem(n + 1, 2), 0).wait()


def moe_gather(un, pos, tables, n_rows, tm, rows):
    t, d = un.shape
    ne = pos.shape[0]
    ce, cb, co, nq, qmax = tables
    grid_spec = pltpu.PrefetchScalarGridSpec(
        num_scalar_prefetch=4,
        grid=(t // tm,),
        in_specs=[pl.BlockSpec((tm, d), lambda i, *_: (i, 0)),
                  pl.BlockSpec((ne, tm), lambda i, *_: (0, i)),
                  pl.BlockSpec(memory_space=pl.ANY)],
        out_specs=pl.BlockSpec(memory_space=pl.ANY),
        scratch_shapes=[pltpu.VMEM((2, rows, d), BF16), pltpu.SemaphoreType.DMA((2,))],
    )
    return pl.pallas_call(
        functools.partial(_moe_gather_kernel, rows=rows, qmax=qmax),
        grid_spec=grid_spec,
        out_shape=jax.ShapeDtypeStruct((n_rows, d), BF16),
        input_output_aliases={6: 0},
        compiler_params=_cparams(("arbitrary",)),
        name="moe_gather",
    )(ce, cb, co, nq, un, pos, jnp.zeros((n_rows, d), BF16))


def _moe_ffn_kernel(be_ref, nv_ref, x_ref, wg_ref, wu_ref, wd_ref, o_ref, acc_ref):
    del be_ref
    b, f = pl.program_id(0), pl.program_id(1)
    nf = pl.num_programs(1)
    bm = x_ref.shape[0]
    nv = nv_ref[b]

    @pl.when(f == 0)
    def _():
        acc_ref[...] = jnp.zeros_like(acc_ref)

    def ffn(rows):
        x = x_ref[rows, :]
        wg, wu, wd = (w_ref[0].astype(BF16) for w_ref in (wg_ref, wu_ref, wd_ref))
        hb = (_silu(_dot(x, wg)) * _dot(x, wu)).astype(BF16)
        acc_ref[rows, :] += _dot(hb, wd)

    @pl.when(nv > bm - MOE_PART_ROWS)
    def _():
        ffn(pl.ds(0, bm))

    for part in range(bm // MOE_PART_ROWS):
        @pl.when(jnp.logical_and(nv <= bm - MOE_PART_ROWS, nv > part * MOE_PART_ROWS))
        def _():
            ffn(pl.ds(part * MOE_PART_ROWS, MOE_PART_ROWS))

    @pl.when(f == nf - 1)
    def _():
        o_ref[...] = acc_ref[...].astype(BF16)


def moe_ffn(xs, blk_e, blk_rows, wg, wu, wd, bm, tf=MOE_FFN_TILE):
    n_rows, d = xs.shape
    ff = wg.shape[2]
    nf = ff // tf

    def fidx(b, f, nv):
        return jnp.where(nv[b] > 0, f, nf - 1)

    grid_spec = pltpu.PrefetchScalarGridSpec(
        num_scalar_prefetch=2,
        grid=(n_rows // bm, nf),
        in_specs=[pl.BlockSpec((bm, d), lambda b, f, be, nv: (b, 0)),
                  pl.BlockSpec((1, d, tf), lambda b, f, be, nv: (be[b], 0, fidx(b, f, nv))),
                  pl.BlockSpec((1, d, tf), lambda b, f, be, nv: (be[b], 0, fidx(b, f, nv))),
                  pl.BlockSpec((1, tf, d), lambda b, f, be, nv: (be[b], fidx(b, f, nv), 0))],
        out_specs=pl.BlockSpec((bm, d), lambda b, f, be, nv: (b, 0)),
        scratch_shapes=[pltpu.VMEM((bm, d), F32)],
    )
    return pl.pallas_call(
        _moe_ffn_kernel,
        grid_spec=grid_spec,
        out_shape=jax.ShapeDtypeStruct((n_rows, d), BF16),
        compiler_params=_cparams(("parallel", "arbitrary")),
        name="moe_ffn",
    )(blk_e, blk_rows, xs, wg, wu, wd)


def _moe_combine_kernel(ce_ref, cb_ref, co_ref, nq_ref, pos_ref, gate_ref, y_ref, o_ref, buf_ref, sem, *, rows, qmax):
    i = pl.program_id(0)
    n = nq_ref[i]
    o_ref[...] = jnp.zeros_like(o_ref)

    def copy(slot, off):
        return pltpu.make_async_copy(y_ref.at[pl.ds(off, rows)], buf_ref.at[slot], sem.at[slot])

    def start(q):
        copy(lax.rem(q, 2), pl.multiple_of(co_ref[i * qmax + q], MOE_GRANULE)).start()

    @pl.when(n > 0)
    def _():
        start(0)

    def body(q, carry):
        k = i * qmax + q
        slot = lax.rem(q, 2)

        @pl.when(q + 1 < n)
        def _():
            start(q + 1)

        copy(slot, 0).wait()
        e = ce_ref[k]
        hit = _pick(pos_ref[pl.ds(e, 1), :], cb_ref[k], rows)
        gsub = jnp.sum(jnp.where(hit, gate_ref[pl.ds(e, 1), :], 0.0), axis=1, keepdims=True)
        yb = (buf_ref[slot].astype(F32) * gsub).astype(BF16)
        o_ref[...] += _dot_tn(jnp.where(hit, 1.0, 0.0).astype(BF16), yb)
        return carry

    lax.fori_loop(0, n, body, 0)


def moe_combine(ys, pos, gate, tables, tm, rows):
    ne, t = pos.shape
    d = ys.shape[1]
    ce, cb, co, nq, qmax = tables
    grid_spec = pltpu.PrefetchScalarGridSpec(
        num_scalar_prefetch=4,
        grid=(t // tm,),
        in_specs=[pl.BlockSpec((ne, tm), lambda i, *_: (0, i)),
                  pl.BlockSpec((ne, tm), lambda i, *_: (0, i)),
                  pl.BlockSpec(memory_space=pl.ANY)],
        out_specs=pl.BlockSpec((tm, d), lambda i, *_: (i, 0)),
        scratch_shapes=[pltpu.VMEM((2, rows, d), BF16), pltpu.SemaphoreType.DMA((2,))],
    )
    return pl.pallas_call(
        functools.partial(_moe_combine_kernel, rows=rows, qmax=qmax),
        grid_spec=grid_spec,
        out_shape=jax.ShapeDtypeStruct((t, d), F32),
        compiler_params=_cparams(("arbitrary",)),
        name="moe_combine",
    )(ce, cb, co, nq, pos, gate, ys)


def _chunk_tables(counts, seg, rows, qmax):
    ne = counts.shape[1]
    ns = (counts + rows - 1) // rows
    cs = jnp.cumsum(ns, axis=1)
    q = jnp.arange(qmax, dtype=jnp.int32)
    ce = jnp.minimum(jnp.sum(q[None, :, None] >= cs[:, None, :], axis=-1), ne - 1).astype(jnp.int32)
    cj = q[None, :] - jnp.take_along_axis(cs - ns, ce, axis=1)
    co = jnp.take_along_axis(seg, ce, axis=1) + cj * rows
    flat = lambda a: a.reshape(-1).astype(jnp.int32)
    return flat(ce), flat(cj * rows), flat(co), cs[:, -1].astype(jnp.int32), qmax


def moe(h, g, w_router, wg, wu, wd, tm=MOE_TILE, bm=MOE_BLOCK):
    t = h.shape[0]
    tm = min(tm, t)
    ne = wg.shape[0]
    nt = t // tm
    un, pos, gate, cnt = router(h, g, w_router.T, tm)
    counts = cnt[:, :, 0]
    padded = (counts + MOE_GRANULE - 1) // MOE_GRANULE * MOE_GRANULE
    tot = jnp.sum(padded, axis=0)
    ptot = (tot + MOE_SLACK + bm - 1) // bm * bm
    eend = jnp.cumsum(ptot)
    seg = (eend - ptot)[None, :] + jnp.cumsum(padded, axis=0) - padded
    n_blocks = (TOP_K * t + nt * ne * (MOE_GRANULE - 1) + ne * MOE_SLACK) // bm + ne
    blk_row0 = jnp.arange(n_blocks, dtype=jnp.int32) * bm
    blk_e = jnp.minimum(jnp.searchsorted(eend, blk_row0, side="right"), ne - 1).astype(jnp.int32)
    blk_rows = jnp.clip((eend - ptot + tot)[blk_e] - blk_row0, 0, bm).astype(jnp.int32)
    g_tab = _chunk_tables(counts, seg, MOE_GATHER_ROWS, TOP_K * tm // MOE_GATHER_ROWS + ne)
    c_tab = _chunk_tables(counts, seg, MOE_COMBINE_ROWS, TOP_K * tm // MOE_COMBINE_ROWS + ne)
    xs = moe_gather(un, pos, g_tab, n_blocks * bm, tm, MOE_GATHER_ROWS)
    ys = moe_ffn(xs, blk_e, blk_rows, wg, wu, wd, bm)
    return moe_combine(ys, pos, gate, c_tab, tm, MOE_COMBINE_ROWS)


def kernel(x, p, ln_mix, ln_ffn, ln_ple, ln_final, lb_table, ab_w_in, ab_conv, b_a_log, b_dt_bias, a_gnorm, b_gnorm, ab_w_out, c_w_in, c_conv_w, c_conv_b, c_w_r, c_b_r, c_w_i, c_b_i, c_lambda, c_w_out, ffn_w_gate, ffn_w_up, ffn_w_down, moe_router, moe_w_gate, moe_w_up, moe_w_down, ple_w_proj, ple_w_gate):
    bsz, seq, d = x.shape
    t = bsz * seq
    depth = ln_mix.shape[0]
    a_heads = lb_table.shape[1] // HEAD_DIM
    b_heads = b_a_log.shape[1]
    a_w = a_heads * HEAD_DIM
    b_w = b_heads * HEAD_DIM
    main_w = 4 * a_w + 4 * b_w
    bf = lambda a: a.astype(BF16)

    h = x.reshape(t, d)
    for layer in range(depth):
        j = layer // 2
        if layer % 2 == 0:
            w_in = ab_w_in[j]
            w_small = jnp.pad(w_in[:, main_w:], ((0, 0), (0, 128 - 2 * b_heads)))
            proj, small = norm_proj(h, ln_mix[layer], bf(w_in[:, :main_w]), bf(w_small))
            mixed = mixer_ab(proj, small, lb_table, a_gnorm[j], ab_conv[j], b_a_log[j], b_dt_bias[j], b_gnorm[j],
                             bsz=bsz, seq=seq, heads_a=a_heads, heads_b=b_heads, layer=layer)
            h = out_proj(mixed, bf(ab_w_out[j]), h)
            h = swiglu(h, ln_ffn[layer], bf(ffn_w_gate[j]), bf(ffn_w_up[j]), bf(ffn_w_down[j]))
            add = None
        else:
            xr, yb = norm_proj_gelu(h, ln_mix[layer], bf(c_w_in[j]))
            hy = rglru(xr, yb, c_conv_w[j], c_conv_b[j], bf(c_w_r[j]), c_b_r[j], bf(c_w_i[j]), c_b_i[j],
                       c_lambda[j], bsz=bsz, seq=seq)
            h = out_proj(hy, bf(c_w_out[j]), h)
            add = moe(h, ln_ffn[layer], moe_router[j], moe_w_gate[j], moe_w_up[j], moe_w_down[j])
        g_final = ln_final if layer == depth - 1 else None
        h = ple(h, p[layer].reshape(t, -1), ln_ple[layer], bf(ple_w_gate[layer]), bf(ple_w_proj[layer]),
                add=add, g_final=g_final)
    if depth == 0:
        raise ValueError("depth must be positive")
    return h.reshape(bsz, seq, d)
```

```python
import functools

import jax
import jax.numpy as jnp
from jax import lax
from jax.experimental import pallas as pl
from jax.experimental.pallas import tpu as pltpu

F32 = jnp.float32
BF16 = jnp.bfloat16
EPS = 1e-6
CHUNK = 64
SUB = 8
HEAD_DIM = 128
CONV_WIDTH = 4
GDN_HEAD_GROUP = 8
MIXER_ROWS = 128
FILL_AFTER_PREP = 4
FILL_AFTER_SCORES = 4
FILL_PER_LEVEL = 1
CONV_TAIL = 8
RGLRU_C = 8.0
SCAN_GROUP = 8
TOP_K = 2
MOE_TILE = 1024
MOE_BLOCK = 1024
MOE_FFN_TILE = 512
MOE_PART_ROWS = 256
MOE_GRANULE = 16
MOE_GATHER_ROWS = 128
MOE_COMBINE_ROWS = 256
MOE_SLACK = 256
VMEM_LIMIT = 56 * 1024 * 1024


def _cparams(sem, vmem=VMEM_LIMIT):
    return pltpu.CompilerParams(dimension_semantics=sem, vmem_limit_bytes=vmem)


def _dot(a, b):
    return jnp.dot(a, b, preferred_element_type=F32)


def _dot_nt(a, b):
    return lax.dot_general(a, b, (((1,), (1,)), ((), ())), preferred_element_type=F32)


def _dot_tn(a, b):
    return lax.dot_general(a, b, (((0,), (0,)), ((), ())), preferred_element_type=F32)


def _split(a):
    hi = a.astype(BF16)
    lo = (a - hi.astype(F32)).astype(BF16)
    return hi, lo


def _dot2(a, b):
    ah, al = _split(a)
    bh = b.astype(BF16)
    return _dot(jnp.concatenate([ah, al], axis=1), jnp.concatenate([bh, bh], axis=0))


def _rms(x, g):
    return x * lax.rsqrt(jnp.mean(x * x, axis=-1, keepdims=True) + EPS) * g


def _sigmoid(x):
    return 1.0 / (1.0 + jnp.exp(-x))


def _silu(x):
    return x * _sigmoid(x)


def _norm_proj_kernel(h_ref, g_ref, w_ref, ws_ref, o_ref, os_ref, un_ref):
    j = pl.program_id(1)

    @pl.when(j == 0)
    def _():
        un = _rms(h_ref[...], g_ref[...]).astype(BF16)
        un_ref[...] = un
        os_ref[...] = _dot(un, ws_ref[...])

    o_ref[...] = _dot(un_ref[...], w_ref[...])


def norm_proj(h, g, w, ws, tm=1024, tn=1024):
    t, d = h.shape
    tm = min(tm, t)
    n = w.shape[1]
    return pl.pallas_call(
        _norm_proj_kernel,
        grid=(t // tm, n // tn),
        in_specs=[pl.BlockSpec((tm, d), lambda i, j: (i, 0)),
                  pl.BlockSpec((1, d), lambda i, j: (0, 0)),
                  pl.BlockSpec((d, tn), lambda i, j: (0, j)),
                  pl.BlockSpec((d, ws.shape[1]), lambda i, j: (0, 0))],
        out_specs=[pl.BlockSpec((tm, tn), lambda i, j: (i, j)),
                   pl.BlockSpec((tm, ws.shape[1]), lambda i, j: (i, 0))],
        out_shape=[jax.ShapeDtypeStruct((t, n), F32), jax.ShapeDtypeStruct((t, ws.shape[1]), F32)],
        scratch_shapes=[pltpu.VMEM((tm, d), BF16)],
        compiler_params=_cparams(("parallel", "arbitrary")),
        name="norm_proj",
    )(h, g.reshape(1, d), w, ws)


def _norm_proj_gelu_kernel(h_ref, g_ref, wy_ref, wx_ref, o_ref, y_ref, un_ref):
    @pl.when(pl.program_id(1) == 0)
    def _():
        un_ref[...] = _rms(h_ref[...], g_ref[...]).astype(BF16)

    un = un_ref[...]
    y_ref[...] = jax.nn.gelu(_dot(un, wy_ref[...])).astype(BF16)
    o_ref[...] = _dot(un, wx_ref[...])


def norm_proj_gelu(h, g, w, tm=1024, tn=512):
    t, d = h.shape
    tm = min(tm, t)
    half = w.shape[1] // 2
    nh = half // tn
    return pl.pallas_call(
        _norm_proj_gelu_kernel,
        grid=(t // tm, nh),
        in_specs=[pl.BlockSpec((tm, d), lambda i, j: (i, 0)),
                  pl.BlockSpec((1, d), lambda i, j: (0, 0)),
                  pl.BlockSpec((d, tn), lambda i, j: (0, j)),
                  pl.BlockSpec((d, tn), lambda i, j: (0, nh + j))],
        out_specs=[pl.BlockSpec((tm, tn), lambda i, j: (i, j)),
                   pl.BlockSpec((tm, tn), lambda i, j: (i, j))],
        out_shape=[jax.ShapeDtypeStruct((t, half), F32), jax.ShapeDtypeStruct((t, half), BF16)],
        scratch_shapes=[pltpu.VMEM((tm, d), BF16)],
        compiler_params=_cparams(("parallel", "arbitrary")),
        name="norm_proj_gelu",
    )(h, g.reshape(1, d), w, w)


def _hgrn2_work(q_ref, f_ref, i_ref, g_ref, lbt_ref, gn_ref, o_ref, st_ref, *, layer, heads):
    c = CHUNK

    lbt = lbt_ref[...]
    e = jnp.exp(lbt - jnp.max(lbt, axis=0, keepdims=True))
    lb_all = jnp.sum(e[:layer + 1], axis=0, keepdims=True) / jnp.sum(e, axis=0, keepdims=True)

    row = lax.broadcasted_iota(jnp.int32, (c, c), 0)
    col = lax.broadcasted_iota(jnp.int32, (c, c), 1)
    tril = jnp.where(row >= col, 1.0, 0.0).astype(BF16)
    gn = gn_ref[...]

    levels = []
    ln = c // 2
    while ln >= SUB:
        levels += [(m * 2 * ln, m * 2 * ln + ln, ln) for m in range(c // (2 * ln))]
        ln //= 2
    n_pairs = sum(l[2] for l in levels)

    def seg_id(idx):
        sid = jnp.zeros_like(idx)
        start = 0
        for l in levels[:-1]:
            start += l[2]
            sid = sid + jnp.where(idx >= start, 1, 0)
        return sid

    same_seg = (seg_id(lax.broadcasted_iota(jnp.int32, (n_pairs, n_pairs), 0))
                == seg_id(lax.broadcasted_iota(jnp.int32, (n_pairs, n_pairs), 1)))
    sub_i = lax.broadcasted_iota(jnp.int32, (c // SUB, SUB, HEAD_DIM), 1)

    sls = [slice(h * HEAD_DIM, (h + 1) * HEAD_DIM) for h in range(heads)]

    def gates(h):
        lb = lb_all[:, sls[h]]
        q = q_ref[:, sls[h]] * (HEAD_DIM ** -0.5)
        forget = lb + (1.0 - lb) * _sigmoid(f_ref[:, sls[h]])
        lh, ll = _split(jnp.log(forget))
        b2 = _dot(tril, jnp.concatenate([lh, ll], axis=1))
        return q, 1.0 - forget, i_ref[:, sls[h]], b2[:, :HEAD_DIM] + b2[:, HEAD_DIM:]

    def block_pairs(h, q, k, v, b):
        o = _dot_nt((q * jnp.exp(b)).astype(BF16), st_ref[h].astype(BF16))
        qs, ks, vs = [], [], []
        for k0, q0, ln in levels:
            bref = b[q0 - 1:q0, :]
            qs.append(q[q0:q0 + ln] * jnp.exp(b[q0:q0 + ln] - bref))
            ks.append(k[k0:k0 + ln] * jnp.exp(bref - b[k0:k0 + ln]))
            vs.append(v[k0:k0 + ln])
        s = _dot_nt(jnp.concatenate(qs, axis=0).astype(BF16), jnp.concatenate(ks, axis=0).astype(BF16))
        r = _dot(jnp.where(same_seg, s, 0.0).astype(BF16), jnp.concatenate(vs, axis=0).astype(BF16))
        groups = [None] * (c // SUB)
        start = 0
        for k0, q0, ln in levels:
            for j in range(ln // SUB):
                piece = r[start + j * SUB:start + (j + 1) * SUB]
                gi = q0 // SUB + j
                groups[gi] = piece if groups[gi] is None else groups[gi] + piece
            start += ln
        groups[0] = jnp.zeros((SUB, HEAD_DIM), F32)
        return o + jnp.concatenate(groups, axis=0)

    def near_pairs(q, k, v, b):
        q3, k3, v3, b3 = (a.reshape(c // SUB, SUB, HEAD_DIM) for a in (q, k, v, b))
        o3 = jnp.sum(q3 * k3, axis=2, keepdims=True) * v3
        for d in range(1, SUB):
            dec = jnp.exp(jnp.where(sub_i >= d, b3 - pltpu.roll(b3, d, 1), -jnp.inf))
            w = jnp.sum(q3 * pltpu.roll(k3, d, 1) * dec, axis=2, keepdims=True)
            o3 = o3 + w * pltpu.roll(v3, d, 1)
        return o3.reshape(c, HEAD_DIM)

    qkvb, far = {}, {}

    def state_part(h):
        qkvb[h] = gates(h)
        q, k, v, b = qkvb[h]
        far[h] = block_pairs(h, q, k, v, b)
        blast = b[c - 1:c, :]
        kd = (k * jnp.exp(blast - b)).astype(BF16)
        st_ref[h] = st_ref[h] * jnp.exp(blast) + _dot_tn(v.astype(BF16), kd)

    def block_part(h):
        o = far[h] + near_pairs(*qkvb[h])
        on = o * lax.rsqrt(jnp.mean(o * o, axis=1, keepdims=True) + EPS) * gn
        o_ref[:, sls[h]] = (on * _silu(g_ref[:, sls[h]])).astype(BF16)

    return ([functools.partial(state_part, h) for h in range(heads)]
            + [functools.partial(block_part, h) for h in range(heads)])


def _causal_conv(ext_ref, x, w, first):
    n = x.shape[0]

    @pl.when(first)
    def _():
        ext_ref[0:CONV_TAIL, :] = jnp.zeros((CONV_TAIL, x.shape[1]), F32)

    ext_ref[CONV_TAIL:CONV_TAIL + n, :] = x
    y = x * w[CONV_WIDTH - 1:CONV_WIDTH, :]
    for k in range(CONV_WIDTH - 1):
        off = CONV_TAIL - (CONV_WIDTH - 1) + k
        y = y + ext_ref[off:off + n, :] * w[k:k + 1, :]
    ext_ref[0:CONV_TAIL, :] = ext_ref[n:n + CONV_TAIL, :]
    return y


def _mixer_ab_kernel(qa_ref, fa_ref, ia_ref, ga_ref, q_ref, k_ref, v_ref, z_ref, sm_ref, lbt_ref, gna_ref, cw_ref,
                     alog_ref, dtb_ref, gn_ref, o_ref, sta_ref, st_ref, eq_ref, ek_ref, ev_ref,
                     *, heads_a, heads, layer):
    c = CHUNK
    first = pl.program_id(1) == 0

    @pl.when(first)
    def _():
        sta_ref[...] = jnp.zeros_like(sta_ref)
        st_ref[...] = jnp.zeros_like(st_ref)

    col0 = heads_a * HEAD_DIM
    hw = heads * HEAD_DIM
    cw = cw_ref[...]
    qc_all = _silu(_causal_conv(eq_ref, q_ref[...], cw[:, 0:hw], first))
    kc_all = _silu(_causal_conv(ek_ref, k_ref[...], cw[:, hw:2 * hw], first))
    vc_all = _silu(_causal_conv(ev_ref, v_ref[...], cw[:, 2 * hw:3 * hw], first))

    row = lax.broadcasted_iota(jnp.int32, (c, c), 0)
    col = lax.broadcasted_iota(jnp.int32, (c, c), 1)
    causal = row >= col
    strict = row > col
    eye = row == col
    eye_f = jnp.where(eye, 1.0, 0.0)
    diag_blk = (row // SUB) == (col // SUB)
    merge_masks = []
    s = SUB
    while s < c:
        merge_masks.append(jnp.logical_and((row // (2 * s)) == (col // (2 * s)), (row // s) == (col // s) + 1))
        s *= 2
    gn = gn_ref[...]

    def to_row(colv):
        return jnp.sum(jnp.where(eye, colv, 0.0), axis=0, keepdims=True)

    def head_group(hg, qc, kc, vc, sm, z_ref, o_ref, fill):
        hs = range(len(hg))
        sls = [slice(h * HEAD_DIM, (h + 1) * HEAD_DIM) for h in hg]
        qn, kn, knb, beta, gam_col, egam, decay, kbeta = [], [], [], [], [], [], [], []
        for i, h in enumerate(hg):
            qh, kh = qc[:, sls[i]], kc[:, sls[i]]
            qn.append(qh * lax.rsqrt(jnp.sum(qh * qh, axis=1, keepdims=True) + EPS) * (HEAD_DIM ** -0.5))
            kn.append(kh * lax.rsqrt(jnp.sum(kh * kh, axis=1, keepdims=True) + EPS))
            knb.append(kn[i].astype(BF16))
            beta.append(_sigmoid(sm[:, heads + h:heads + h + 1]))
            g_col = -jnp.exp(alog_ref[:, h:h + 1]) * jax.nn.softplus(sm[:, h:h + 1] + dtb_ref[:, h:h + 1])
            g_row = to_row(g_col)
            gam_col.append(jnp.sum(jnp.where(causal, g_row, 0.0), axis=1, keepdims=True))
            gam_row = jnp.sum(jnp.where(strict, 0.0, g_col), axis=0, keepdims=True)
            decay.append(jnp.exp(jnp.where(causal, gam_col[i] - gam_row, -jnp.inf)))
            egam.append(jnp.exp(gam_col[i]))
            kbeta.append(kn[i] * beta[i])

        fill(FILL_AFTER_PREP)
        a_mat = [jnp.where(strict, _dot_nt(kbeta[i].astype(BF16), knb[i]) * decay[i], 0.0) for i in hs]
        qk = [(_dot_nt(qn[i].astype(BF16), knb[i]) * decay[i]).astype(BF16) for i in hs]
        fill(FILL_AFTER_SCORES)
        d_mat = [jnp.where(diag_blk, a_mat[i], 0.0) for i in hs]
        x = [eye_f - d_mat[i] for i in hs]
        p = [_dot2(d_mat[i], d_mat[i]) for i in hs]
        fill(FILL_PER_LEVEL)
        n_sq = (SUB - 1).bit_length() - 1
        for lvl in range(n_sq):
            if lvl < n_sq - 1:
                y = [_dot2(jnp.concatenate([x[i], p[i]], axis=0), p[i]) for i in hs]
                fill(FILL_PER_LEVEL)
                x = [x[i] + y[i][:c] for i in hs]
                p = [y[i][c:] for i in hs]
            else:
                y = [_dot2(x[i], p[i]) for i in hs]
                fill(FILL_PER_LEVEL)
                x = [x[i] + y[i] for i in hs]
        for below in merge_masks:
            y = [_dot2(x[i], jnp.where(below, a_mat[i], 0.0)) for i in hs]
            fill(FILL_PER_LEVEL)
            y = [_dot2(y[i], x[i]) for i in hs]
            fill(FILL_PER_LEVEL)
            x = [x[i] - y[i] for i in hs]
        rhs = [jnp.concatenate([vc[:, sls[i]] * beta[i], kbeta[i] * egam[i]], axis=1).astype(BF16) for i in hs]
        uw = [_dot(x[i].astype(BF16), rhs[i]) for i in hs]

        st = [st_ref[h] for h in hg]
        stb = [s.astype(BF16) for s in st]
        v_new = [uw[i][:, :HEAD_DIM] - _dot_nt(uw[i][:, HEAD_DIM:].astype(BF16), stb[i]) for i in hs]
        vnb = [v.astype(BF16) for v in v_new]
        o = [_dot_nt((qn[i] * egam[i]).astype(BF16), stb[i]) + _dot(qk[i], vnb[i]) for i in hs]
        for i, h in enumerate(hg):
            glast = gam_col[i][c - 1:c, :]
            kd = (kn[i] * jnp.exp(glast - gam_col[i])).astype(BF16)
            st_ref[h] = st[i] * jnp.exp(glast) + _dot_tn(vnb[i], kd)
        for i in hs:
            on = o[i] * lax.rsqrt(jnp.mean(o[i] * o[i], axis=1, keepdims=True) + EPS) * gn
            o_ref[:, col0 + hg[i] * HEAD_DIM:col0 + (hg[i] + 1) * HEAD_DIM] = (on * _silu(z_ref[:, sls[i]])).astype(BF16)

    for ci in range(q_ref.shape[0] // c):
        rs = pl.ds(ci * c, c)
        pending = _hgrn2_work(qa_ref.at[rs], fa_ref.at[rs], ia_ref.at[rs], ga_ref.at[rs], lbt_ref, gna_ref,
                              o_ref.at[rs], sta_ref, layer=layer, heads=heads_a)

        def fill(n, pending=pending):
            for _ in range(min(n, len(pending))):
                pending.pop(0)()

        r0 = ci * c
        for h0 in range(0, heads, GDN_HEAD_GROUP):
            head_group(list(range(h0, min(h0 + GDN_HEAD_GROUP, heads))), qc_all[r0:r0 + c], kc_all[r0:r0 + c],
                       vc_all[r0:r0 + c], sm_ref[rs, :], z_ref.at[rs], o_ref.at[rs], fill)
        fill(len(pending))


def mixer_ab(proj, small, lb_table, gnorm_a, conv_w, a_log, dt_bias, gnorm_b, *, bsz, seq, heads_a, heads_b, layer):
    t = proj.shape[0]
    wa, wb = heads_a * HEAD_DIM, heads_b * HEAD_DIM
    rows = min(MIXER_ROWS, seq)
    nc = seq // rows
    row = lambda b, s: b * nc + s

    def spec_a(k):
        return pl.BlockSpec((rows, wa), lambda b, s: (row(b, s), k))

    def spec_b(k):
        return pl.BlockSpec((rows, wb), lambda b, s: (row(b, s), 4 * wa // wb + k))

    def const(shape):
        return pl.BlockSpec(shape, lambda b, s: (0, 0))

    return pl.pallas_call(
        functools.partial(_mixer_ab_kernel, heads_a=heads_a, heads=heads_b, layer=layer),
        grid=(bsz, nc),
        in_specs=[spec_a(0), spec_a(1), spec_a(2), spec_a(3), spec_b(0), spec_b(1), spec_b(2), spec_b(3),
                  pl.BlockSpec((rows, small.shape[1]), lambda b, s: (row(b, s), 0)),
                  const(lb_table.shape), const((1, HEAD_DIM)),
                  const(conv_w.shape), const((1, heads_b)), const((1, heads_b)), const((1, HEAD_DIM))],
        out_specs=pl.BlockSpec((rows, wa + wb), lambda b, s: (row(b, s), 0)),
        out_shape=jax.ShapeDtypeStruct((t, wa + wb), BF16),
        scratch_shapes=[pltpu.VMEM((heads_a, HEAD_DIM, HEAD_DIM), F32), pltpu.VMEM((heads_b, HEAD_DIM, HEAD_DIM), F32)]
        + [pltpu.VMEM((rows + CONV_TAIL, wb), F32)] * 3,
        compiler_params=_cparams(("parallel", "arbitrary")),
        name="mixer_ab",
    )(proj, proj, proj, proj, proj, proj, proj, proj, small, lb_table, gnorm_a.reshape(1, HEAD_DIM), conv_w,
      a_log.reshape(1, heads_b), dt_bias.reshape(1, heads_b), gnorm_b.reshape(1, HEAD_DIM))


def _out_proj_kernel(a_ref, w_ref, h_ref, o_ref):
    o_ref[...] = h_ref[...] + _dot(a_ref[...], w_ref[...])


def out_proj(a, w, h, tm=1024, tn=1024):
    t, k = a.shape
    tm = min(tm, t)
    n = w.shape[1]
    return pl.pallas_call(
        _out_proj_kernel,
        grid=(t // tm, n // tn),
        in_specs=[pl.BlockSpec((tm, k), lambda i, j: (i, 0)),
                  pl.BlockSpec((k, tn), lambda i, j: (0, j)),
                  pl.BlockSpec((tm, tn), lambda i, j: (i, j))],
        out_specs=pl.BlockSpec((tm, tn), lambda i, j: (i, j)),
        out_shape=jax.ShapeDtypeStruct((t, n), F32),
        compiler_params=_cparams(("parallel", "arbitrary")),
        name="out_proj",
    )(a, w, h)


def _swiglu_kernel(h_ref, g_ref, wg_ref, wu_ref, wd_ref, o_ref, un_ref):
    f = pl.program_id(1)

    @pl.when(f == 0)
    def _():
        un_ref[...] = _rms(h_ref[...], g_ref[...]).astype(BF16)
        o_ref[...] = h_ref[...]

    un = un_ref[...]
    hb = (_silu(_dot(un, wg_ref[...])) * _dot(un, wu_ref[...])).astype(BF16)
    o_ref[...] += _dot(hb, wd_ref[...])


def swiglu(h, g, wg, wu, wd, tm=1024, tf=512):
    t, d = h.shape
    tm = min(tm, t)
    ff = wg.shape[1]
    return pl.pallas_call(
        _swiglu_kernel,
        grid=(t // tm, ff // tf),
        in_specs=[pl.BlockSpec((tm, d), lambda i, f: (i, 0), pipeline_mode=pl.Buffered(1)),
                  pl.BlockSpec((1, d), lambda i, f: (0, 0)),
                  pl.BlockSpec((d, tf), lambda i, f: (0, f)),
                  pl.BlockSpec((d, tf), lambda i, f: (0, f)),
                  pl.BlockSpec((tf, d), lambda i, f: (f, 0))],
        out_specs=pl.BlockSpec((tm, d), lambda i, f: (i, 0)),
        out_shape=jax.ShapeDtypeStruct((t, d), F32),
        scratch_shapes=[pltpu.VMEM((tm, d), BF16)],
        compiler_params=_cparams(("parallel", "arbitrary")),
        name="swiglu",
    )(h, g.reshape(1, d), wg, wu, wd)


def _ple_kernel(*refs, has_add, has_final):
    h_ref, p_ref, g_ref, wg_ref, wp_ref = refs[:5]
    k = 5
    add_ref = gf_ref = None
    if has_add:
        add_ref = refs[k]
        k += 1
    if has_final:
        gf_ref = refs[k]
        k += 1
    o_ref = refs[k]
    h = h_ref[...]
    if has_add:
        h = h + add_ref[...]
    un = _rms(h, g_ref[...]).astype(BF16)
    gate = _sigmoid(_dot(un, wg_ref[...]))
    out = h + gate * _dot(p_ref[...].astype(BF16), wp_ref[...])
    if has_final:
        out = _rms(out, gf_ref[...])
    o_ref[...] = out


def ple(h, p, g, wg, wp, add=None, g_final=None, tm=512):
    t, d = h.shape
    tm = min(tm, t)
    pd = p.shape[1]
    row = lambda i: (i, 0)
    const = lambda i: (0, 0)
    in_specs = [pl.BlockSpec((tm, d), row), pl.BlockSpec((tm, pd), row), pl.BlockSpec((1, d), const),
                pl.BlockSpec((d, d), const), pl.BlockSpec((pd, d), const)]
    args = [h, p, g.reshape(1, d), wg, wp]
    if add is not None:
        in_specs.append(pl.BlockSpec((tm, d), row))
        args.append(add)
    if g_final is not None:
        in_specs.append(pl.BlockSpec((1, d), const))
        args.append(g_final.reshape(1, d))
    return pl.pallas_call(
        functools.partial(_ple_kernel, has_add=add is not None, has_final=g_final is not None),
        grid=(t // tm,),
        in_specs=in_specs,
        out_specs=pl.BlockSpec((tm, d), row),
        out_shape=jax.ShapeDtypeStruct((t, d), F32),
        compiler_params=_cparams(("parallel",)),
        name="ple",
    )(*args)


def _rglru_kernel(x_ref, y_ref, cw_ref, cb_ref, wr_ref, br_ref, wi_ref, bi_ref, lam_ref, o_ref,
                  ext_ref, hc_ref, *, blocks):
    n = x_ref.shape[0]
    first = pl.program_id(1) == 0

    @pl.when(first)
    def _():
        hc_ref[...] = jnp.zeros_like(hc_ref)

    xc = _causal_conv(ext_ref, x_ref[...], cw_ref[...], first) + cb_ref[...]
    bw = xc.shape[1] // blocks
    rowi = lax.broadcasted_iota(jnp.int32, (n, bw), 0)
    at_start = jnp.logical_and(first, rowi == 0)
    gidx = lax.broadcasted_iota(jnp.int32, (n // SCAN_GROUP, SCAN_GROUP, bw), 1)

    for nb in range(blocks):
        sl = slice(nb * bw, (nb + 1) * bw)
        xb = xc[:, sl]
        xbb = xb.astype(BF16)
        r = _sigmoid(_dot(xbb, wr_ref[nb]) + br_ref[:, sl])
        gi = _sigmoid(_dot(xbb, wi_ref[nb]) + bi_ref[:, sl])
        log_a = -RGLRU_C * r * jax.nn.softplus(-lam_ref[:, sl])
        a = jnp.exp(log_a)
        m2 = 1.0 - a * a
        mult = jnp.where(m2 > 0.0, m2 * lax.rsqrt(m2), 0.0)
        mult = jnp.where(at_start, 1.0, mult)
        b = mult * gi * xb
        a = a.reshape(n // SCAN_GROUP, SCAN_GROUP, bw)
        b = b.reshape(n // SCAN_GROUP, SCAN_GROUP, bw)
        sh = 1
        while sh < SCAN_GROUP:
            ok = gidx >= sh
            a_prev = jnp.where(ok, pltpu.roll(a, sh, 1), 1.0)
            b_prev = jnp.where(ok, pltpu.roll(b, sh, 1), 0.0)
            b = b + a * b_prev
            a = a * a_prev
            sh *= 2
        carry = hc_ref[:, sl]
        groups = []
        for gi_ in range(n // SCAN_GROUP):
            hg = b[gi_] + a[gi_] * carry
            groups.append(hg)
            carry = hg[SCAN_GROUP - 1:SCAN_GROUP, :]
        hseq = jnp.concatenate(groups, axis=0)
        hc_ref[:, sl] = carry
        o_ref[:, sl] = (hseq * y_ref[:, sl].astype(F32)).astype(BF16)


def rglru(xr, y, conv_w, conv_b, w_r, b_r, w_i, b_i, lam, *, bsz, seq, rows=256):
    t, cwid = xr.shape
    blocks = w_r.shape[0]
    ns = seq // rows
    row = lambda b, s: (b * ns + s, 0)
    c2 = lambda b, s: (0, 0)
    c3 = lambda b, s: (0, 0, 0)
    vec = lambda a: a.reshape(1, cwid)
    return pl.pallas_call(
        functools.partial(_rglru_kernel, blocks=blocks),
        grid=(bsz, ns),
        in_specs=[pl.BlockSpec((rows, cwid), row), pl.BlockSpec((rows, cwid), row),
                  pl.BlockSpec(conv_w.shape, c2), pl.BlockSpec((1, cwid), c2),
                  pl.BlockSpec(w_r.shape, c3), pl.BlockSpec((1, cwid), c2),
                  pl.BlockSpec(w_i.shape, c3), pl.BlockSpec((1, cwid), c2),
                  pl.BlockSpec((1, cwid), c2)],
        out_specs=pl.BlockSpec((rows, cwid), row),
        out_shape=jax.ShapeDtypeStruct((t, cwid), BF16),
        scratch_shapes=[pltpu.VMEM((rows + CONV_TAIL, cwid), F32), pltpu.VMEM((1, cwid), F32)],
        compiler_params=_cparams(("parallel", "arbitrary")),
        name="rglru",
    )(xr, y, conv_w, vec(conv_b), w_r, vec(b_r), w_i, vec(b_i), vec(lam))


def _router_kernel(h_ref, g_ref, wr_ref, un_ref, pos_ref, gate_ref, cnt_ref):
    tm = h_ref.shape[0]
    ne = wr_ref.shape[0]
    un = _rms(h_ref[...], g_ref[...])
    uh, ul = _split(un)
    un_ref[...] = uh
    wh, wl = _split(wr_ref[...])
    logits = _dot_nt(wh, uh) + _dot_nt(wh, ul) + _dot_nt(wl, uh)
    eidx = lax.broadcasted_iota(jnp.int32, (ne, tm), 0).astype(F32)
    m1 = jnp.max(logits, axis=0, keepdims=True)
    i1 = jnp.min(jnp.where(logits == m1, eidx, float(ne)), axis=0, keepdims=True)
    mask1 = eidx == i1
    rest = jnp.where(mask1, -jnp.inf, logits)
    m2 = jnp.max(rest, axis=0, keepdims=True)
    i2 = jnp.min(jnp.where(rest == m2, eidx, float(ne)), axis=0, keepdims=True)
    mask2 = eidx == i2
    e2 = jnp.exp(m2 - m1)
    g1 = 1.0 / (1.0 + e2)
    g2 = e2 / (1.0 + e2)
    gate_ref[...] = jnp.where(mask1, g1, jnp.where(mask2, g2, 0.0))
    sel = jnp.logical_or(mask1, mask2)
    self32 = jnp.where(sel, 1.0, 0.0)
    before = lax.broadcasted_iota(jnp.int32, (tm, tm), 0) < lax.broadcasted_iota(jnp.int32, (tm, tm), 1)
    rank = _dot(self32.astype(BF16), jnp.where(before, 1.0, 0.0).astype(BF16))
    pos_ref[...] = jnp.where(sel, rank, -1.0)
    cnt = jnp.sum(self32, axis=1, keepdims=True).astype(jnp.int32)
    cnt_ref[0] = jnp.broadcast_to(cnt, cnt_ref.shape[1:])


def router(h, g, wr_t, tm):
    t, d = h.shape
    tm = min(tm, t)
    ne = wr_t.shape[0]
    nt = t // tm
    return pl.pallas_call(
        _router_kernel,
        grid=(nt,),
        in_specs=[pl.BlockSpec((tm, d), lambda i: (i, 0)),
                  pl.BlockSpec((1, d), lambda i: (0, 0)),
                  pl.BlockSpec((ne, d), lambda i: (0, 0))],
        out_specs=[pl.BlockSpec((tm, d), lambda i: (i, 0)),
                   pl.BlockSpec((ne, tm), lambda i: (0, i)),
                   pl.BlockSpec((ne, tm), lambda i: (0, i)),
                   pl.BlockSpec((1, ne, 128), lambda i: (i, 0, 0))],
        out_shape=[jax.ShapeDtypeStruct((t, d), BF16), jax.ShapeDtypeStruct((ne, t), F32),
                   jax.ShapeDtypeStruct((ne, t), F32), jax.ShapeDtypeStruct((nt, ne, 128), jnp.int32)],
        compiler_params=_cparams(("parallel",)),
        name="moe_router",
    )(h, g.reshape(1, d), wr_t)


def _pick(pos, base, rows):
    slot = lax.broadcasted_iota(jnp.int32, (rows, pos.shape[1]), 0).astype(F32)
    return pos == slot + base.astype(F32)


def _moe_gather_kernel(ce_ref, cb_ref, co_ref, nq_ref, un_ref, pos_ref, xs_in_ref, xs_ref, buf_ref, sem, *, rows, qmax):
    del xs_in_ref
    i = pl.program_id(0)
    n = nq_ref[i]

    def copy(slot, off):
        return pltpu.make_async_copy(buf_ref.at[slot], xs_ref.at[pl.ds(off, rows)], sem.at[slot])

    def body(q, carry):
        k = i * qmax + q
        slot = lax.rem(q, 2)

        @pl.when(q >= 2)
        def _():
            copy(slot, 0).wait()

        pos = pos_ref[pl.ds(ce_ref[k], 1), :]
        sel = jnp.where(_pick(pos, cb_ref[k], rows), 1.0, 0.0).astype(BF16)
        buf_ref[slot] = _dot(sel, un_ref[...]).astype(BF16)
        copy(slot, pl.multiple_of(co_ref[k], MOE_GRANULE)).start()
        return carry

    lax.fori_loop(0, n, body, 0)

    @pl.when(n >= 2)
    def _():
        copy(lax.rem(n, 2), 0).wait()

    @pl.when(n >= 1)
    def _():
        copy(lax.rem(n + 1, 2), 0).wait()


def moe_gather(un, pos, tables, n_rows, tm, rows):
    t, d = un.shape
    ne = pos.shape[0]
    ce, cb, co, nq, qmax = tables
    grid_spec = pltpu.PrefetchScalarGridSpec(
        num_scalar_prefetch=4,
        grid=(t // tm,),
        in_specs=[pl.BlockSpec((tm, d), lambda i, *_: (i, 0)),
                  pl.BlockSpec((ne, tm), lambda i, *_: (0, i)),
                  pl.BlockSpec(memory_space=pl.ANY)],
        out_specs=pl.BlockSpec(memory_space=pl.ANY),
        scratch_shapes=[pltpu.VMEM((2, rows, d), BF16), pltpu.SemaphoreType.DMA((2,))],
    )
    return pl.pallas_call(
        functools.partial(_moe_gather_kernel, rows=rows, qmax=qmax),
        grid_spec=grid_spec,
        out_shape=jax.ShapeDtypeStruct((n_rows, d), BF16),
        input_output_aliases={6: 0},
        compiler_params=_cparams(("arbitrary",)),
        name="moe_gather",
    )(ce, cb, co, nq, un, pos, jnp.zeros((n_rows, d), BF16))


def _moe_ffn_kernel(be_ref, nv_ref, x_ref, wg_ref, wu_ref, wd_ref, o_ref, acc_ref):
    del be_ref
    b, f = pl.program_id(0), pl.program_id(1)
    nf = pl.num_programs(1)
    bm = x_ref.shape[0]
    nv = nv_ref[b]

    @pl.when(f == 0)
    def _():
        acc_ref[...] = jnp.zeros_like(acc_ref)

    def ffn(rows):
        x = x_ref[rows, :]
        wg, wu, wd = (w_ref[0].astype(BF16) for w_ref in (wg_ref, wu_ref, wd_ref))
        hb = (_silu(_dot(x, wg)) * _dot(x, wu)).astype(BF16)
        acc_ref[rows, :] += _dot(hb, wd)

    @pl.when(nv > bm - MOE_PART_ROWS)
    def _():
        ffn(pl.ds(0, bm))

    for part in range(bm // MOE_PART_ROWS):
        @pl.when(jnp.logical_and(nv <= bm - MOE_PART_ROWS, nv > part * MOE_PART_ROWS))
        def _():
            ffn(pl.ds(part * MOE_PART_ROWS, MOE_PART_ROWS))

    @pl.when(f == nf - 1)
    def _():
        o_ref[...] = acc_ref[...].astype(BF16)


def moe_ffn(xs, blk_e, blk_rows, wg, wu, wd, bm, tf=MOE_FFN_TILE):
    n_rows, d = xs.shape
    ff = wg.shape[2]
    nf = ff // tf

    def fidx(b, f, nv):
        return jnp.where(nv[b] > 0, f, nf - 1)

    grid_spec = pltpu.PrefetchScalarGridSpec(
        num_scalar_prefetch=2,
        grid=(n_rows // bm, nf),
        in_specs=[pl.BlockSpec((bm, d), lambda b, f, be, nv: (b, 0), pipeline_mode=pl.Buffered(1)),
                  pl.BlockSpec((1, d, tf), lambda b, f, be, nv: (be[b], 0, fidx(b, f, nv))),
                  pl.BlockSpec((1, d, tf), lambda b, f, be, nv: (be[b], 0, fidx(b, f, nv))),
                  pl.BlockSpec((1, tf, d), lambda b, f, be, nv: (be[b], fidx(b, f, nv), 0))],
        out_specs=pl.BlockSpec((bm, d), lambda b, f, be, nv: (b, 0)),
        scratch_shapes=[pltpu.VMEM((bm, d), F32)],
    )
    return pl.pallas_call(
        _moe_ffn_kernel,
        grid_spec=grid_spec,
        out_shape=jax.ShapeDtypeStruct((n_rows, d), BF16),
        compiler_params=_cparams(("parallel", "arbitrary")),
        name="moe_ffn",
    )(blk_e, blk_rows, xs, wg, wu, wd)


def _moe_combine_kernel(ce_ref, cb_ref, co_ref, nq_ref, pos_ref, gate_ref, y_ref, o_ref, buf_ref, sem, *, rows, qmax):
    i = pl.program_id(0)
    n = nq_ref[i]
    o_ref[...] = jnp.zeros_like(o_ref)

    def copy(slot, off):
        return pltpu.make_async_copy(y_ref.at[pl.ds(off, rows)], buf_ref.at[slot], sem.at[slot])

    def start(q):
        copy(lax.rem(q, 2), pl.multiple_of(co_ref[i * qmax + q], MOE_GRANULE)).start()

    @pl.when(n > 0)
    def _():
        start(0)

    def body(q, carry):
        k = i * qmax + q
        slot = lax.rem(q, 2)

        @pl.when(q + 1 < n)
        def _():
            start(q + 1)

        copy(slot, 0).wait()
        e = ce_ref[k]
        hit = _pick(pos_ref[pl.ds(e, 1), :], cb_ref[k], rows)
        gsub = jnp.sum(jnp.where(hit, gate_ref[pl.ds(e, 1), :], 0.0), axis=1, keepdims=True)
        yb = (buf_ref[slot].astype(F32) * gsub).astype(BF16)
        o_ref[...] += _dot_tn(jnp.where(hit, 1.0, 0.0).astype(BF16), yb)
        return carry

    lax.fori_loop(0, n, body, 0)


def moe_combine(ys, pos, gate, tables, tm, rows):
    ne, t = pos.shape
    d = ys.shape[1]
    ce, cb, co, nq, qmax = tables
    grid_spec = pltpu.PrefetchScalarGridSpec(
        num_scalar_prefetch=4,
        grid=(t // tm,),
        in_specs=[pl.BlockSpec((ne, tm), lambda i, *_: (0, i)),
                  pl.BlockSpec((ne, tm), lambda i, *_: (0, i)),
                  pl.BlockSpec(memory_space=pl.ANY)],
        out_specs=pl.BlockSpec((tm, d), lambda i, *_: (i, 0)),
        scratch_shapes=[pltpu.VMEM((2, rows, d), BF16), pltpu.SemaphoreType.DMA((2,))],
    )
    return pl.pallas_call(
        functools.partial(_moe_combine_kernel, rows=rows, qmax=qmax),
        grid_spec=grid_spec,
        out_shape=jax.ShapeDtypeStruct((t, d), F32),
        compiler_params=_cparams(("arbitrary",)),
        name="moe_combine",
    )(ce, cb, co, nq, pos, gate, ys)


def _chunk_tables(counts, seg, rows, qmax):
    ne = counts.shape[1]
    ns = (counts + rows - 1) // rows
    cs = jnp.cumsum(ns, axis=1)
    q = jnp.arange(qmax, dtype=jnp.int32)
    ce = jnp.minimum(jnp.sum(q[None, :, None] >= cs[:, None, :], axis=-1), ne - 1).astype(jnp.int32)
    cj = q[None, :] - jnp.take_along_axis(cs - ns, ce, axis=1)
    co = jnp.take_along_axis(seg, ce, axis=1) + cj * rows
    flat = lambda a: a.reshape(-1).astype(jnp.int32)
    return flat(ce), flat(cj * rows), flat(co), cs[:, -1].astype(jnp.int32), qmax


def moe(h, g, w_router, wg, wu, wd, tm=MOE_TILE, bm=MOE_BLOCK):
    t = h.shape[0]
    tm = min(tm, t)
    ne = wg.shape[0]
    nt = t // tm
    un, pos, gate, cnt = router(h, g, w_router.T, tm)
    counts = cnt[:, :, 0]
    padded = (counts + MOE_GRANULE - 1) // MOE_GRANULE * MOE_GRANULE
    tot = jnp.sum(padded, axis=0)
    ptot = (tot + MOE_SLACK + bm - 1) // bm * bm
    eend = jnp.cumsum(ptot)
    seg = (eend - ptot)[None, :] + jnp.cumsum(padded, axis=0) - padded
    n_blocks = (TOP_K * t + nt * ne * (MOE_GRANULE - 1) + ne * MOE_SLACK) // bm + ne
    blk_row0 = jnp.arange(n_blocks, dtype=jnp.int32) * bm
    blk_e = jnp.minimum(jnp.searchsorted(eend, blk_row0, side="right"), ne - 1).astype(jnp.int32)
    blk_rows = jnp.clip((eend - ptot + tot)[blk_e] - blk_row0, 0, bm).astype(jnp.int32)
    g_tab = _chunk_tables(counts, seg, MOE_GATHER_ROWS, TOP_K * tm // MOE_GATHER_ROWS + ne)
    c_tab = _chunk_tables(counts, seg, MOE_COMBINE_ROWS, TOP_K * tm // MOE_COMBINE_ROWS + ne)
    xs = moe_gather(un, pos, g_tab, n_blocks * bm, tm, MOE_GATHER_ROWS)
    ys = moe_ffn(xs, blk_e, blk_rows, wg, wu, wd, bm)
    return moe_combine(ys, pos, gate, c_tab, tm, MOE_COMBINE_ROWS)


def kernel(x, p, ln_mix, ln_ffn, ln_ple, ln_final, lb_table, ab_w_in, ab_conv, b_a_log, b_dt_bias, a_gnorm, b_gnorm, ab_w_out, c_w_in, c_conv_w, c_conv_b, c_w_r, c_b_r, c_w_i, c_b_i, c_lambda, c_w_out, ffn_w_gate, ffn_w_up, ffn_w_down, moe_router, moe_w_gate, moe_w_up, moe_w_down, ple_w_proj, ple_w_gate):
    bsz, seq, d = x.shape
    t = bsz * seq
    depth = ln_mix.shape[0]
    a_heads = lb_table.shape[1] // HEAD_DIM
    b_heads = b_a_log.shape[1]
    a_w = a_heads * HEAD_DIM
    b_w = b_heads * HEAD_DIM
    main_w = 4 * a_w + 4 * b_w
    bf = lambda a: a.astype(BF16)

    h = x.reshape(t, d)
    for layer in range(depth):
        j = layer // 2
        if layer % 2 == 0:
            w_in = ab_w_in[j]
            w_small = jnp.pad(w_in[:, main_w:], ((0, 0), (0, 128 - 2 * b_heads)))
            proj, small = norm_proj(h, ln_mix[layer], bf(w_in[:, :main_w]), bf(w_small))
            mixed = mixer_ab(proj, small, lb_table, a_gnorm[j], ab_conv[j], b_a_log[j], b_dt_bias[j], b_gnorm[j],
                             bsz=bsz, seq=seq, heads_a=a_heads, heads_b=b_heads, layer=layer)
            h = out_proj(mixed, bf(ab_w_out[j]), h)
            h = swiglu(h, ln_ffn[layer], bf(ffn_w_gate[j]), bf(ffn_w_up[j]), bf(ffn_w_down[j]))
            add = None
        else:
            xr, yb = norm_proj_gelu(h, ln_mix[layer], bf(c_w_in[j]))
            hy = rglru(xr, yb, c_conv_w[j], c_conv_b[j], bf(c_w_r[j]), c_b_r[j], bf(c_w_i[j]), c_b_i[j],
                       c_lambda[j], bsz=bsz, seq=seq)
            h = out_proj(hy, bf(c_w_out[j]), h)
            add = moe(h, ln_ffn[layer], moe_router[j], moe_w_gate[j], moe_w_up[j], moe_w_down[j])
        g_final = ln_final if layer == depth - 1 else None
        h = ple(h, p[layer].reshape(t, -1), ln_ple[layer], bf(ple_w_gate[layer]), bf(ple_w_proj[layer]),
                add=add, g_final=g_final)
    if depth == 0:
        raise ValueError("depth must be positive")
    return h.reshape(bsz, seq, d)
```

```python
import functools

import jax
import jax.numpy as jnp
from jax import lax
from jax.experimental import pallas as pl
from jax.experimental.pallas import tpu as pltpu

F32 = jnp.float32
BF16 = jnp.bfloat16
EPS = 1e-6
CHUNK = 64
SUB = 8
HEAD_DIM = 128
CONV_WIDTH = 4
GDN_HEAD_GROUP = 8
MIXER_ROWS = 128
FILL_AFTER_PREP = 4
FILL_AFTER_SCORES = 4
FILL_PER_LEVEL = 1
CONV_TAIL = 8
RGLRU_C = 8.0
SCAN_GROUP = 8
TOP_K = 2
MOE_TILE = 1024
MOE_BLOCK = 1024
MOE_FFN_TILE = 256
MOE_PART_ROWS = 256
MOE_GRANULE = 16
MOE_GATHER_ROWS = 128
MOE_COMBINE_ROWS = 256
MOE_SLACK = 256
VMEM_LIMIT = 56 * 1024 * 1024


def _cparams(sem, vmem=VMEM_LIMIT):
    return pltpu.CompilerParams(dimension_semantics=sem, vmem_limit_bytes=vmem)


def _dot(a, b):
    return jnp.dot(a, b, preferred_element_type=F32)


def _dot_nt(a, b):
    return lax.dot_general(a, b, (((1,), (1,)), ((), ())), preferred_element_type=F32)


def _dot_tn(a, b):
    return lax.dot_general(a, b, (((0,), (0,)), ((), ())), preferred_element_type=F32)


def _split(a):
    hi = a.astype(BF16)
    lo = (a - hi.astype(F32)).astype(BF16)
    return hi, lo


def _dot2(a, b):
    ah, al = _split(a)
    bh = b.astype(BF16)
    return _dot(jnp.concatenate([ah, al], axis=1), jnp.concatenate([bh, bh], axis=0))


def _rms(x, g):
    return x * lax.rsqrt(jnp.mean(x * x, axis=-1, keepdims=True) + EPS) * g


def _sigmoid(x):
    return 1.0 / (1.0 + jnp.exp(-x))


def _silu(x):
    return x * _sigmoid(x)


def _norm_proj_kernel(h_ref, g_ref, w_ref, ws_ref, o_ref, os_ref, un_ref):
    j = pl.program_id(1)

    @pl.when(j == 0)
    def _():
        un = _rms(h_ref[...], g_ref[...]).astype(BF16)
        un_ref[...] = un
        os_ref[...] = _dot(un, ws_ref[...])

    o_ref[...] = _dot(un_ref[...], w_ref[...])


def norm_proj(h, g, w, ws, tm=1024, tn=1024):
    t, d = h.shape
    tm = min(tm, t)
    n = w.shape[1]
    return pl.pallas_call(
        _norm_proj_kernel,
        grid=(t // tm, n // tn),
        in_specs=[pl.BlockSpec((tm, d), lambda i, j: (i, 0)),
                  pl.BlockSpec((1, d), lambda i, j: (0, 0)),
                  pl.BlockSpec((d, tn), lambda i, j: (0, j)),
                  pl.BlockSpec((d, ws.shape[1]), lambda i, j: (0, 0))],
        out_specs=[pl.BlockSpec((tm, tn), lambda i, j: (i, j)),
                   pl.BlockSpec((tm, ws.shape[1]), lambda i, j: (i, 0))],
        out_shape=[jax.ShapeDtypeStruct((t, n), F32), jax.ShapeDtypeStruct((t, ws.shape[1]), F32)],
        scratch_shapes=[pltpu.VMEM((tm, d), BF16)],
        compiler_params=_cparams(("parallel", "arbitrary")),
        name="norm_proj",
    )(h, g.reshape(1, d), w, ws)


def _norm_proj_gelu_kernel(h_ref, g_ref, wy_ref, wx_ref, o_ref, y_ref, un_ref):
    @pl.when(pl.program_id(1) == 0)
    def _():
        un_ref[...] = _rms(h_ref[...], g_ref[...]).astype(BF16)

    un = un_ref[...]
    y_ref[...] = jax.nn.gelu(_dot(un, wy_ref[...])).astype(BF16)
    o_ref[...] = _dot(un, wx_ref[...])


def norm_proj_gelu(h, g, w, tm=1024, tn=512):
    t, d = h.shape
    tm = min(tm, t)
    half = w.shape[1] // 2
    nh = half // tn
    return pl.pallas_call(
        _norm_proj_gelu_kernel,
        grid=(t // tm, nh),
        in_specs=[pl.BlockSpec((tm, d), lambda i, j: (i, 0)),
                  pl.BlockSpec((1, d), lambda i, j: (0, 0)),
                  pl.BlockSpec((d, tn), lambda i, j: (0, j)),
                  pl.BlockSpec((d, tn), lambda i, j: (0, nh + j))],
        out_specs=[pl.BlockSpec((tm, tn), lambda i, j: (i, j)),
                   pl.BlockSpec((tm, tn), lambda i, j: (i, j))],
        out_shape=[jax.ShapeDtypeStruct((t, half), F32), jax.ShapeDtypeStruct((t, half), BF16)],
        scratch_shapes=[pltpu.VMEM((tm, d), BF16)],
        compiler_params=_cparams(("parallel", "arbitrary")),
        name="norm_proj_gelu",
    )(h, g.reshape(1, d), w, w)


def _proj_gelu_kernel(un_ref, wy_ref, wx_ref, o_ref, y_ref):
    un = un_ref[...]
    y_ref[...] = jax.nn.gelu(_dot(un, wy_ref[...])).astype(BF16)
    o_ref[...] = _dot(un, wx_ref[...])


def proj_gelu(un, w, tm=1024, tn=512):
    t, d = un.shape
    tm = min(tm, t)
    half = w.shape[1] // 2
    nh = half // tn
    return pl.pallas_call(
        _proj_gelu_kernel,
        grid=(t // tm, nh),
        in_specs=[pl.BlockSpec((tm, d), lambda i, j: (i, 0)),
                  pl.BlockSpec((d, tn), lambda i, j: (0, j)),
                  pl.BlockSpec((d, tn), lambda i, j: (0, nh + j))],
        out_specs=[pl.BlockSpec((tm, tn), lambda i, j: (i, j)),
                   pl.BlockSpec((tm, tn), lambda i, j: (i, j))],
        out_shape=[jax.ShapeDtypeStruct((t, half), F32), jax.ShapeDtypeStruct((t, half), BF16)],
        compiler_params=_cparams(("parallel", "arbitrary")),
        name="proj_gelu",
    )(un, w, w)


def _hgrn2_work(q_ref, f_ref, i_ref, g_ref, lbt_ref, gn_ref, o_ref, st_ref, *, layer, heads):
    c = CHUNK

    lbt = lbt_ref[...]
    e = jnp.exp(lbt - jnp.max(lbt, axis=0, keepdims=True))
    lb_all = jnp.sum(e[:layer + 1], axis=0, keepdims=True) / jnp.sum(e, axis=0, keepdims=True)

    row = lax.broadcasted_iota(jnp.int32, (c, c), 0)
    col = lax.broadcasted_iota(jnp.int32, (c, c), 1)
    tril = jnp.where(row >= col, 1.0, 0.0).astype(BF16)
    gn = gn_ref[...]

    levels = []
    ln = c // 2
    while ln >= SUB:
        levels += [(m * 2 * ln, m * 2 * ln + ln, ln) for m in range(c // (2 * ln))]
        ln //= 2
    n_pairs = sum(l[2] for l in levels)

    def seg_id(idx):
        sid = jnp.zeros_like(idx)
        start = 0
        for l in levels[:-1]:
            start += l[2]
            sid = sid + jnp.where(idx >= start, 1, 0)
        return sid

    same_seg = (seg_id(lax.broadcasted_iota(jnp.int32, (n_pairs, n_pairs), 0))
                == seg_id(lax.broadcasted_iota(jnp.int32, (n_pairs, n_pairs), 1)))
    sub_i = lax.broadcasted_iota(jnp.int32, (c // SUB, SUB, HEAD_DIM), 1)

    sls = [slice(h * HEAD_DIM, (h + 1) * HEAD_DIM) for h in range(heads)]

    def gates(h):
        lb = lb_all[:, sls[h]]
        q = q_ref[:, sls[h]] * (HEAD_DIM ** -0.5)
        forget = lb + (1.0 - lb) * _sigmoid(f_ref[:, sls[h]])
        lh, ll = _split(jnp.log(forget))
        b2 = _dot(tril, jnp.concatenate([lh, ll], axis=1))
        return q, 1.0 - forget, i_ref[:, sls[h]], b2[:, :HEAD_DIM] + b2[:, HEAD_DIM:]

    def block_pairs(h, q, k, v, b):
        o = _dot_nt((q * jnp.exp(b)).astype(BF16), st_ref[h].astype(BF16))
        qs, ks, vs = [], [], []
        for k0, q0, ln in levels:
            bref = b[q0 - 1:q0, :]
            qs.append(q[q0:q0 + ln] * jnp.exp(b[q0:q0 + ln] - bref))
            ks.append(k[k0:k0 + ln] * jnp.exp(bref - b[k0:k0 + ln]))
            vs.append(v[k0:k0 + ln])
        s = _dot_nt(jnp.concatenate(qs, axis=0).astype(BF16), jnp.concatenate(ks, axis=0).astype(BF16))
        r = _dot(jnp.where(same_seg, s, 0.0).astype(BF16), jnp.concatenate(vs, axis=0).astype(BF16))
        groups = [None] * (c // SUB)
        start = 0
        for k0, q0, ln in levels:
            for j in range(ln // SUB):
                piece = r[start + j * SUB:start + (j + 1) * SUB]
                gi = q0 // SUB + j
                groups[gi] = piece if groups[gi] is None else groups[gi] + piece
            start += ln
        groups[0] = jnp.zeros((SUB, HEAD_DIM), F32)
        return o + jnp.concatenate(groups, axis=0)

    def near_pairs(q, k, v, b):
        q3, k3, v3, b3 = (a.reshape(c // SUB, SUB, HEAD_DIM) for a in (q, k, v, b))
        o3 = jnp.sum(q3 * k3, axis=2, keepdims=True) * v3
        for d in range(1, SUB):
            dec = jnp.exp(jnp.where(sub_i >= d, b3 - pltpu.roll(b3, d, 1), -jnp.inf))
            w = jnp.sum(q3 * pltpu.roll(k3, d, 1) * dec, axis=2, keepdims=True)
            o3 = o3 + w * pltpu.roll(v3, d, 1)
        return o3.reshape(c, HEAD_DIM)

    qkvb, far = {}, {}

    def state_part(h):
        qkvb[h] = gates(h)
        q, k, v, b = qkvb[h]
        far[h] = block_pairs(h, q, k, v, b)
        blast = b[c - 1:c, :]
        kd = (k * jnp.exp(blast - b)).astype(BF16)
        st_ref[h] = st_ref[h] * jnp.exp(blast) + _dot_tn(v.astype(BF16), kd)

    def block_part(h):
        o = far[h] + near_pairs(*qkvb[h])
        on = o * lax.rsqrt(jnp.mean(o * o, axis=1, keepdims=True) + EPS) * gn
        o_ref[:, sls[h]] = (on * _silu(g_ref[:, sls[h]])).astype(BF16)

    return ([functools.partial(state_part, h) for h in range(heads)]
            + [functools.partial(block_part, h) for h in range(heads)])


def _causal_conv(ext_ref, x, w, first):
    n = x.shape[0]

    @pl.when(first)
    def _():
        ext_ref[0:CONV_TAIL, :] = jnp.zeros((CONV_TAIL, x.shape[1]), F32)

    ext_ref[CONV_TAIL:CONV_TAIL + n, :] = x
    y = x * w[CONV_WIDTH - 1:CONV_WIDTH, :]
    for k in range(CONV_WIDTH - 1):
        off = CONV_TAIL - (CONV_WIDTH - 1) + k
        y = y + ext_ref[off:off + n, :] * w[k:k + 1, :]
    ext_ref[0:CONV_TAIL, :] = ext_ref[n:n + CONV_TAIL, :]
    return y


def _mixer_ab_kernel(qa_ref, fa_ref, ia_ref, ga_ref, q_ref, k_ref, v_ref, z_ref, sm_ref, lbt_ref, gna_ref, cw_ref,
                     alog_ref, dtb_ref, gn_ref, o_ref, sta_ref, st_ref, eq_ref, ek_ref, ev_ref,
                     *, heads_a, heads, layer):
    c = CHUNK
    first = pl.program_id(1) == 0

    @pl.when(first)
    def _():
        sta_ref[...] = jnp.zeros_like(sta_ref)
        st_ref[...] = jnp.zeros_like(st_ref)

    col0 = heads_a * HEAD_DIM
    hw = heads * HEAD_DIM
    cw = cw_ref[...]
    qc_all = _silu(_causal_conv(eq_ref, q_ref[...], cw[:, 0:hw], first))
    kc_all = _silu(_causal_conv(ek_ref, k_ref[...], cw[:, hw:2 * hw], first))
    vc_all = _silu(_causal_conv(ev_ref, v_ref[...], cw[:, 2 * hw:3 * hw], first))

    row = lax.broadcasted_iota(jnp.int32, (c, c), 0)
    col = lax.broadcasted_iota(jnp.int32, (c, c), 1)
    causal = row >= col
    strict = row > col
    eye = row == col
    eye_f = jnp.where(eye, 1.0, 0.0)
    diag_blk = (row // SUB) == (col // SUB)
    merge_masks = []
    s = SUB
    while s < c:
        merge_masks.append(jnp.logical_and((row // (2 * s)) == (col // (2 * s)), (row // s) == (col // s) + 1))
        s *= 2
    gn = gn_ref[...]

    def to_row(colv):
        return jnp.sum(jnp.where(eye, colv, 0.0), axis=0, keepdims=True)

    def head_group(hg, qc, kc, vc, sm, z_ref, o_ref, fill):
        hs = range(len(hg))
        sls = [slice(h * HEAD_DIM, (h + 1) * HEAD_DIM) for h in hg]
        qn, kn, knb, beta, gam_col, egam, decay, kbeta = [], [], [], [], [], [], [], []
        for i, h in enumerate(hg):
            qh, kh = qc[:, sls[i]], kc[:, sls[i]]
            qn.append(qh * lax.rsqrt(jnp.sum(qh * qh, axis=1, keepdims=True) + EPS) * (HEAD_DIM ** -0.5))
            kn.append(kh * lax.rsqrt(jnp.sum(kh * kh, axis=1, keepdims=True) + EPS))
            knb.append(kn[i].astype(BF16))
            beta.append(_sigmoid(sm[:, heads + h:heads + h + 1]))
            g_col = -jnp.exp(alog_ref[:, h:h + 1]) * jax.nn.softplus(sm[:, h:h + 1] + dtb_ref[:, h:h + 1])
            g_row = to_row(g_col)
            gam_col.append(jnp.sum(jnp.where(causal, g_row, 0.0), axis=1, keepdims=True))
            gam_row = jnp.sum(jnp.where(strict, 0.0, g_col), axis=0, keepdims=True)
            decay.append(jnp.exp(jnp.where(causal, gam_col[i] - gam_row, -jnp.inf)))
            egam.append(jnp.exp(gam_col[i]))
            kbeta.append(kn[i] * beta[i])

        fill(FILL_AFTER_PREP)
        a_mat = [jnp.where(strict, _dot_nt(kbeta[i].astype(BF16), knb[i]) * decay[i], 0.0) for i in hs]
        qk = [(_dot_nt(qn[i].astype(BF16), knb[i]) * decay[i]).astype(BF16) for i in hs]
        fill(FILL_AFTER_SCORES)
        d_mat = [jnp.where(diag_blk, a_mat[i], 0.0) for i in hs]
        x = [eye_f - d_mat[i] for i in hs]
        p = [_dot2(d_mat[i], d_mat[i]) for i in hs]
        fill(FILL_PER_LEVEL)
        n_sq = (SUB - 1).bit_length() - 1
        for lvl in range(n_sq):
            if lvl < n_sq - 1:
                y = [_dot2(jnp.concatenate([x[i], p[i]], axis=0), p[i]) for i in hs]
                fill(FILL_PER_LEVEL)
                x = [x[i] + y[i][:c] for i in hs]
                p = [y[i][c:] for i in hs]
            else:
                y = [_dot2(x[i], p[i]) for i in hs]
                fill(FILL_PER_LEVEL)
                x = [x[i] + y[i] for i in hs]
        for below in merge_masks:
            y = [_dot2(x[i], jnp.where(below, a_mat[i], 0.0)) for i in hs]
            fill(FILL_PER_LEVEL)
            y = [_dot2(y[i], x[i]) for i in hs]
            fill(FILL_PER_LEVEL)
            x = [x[i] - y[i] for i in hs]
        rhs = [jnp.concatenate([vc[:, sls[i]] * beta[i], kbeta[i] * egam[i]], axis=1).astype(BF16) for i in hs]
        uw = [_dot(x[i].astype(BF16), rhs[i]) for i in hs]

        st = [st_ref[h] for h in hg]
        stb = [s.astype(BF16) for s in st]
        v_new = [uw[i][:, :HEAD_DIM] - _dot_nt(uw[i][:, HEAD_DIM:].astype(BF16), stb[i]) for i in hs]
        vnb = [v.astype(BF16) for v in v_new]
        o = [_dot_nt((qn[i] * egam[i]).astype(BF16), stb[i]) + _dot(qk[i], vnb[i]) for i in hs]
        for i, h in enumerate(hg):
            glast = gam_col[i][c - 1:c, :]
            kd = (kn[i] * jnp.exp(glast - gam_col[i])).astype(BF16)
            st_ref[h] = st[i] * jnp.exp(glast) + _dot_tn(vnb[i], kd)
        for i in hs:
            on = o[i] * lax.rsqrt(jnp.mean(o[i] * o[i], axis=1, keepdims=True) + EPS) * gn
            o_ref[:, col0 + hg[i] * HEAD_DIM:col0 + (hg[i] + 1) * HEAD_DIM] = (on * _silu(z_ref[:, sls[i]])).astype(BF16)

    for ci in range(q_ref.shape[0] // c):
        rs = pl.ds(ci * c, c)
        pending = _hgrn2_work(qa_ref.at[rs], fa_ref.at[rs], ia_ref.at[rs], ga_ref.at[rs], lbt_ref, gna_ref,
                              o_ref.at[rs], sta_ref, layer=layer, heads=heads_a)

        def fill(n, pending=pending):
            for _ in range(min(n, len(pending))):
                pending.pop(0)()

        r0 = ci * c
        for h0 in range(0, heads, GDN_HEAD_GROUP):
            head_group(list(range(h0, min(h0 + GDN_HEAD_GROUP, heads))), qc_all[r0:r0 + c], kc_all[r0:r0 + c],
                       vc_all[r0:r0 + c], sm_ref[rs, :], z_ref.at[rs], o_ref.at[rs], fill)
        fill(len(pending))


def mixer_ab(proj, small, lb_table, gnorm_a, conv_w, a_log, dt_bias, gnorm_b, *, bsz, seq, heads_a, heads_b, layer):
    t = proj.shape[0]
    wa, wb = heads_a * HEAD_DIM, heads_b * HEAD_DIM
    rows = min(MIXER_ROWS, seq)
    nc = seq // rows
    row = lambda b, s: b * nc + s

    def spec_a(k):
        return pl.BlockSpec((rows, wa), lambda b, s: (row(b, s), k))

    def spec_b(k):
        return pl.BlockSpec((rows, wb), lambda b, s: (row(b, s), 4 * wa // wb + k))

    def const(shape):
        return pl.BlockSpec(shape, lambda b, s: (0, 0))

    return pl.pallas_call(
        functools.partial(_mixer_ab_kernel, heads_a=heads_a, heads=heads_b, layer=layer),
        grid=(bsz, nc),
        in_specs=[spec_a(0), spec_a(1), spec_a(2), spec_a(3), spec_b(0), spec_b(1), spec_b(2), spec_b(3),
                  pl.BlockSpec((rows, small.shape[1]), lambda b, s: (row(b, s), 0)),
                  const(lb_table.shape), const((1, HEAD_DIM)),
                  const(conv_w.shape), const((1, heads_b)), const((1, heads_b)), const((1, HEAD_DIM))],
        out_specs=pl.BlockSpec((rows, wa + wb), lambda b, s: (row(b, s), 0)),
        out_shape=jax.ShapeDtypeStruct((t, wa + wb), BF16),
        scratch_shapes=[pltpu.VMEM((heads_a, HEAD_DIM, HEAD_DIM), F32), pltpu.VMEM((heads_b, HEAD_DIM, HEAD_DIM), F32)]
        + [pltpu.VMEM((rows + CONV_TAIL, wb), F32)] * 3,
        compiler_params=_cparams(("parallel", "arbitrary")),
        name="mixer_ab",
    )(proj, proj, proj, proj, proj, proj, proj, proj, small, lb_table, gnorm_a.reshape(1, HEAD_DIM), conv_w,
      a_log.reshape(1, heads_b), dt_bias.reshape(1, heads_b), gnorm_b.reshape(1, HEAD_DIM))


def _out_proj_kernel(a_ref, w_ref, h_ref, o_ref):
    o_ref[...] = h_ref[...] + _dot(a_ref[...], w_ref[...])


def out_proj(a, w, h, tm=1024, tn=1024):
    t, k = a.shape
    tm = min(tm, t)
    n = w.shape[1]
    return pl.pallas_call(
        _out_proj_kernel,
        grid=(t // tm, n // tn),
        in_specs=[pl.BlockSpec((tm, k), lambda i, j: (i, 0)),
                  pl.BlockSpec((k, tn), lambda i, j: (0, j)),
                  pl.BlockSpec((tm, tn), lambda i, j: (i, j))],
        out_specs=pl.BlockSpec((tm, tn), lambda i, j: (i, j)),
        out_shape=jax.ShapeDtypeStruct((t, n), F32),
        compiler_params=_cparams(("parallel", "arbitrary")),
        name="out_proj",
    )(a, w, h)


def _swiglu_kernel(h_ref, g_ref, wg_ref, wu_ref, wd_ref, o_ref, un_ref):
    f = pl.program_id(1)

    @pl.when(f == 0)
    def _():
        un_ref[...] = _rms(h_ref[...], g_ref[...]).astype(BF16)
        o_ref[...] = h_ref[...]

    un = un_ref[...]
    hb = (_silu(_dot(un, wg_ref[...])) * _dot(un, wu_ref[...])).astype(BF16)
    o_ref[...] += _dot(hb, wd_ref[...])


def swiglu(h, g, wg, wu, wd, tm=1024, tf=512):
    t, d = h.shape
    tm = min(tm, t)
    ff = wg.shape[1]
    return pl.pallas_call(
        _swiglu_kernel,
        grid=(t // tm, ff // tf),
        in_specs=[pl.BlockSpec((tm, d), lambda i, f: (i, 0), pipeline_mode=pl.Buffered(1)),
                  pl.BlockSpec((1, d), lambda i, f: (0, 0)),
                  pl.BlockSpec((d, tf), lambda i, f: (0, f)),
                  pl.BlockSpec((d, tf), lambda i, f: (0, f)),
                  pl.BlockSpec((tf, d), lambda i, f: (f, 0))],
        out_specs=pl.BlockSpec((tm, d), lambda i, f: (i, 0)),
        out_shape=jax.ShapeDtypeStruct((t, d), F32),
        scratch_shapes=[pltpu.VMEM((tm, d), BF16)],
        compiler_params=_cparams(("parallel", "arbitrary")),
        name="swiglu",
    )(h, g.reshape(1, d), wg, wu, wd)


def _ple_kernel(*refs, has_add, has_final, has_next):
    h_ref, p_ref, g_ref, wg_ref, wp_ref = refs[:5]
    k = 5
    add_ref = gf_ref = gn_ref = None
    if has_add:
        add_ref = refs[k]
        k += 1
    if has_final:
        gf_ref = refs[k]
        k += 1
    if has_next:
        gn_ref = refs[k]
        k += 1
    o_ref = refs[k]
    h = h_ref[...]
    if has_add:
        h = h + add_ref[...]
    un = _rms(h, g_ref[...]).astype(BF16)
    gate = _sigmoid(_dot(un, wg_ref[...]))
    out = h + gate * _dot(p_ref[...].astype(BF16), wp_ref[...])
    if has_final:
        out = _rms(out, gf_ref[...])
    o_ref[...] = out
    if has_next:
        refs[k + 1][...] = _rms(out, gn_ref[...]).astype(BF16)


def ple(h, p, g, wg, wp, add=None, g_final=None, g_next=None, tm=512):
    t, d = h.shape
    tm = min(tm, t)
    pd = p.shape[1]
    row = lambda i: (i, 0)
    const = lambda i: (0, 0)
    in_specs = [pl.BlockSpec((tm, d), row), pl.BlockSpec((tm, pd), row), pl.BlockSpec((1, d), const),
                pl.BlockSpec((d, d), const), pl.BlockSpec((pd, d), const)]
    args = [h, p, g.reshape(1, d), wg, wp]
    if add is not None:
        in_specs.append(pl.BlockSpec((tm, d), row))
        args.append(add)
    if g_final is not None:
        in_specs.append(pl.BlockSpec((1, d), const))
        args.append(g_final.reshape(1, d))
    out_specs, out_shape = pl.BlockSpec((tm, d), row), jax.ShapeDtypeStruct((t, d), F32)
    if g_next is not None:
        in_specs.append(pl.BlockSpec((1, d), const))
        args.append(g_next.reshape(1, d))
        out_specs = [out_specs, pl.BlockSpec((tm, d), row)]
        out_shape = [out_shape, jax.ShapeDtypeStruct((t, d), BF16)]
    return pl.pallas_call(
        functools.partial(_ple_kernel, has_add=add is not None, has_final=g_final is not None,
                          has_next=g_next is not None),
        grid=(t // tm,),
        in_specs=in_specs,
        out_specs=out_specs,
        out_shape=out_shape,
        compiler_params=_cparams(("parallel",)),
        name="ple",
    )(*args)


def _rglru_kernel(x_ref, y_ref, cw_ref, cb_ref, wr_ref, br_ref, wi_ref, bi_ref, lam_ref, o_ref,
                  ext_ref, hc_ref, *, blocks):
    n = x_ref.shape[0]
    first = pl.program_id(1) == 0

    @pl.when(first)
    def _():
        hc_ref[...] = jnp.zeros_like(hc_ref)

    xc = _causal_conv(ext_ref, x_ref[...], cw_ref[...], first) + cb_ref[...]
    bw = xc.shape[1] // blocks
    rowi = lax.broadcasted_iota(jnp.int32, (n, bw), 0)
    at_start = jnp.logical_and(first, rowi == 0)
    gidx = lax.broadcasted_iota(jnp.int32, (n // SCAN_GROUP, SCAN_GROUP, bw), 1)

    for nb in range(blocks):
        sl = slice(nb * bw, (nb + 1) * bw)
        xb = xc[:, sl]
        xbb = xb.astype(BF16)
        r = _sigmoid(_dot(xbb, wr_ref[nb]) + br_ref[:, sl])
        gi = _sigmoid(_dot(xbb, wi_ref[nb]) + bi_ref[:, sl])
        log_a = -RGLRU_C * r * jax.nn.softplus(-lam_ref[:, sl])
        a = jnp.exp(log_a)
        m2 = 1.0 - a * a
        mult = jnp.where(m2 > 0.0, m2 * lax.rsqrt(m2), 0.0)
        mult = jnp.where(at_start, 1.0, mult)
        b = mult * gi * xb
        a = a.reshape(n // SCAN_GROUP, SCAN_GROUP, bw)
        b = b.reshape(n // SCAN_GROUP, SCAN_GROUP, bw)
        sh = 1
        while sh < SCAN_GROUP:
            ok = gidx >= sh
            a_prev = jnp.where(ok, pltpu.roll(a, sh, 1), 1.0)
            b_prev = jnp.where(ok, pltpu.roll(b, sh, 1), 0.0)
            b = b + a * b_prev
            a = a * a_prev
            sh *= 2
        carry = hc_ref[:, sl]
        groups = []
        for gi_ in range(n // SCAN_GROUP):
            hg = b[gi_] + a[gi_] * carry
            groups.append(hg)
            carry = hg[SCAN_GROUP - 1:SCAN_GROUP, :]
        hseq = jnp.concatenate(groups, axis=0)
        hc_ref[:, sl] = carry
        o_ref[:, sl] = (hseq * y_ref[:, sl].astype(F32)).astype(BF16)


def rglru(xr, y, conv_w, conv_b, w_r, b_r, w_i, b_i, lam, *, bsz, seq, rows=256):
    t, cwid = xr.shape
    blocks = w_r.shape[0]
    ns = seq // rows
    row = lambda b, s: (b * ns + s, 0)
    c2 = lambda b, s: (0, 0)
    c3 = lambda b, s: (0, 0, 0)
    vec = lambda a: a.reshape(1, cwid)
    return pl.pallas_call(
        functools.partial(_rglru_kernel, blocks=blocks),
        grid=(bsz, ns),
        in_specs=[pl.BlockSpec((rows, cwid), row), pl.BlockSpec((rows, cwid), row),
                  pl.BlockSpec(conv_w.shape, c2), pl.BlockSpec((1, cwid), c2),
                  pl.BlockSpec(w_r.shape, c3), pl.BlockSpec((1, cwid), c2),
                  pl.BlockSpec(w_i.shape, c3), pl.BlockSpec((1, cwid), c2),
                  pl.BlockSpec((1, cwid), c2)],
        out_specs=pl.BlockSpec((rows, cwid), row),
        out_shape=jax.ShapeDtypeStruct((t, cwid), BF16),
        scratch_shapes=[pltpu.VMEM((rows + CONV_TAIL, cwid), F32), pltpu.VMEM((1, cwid), F32)],
        compiler_params=_cparams(("parallel", "arbitrary")),
        name="rglru",
    )(xr, y, conv_w, vec(conv_b), w_r, vec(b_r), w_i, vec(b_i), vec(lam))


def _router_kernel(h_ref, g_ref, wr_ref, un_ref, pos_ref, gate_ref, cnt_ref):
    tm = h_ref.shape[0]
    ne = wr_ref.shape[0]
    un = _rms(h_ref[...], g_ref[...])
    uh, ul = _split(un)
    un_ref[...] = uh
    wh, wl = _split(wr_ref[...])
    logits = _dot_nt(wh, uh) + _dot_nt(wh, ul) + _dot_nt(wl, uh)
    eidx = lax.broadcasted_iota(jnp.int32, (ne, tm), 0).astype(F32)
    m1 = jnp.max(logits, axis=0, keepdims=True)
    i1 = jnp.min(jnp.where(logits == m1, eidx, float(ne)), axis=0, keepdims=True)
    mask1 = eidx == i1
    rest = jnp.where(mask1, -jnp.inf, logits)
    m2 = jnp.max(rest, axis=0, keepdims=True)
    i2 = jnp.min(jnp.where(rest == m2, eidx, float(ne)), axis=0, keepdims=True)
    mask2 = eidx == i2
    e2 = jnp.exp(m2 - m1)
    g1 = 1.0 / (1.0 + e2)
    g2 = e2 / (1.0 + e2)
    gate_ref[...] = jnp.where(mask1, g1, jnp.where(mask2, g2, 0.0))
    sel = jnp.logical_or(mask1, mask2)
    self32 = jnp.where(sel, 1.0, 0.0)
    before = lax.broadcasted_iota(jnp.int32, (tm, tm), 0) < lax.broadcasted_iota(jnp.int32, (tm, tm), 1)
    rank = _dot(self32.astype(BF16), jnp.where(before, 1.0, 0.0).astype(BF16))
    pos_ref[...] = jnp.where(sel, rank, -1.0)
    cnt = jnp.sum(self32, axis=1, keepdims=True).astype(jnp.int32)
    cnt_ref[0] = jnp.broadcast_to(cnt, cnt_ref.shape[1:])


def router(h, g, wr_t, tm):
    t, d = h.shape
    tm = min(tm, t)
    ne = wr_t.shape[0]
    nt = t // tm
    return pl.pallas_call(
        _router_kernel,
        grid=(nt,),
        in_specs=[pl.BlockSpec((tm, d), lambda i: (i, 0)),
                  pl.BlockSpec((1, d), lambda i: (0, 0)),
                  pl.BlockSpec((ne, d), lambda i: (0, 0))],
        out_specs=[pl.BlockSpec((tm, d), lambda i: (i, 0)),
                   pl.BlockSpec((ne, tm), lambda i: (0, i)),
                   pl.BlockSpec((ne, tm), lambda i: (0, i)),
                   pl.BlockSpec((1, ne, 128), lambda i: (i, 0, 0))],
        out_shape=[jax.ShapeDtypeStruct((t, d), BF16), jax.ShapeDtypeStruct((ne, t), F32),
                   jax.ShapeDtypeStruct((ne, t), F32), jax.ShapeDtypeStruct((nt, ne, 128), jnp.int32)],
        compiler_params=_cparams(("parallel",)),
        name="moe_router",
    )(h, g.reshape(1, d), wr_t)


def _pick(pos, base, rows):
    slot = lax.broadcasted_iota(jnp.int32, (rows, pos.shape[1]), 0).astype(F32)
    return pos == slot + base.astype(F32)


def _moe_gather_kernel(ce_ref, cb_ref, co_ref, nq_ref, un_ref, pos_ref, xs_in_ref, xs_ref, buf_ref, sem, *, rows, qmax):
    del xs_in_ref
    i = pl.program_id(0)
    n = nq_ref[i]

    def copy(slot, off):
        return pltpu.make_async_copy(buf_ref.at[slot], xs_ref.at[pl.ds(off, rows)], sem.at[slot])

    def body(q, carry):
        k = i * qmax + q
        slot = lax.rem(q, 2)

        @pl.when(q >= 2)
        def _():
            copy(slot, 0).wait()

        pos = pos_ref[pl.ds(ce_ref[k], 1), :]
        sel = jnp.where(_pick(pos, cb_ref[k], rows), 1.0, 0.0).astype(BF16)
        buf_ref[slot] = _dot(sel, un_ref[...]).astype(BF16)
        copy(slot, pl.multiple_of(co_ref[k], MOE_GRANULE)).start()
        return carry

    lax.fori_loop(0, n, body, 0)

    @pl.when(n >= 2)
    def _():
        copy(lax.rem(n, 2), 0).wait()

    @pl.when(n >= 1)
    def _():
        copy(lax.rem(n + 1, 2), 0).wait()


def moe_gather(un, pos, tables, n_rows, tm, rows):
    t, d = un.shape
    ne = pos.shape[0]
    ce, cb, co, nq, qmax = tables
    grid_spec = pltpu.PrefetchScalarGridSpec(
        num_scalar_prefetch=4,
        grid=(t // tm,),
        in_specs=[pl.BlockSpec((tm, d), lambda i, *_: (i, 0)),
                  pl.BlockSpec((ne, tm), lambda i, *_: (0, i)),
                  pl.BlockSpec(memory_space=pl.ANY)],
        out_specs=pl.BlockSpec(memory_space=pl.ANY),
        scratch_shapes=[pltpu.VMEM((2, rows, d), BF16), pltpu.SemaphoreType.DMA((2,))],
    )
    return pl.pallas_call(
        functools.partial(_moe_gather_kernel, rows=rows, qmax=qmax),
        grid_spec=grid_spec,
        out_shape=jax.ShapeDtypeStruct((n_rows, d), BF16),
        input_output_aliases={6: 0},
        compiler_params=_cparams(("arbitrary",)),
        name="moe_gather",
    )(ce, cb, co, nq, un, pos, jnp.zeros((n_rows, d), BF16))


def _moe_ffn_kernel(be_ref, nv_ref, x_ref, wg_ref, wu_ref, wd_ref, o_ref, acc_ref):
    del be_ref
    b, f = pl.program_id(0), pl.program_id(1)
    nf = pl.num_programs(1)
    bm = x_ref.shape[0]
    nv = nv_ref[b]

    @pl.when(f == 0)
    def _():
        acc_ref[...] = jnp.zeros_like(acc_ref)

    def ffn(rows):
        x = x_ref[rows, :]
        wg, wu, wd = (w_ref[0].astype(BF16) for w_ref in (wg_ref, wu_ref, wd_ref))
        hb = (_silu(_dot(x, wg)) * _dot(x, wu)).astype(BF16)
        acc_ref[rows, :] += _dot(hb, wd)

    @pl.when(nv > bm - MOE_PART_ROWS)
    def _():
        ffn(pl.ds(0, bm))

    for part in range(bm // MOE_PART_ROWS):
        @pl.when(jnp.logical_and(nv <= bm - MOE_PART_ROWS, nv > part * MOE_PART_ROWS))
        def _():
            ffn(pl.ds(part * MOE_PART_ROWS, MOE_PART_ROWS))

    @pl.when(f == nf - 1)
    def _():
        o_ref[...] = acc_ref[...].astype(BF16)


def moe_ffn(xs, blk_e, blk_rows, wg, wu, wd, bm, tf=MOE_FFN_TILE):
    n_rows, d = xs.shape
    ff = wg.shape[2]
    nf = ff // tf

    def fidx(b, f, nv):
        return jnp.where(nv[b] > 0, f, nf - 1)

    grid_spec = pltpu.PrefetchScalarGridSpec(
        num_scalar_prefetch=2,
        grid=(n_rows // bm, nf),
        in_specs=[pl.BlockSpec((bm, d), lambda b, f, be, nv: (b, 0)),
                  pl.BlockSpec((1, d, tf), lambda b, f, be, nv: (be[b], 0, fidx(b, f, nv))),
                  pl.BlockSpec((1, d, tf), lambda b, f, be, nv: (be[b], 0, fidx(b, f, nv))),
                  pl.BlockSpec((1, tf, d), lambda b, f, be, nv: (be[b], fidx(b, f, nv), 0))],
        out_specs=pl.BlockSpec((bm, d), lambda b, f, be, nv: (b, 0)),
        scratch_shapes=[pltpu.VMEM((bm, d), F32)],
    )
    return pl.pallas_call(
        _moe_ffn_kernel,
        grid_spec=grid_spec,
        out_shape=jax.ShapeDtypeStruct((n_rows, d), BF16),
        compiler_params=_cparams(("parallel", "arbitrary")),
        name="moe_ffn",
    )(blk_e, blk_rows, xs, wg, wu, wd)


def _moe_combine_kernel(ce_ref, cb_ref, co_ref, nq_ref, pos_ref, gate_ref, y_ref, o_ref, buf_ref, sem, *, rows, qmax):
    i = pl.program_id(0)
    n = nq_ref[i]
    o_ref[...] = jnp.zeros_like(o_ref)

    def copy(slot, off):
        return pltpu.make_async_copy(y_ref.at[pl.ds(off, rows)], buf_ref.at[slot], sem.at[slot])

    def start(q):
        copy(lax.rem(q, 2), pl.multiple_of(co_ref[i * qmax + q], MOE_GRANULE)).start()

    @pl.when(n > 0)
    def _():
        start(0)

    def body(q, carry):
        k = i * qmax + q
        slot = lax.rem(q, 2)

        @pl.when(q + 1 < n)
        def _():
            start(q + 1)

        copy(slot, 0).wait()
        e = ce_ref[k]
        hit = _pick(pos_ref[pl.ds(e, 1), :], cb_ref[k], rows)
        gsub = jnp.sum(jnp.where(hit, gate_ref[pl.ds(e, 1), :], 0.0), axis=1, keepdims=True)
        yb = (buf_ref[slot].astype(F32) * gsub).astype(BF16)
        o_ref[...] += _dot_tn(jnp.where(hit, 1.0, 0.0).astype(BF16), yb)
        return carry

    lax.fori_loop(0, n, body, 0)


def moe_combine(ys, pos, gate, tables, tm, rows):
    ne, t = pos.shape
    d = ys.shape[1]
    ce, cb, co, nq, qmax = tables
    grid_spec = pltpu.PrefetchScalarGridSpec(
        num_scalar_prefetch=4,
        grid=(t // tm,),
        in_specs=[pl.BlockSpec((ne, tm), lambda i, *_: (0, i)),
                  pl.BlockSpec((ne, tm), lambda i, *_: (0, i)),
                  pl.BlockSpec(memory_space=pl.ANY)],
        out_specs=pl.BlockSpec((tm, d), lambda i, *_: (i, 0)),
        scratch_shapes=[pltpu.VMEM((2, rows, d), BF16), pltpu.SemaphoreType.DMA((2,))],
    )
    return pl.pallas_call(
        functools.partial(_moe_combine_kernel, rows=rows, qmax=qmax),
        grid_spec=grid_spec,
        out_shape=jax.ShapeDtypeStruct((t, d), F32),
        compiler_params=_cparams(("arbitrary",)),
        name="moe_combine",
    )(ce, cb, co, nq, pos, gate, ys)


def _chunk_tables(counts, seg, rows, qmax):
    ne = counts.shape[1]
    ns = (counts + rows - 1) // rows
    cs = jnp.cumsum(ns, axis=1)
    q = jnp.arange(qmax, dtype=jnp.int32)
    ce = jnp.minimum(jnp.sum(q[None, :, None] >= cs[:, None, :], axis=-1), ne - 1).astype(jnp.int32)
    cj = q[None, :] - jnp.take_along_axis(cs - ns, ce, axis=1)
    co = jnp.take_along_axis(seg, ce, axis=1) + cj * rows
    flat = lambda a: a.reshape(-1).astype(jnp.int32)
    return flat(ce), flat(cj * rows), flat(co), cs[:, -1].astype(jnp.int32), qmax


def moe(h, g, w_router, wg, wu, wd, tm=MOE_TILE, bm=MOE_BLOCK):
    t = h.shape[0]
    tm = min(tm, t)
    ne = wg.shape[0]
    nt = t // tm
    un, pos, gate, cnt = router(h, g, w_router.T, tm)
    counts = cnt[:, :, 0]
    padded = (counts + MOE_GRANULE - 1) // MOE_GRANULE * MOE_GRANULE
    tot = jnp.sum(padded, axis=0)
    ptot = (tot + MOE_SLACK + bm - 1) // bm * bm
    eend = jnp.cumsum(ptot)
    seg = (eend - ptot)[None, :] + jnp.cumsum(padded, axis=0) - padded
    n_blocks = (TOP_K * t + nt * ne * (MOE_GRANULE - 1) + ne * MOE_SLACK) // bm + ne
    blk_row0 = jnp.arange(n_blocks, dtype=jnp.int32) * bm
    blk_e = jnp.minimum(jnp.searchsorted(eend, blk_row0, side="right"), ne - 1).astype(jnp.int32)
    blk_rows = jnp.clip((eend - ptot + tot)[blk_e] - blk_row0, 0, bm).astype(jnp.int32)
    g_tab = _chunk_tables(counts, seg, MOE_GATHER_ROWS, TOP_K * tm // MOE_GATHER_ROWS + ne)
    c_tab = _chunk_tables(counts, seg, MOE_COMBINE_ROWS, TOP_K * tm // MOE_COMBINE_ROWS + ne)
    xs = moe_gather(un, pos, g_tab, n_blocks * bm, tm, MOE_GATHER_ROWS)
    ys = moe_ffn(xs, blk_e, blk_rows, wg, wu, wd, bm)
    return moe_combine(ys, pos, gate, c_tab, tm, MOE_COMBINE_ROWS)


def kernel(x, p, ln_mix, ln_ffn, ln_ple, ln_final, lb_table, ab_w_in, ab_conv, b_a_log, b_dt_bias, a_gnorm, b_gnorm, ab_w_out, c_w_in, c_conv_w, c_conv_b, c_w_r, c_b_r, c_w_i, c_b_i, c_lambda, c_w_out, ffn_w_gate, ffn_w_up, ffn_w_down, moe_router, moe_w_gate, moe_w_up, moe_w_down, ple_w_proj, ple_w_gate):
    bsz, seq, d = x.shape
    t = bsz * seq
    depth = ln_mix.shape[0]
    a_heads = lb_table.shape[1] // HEAD_DIM
    b_heads = b_a_log.shape[1]
    a_w = a_heads * HEAD_DIM
    b_w = b_heads * HEAD_DIM
    main_w = 4 * a_w + 4 * b_w
    bf = lambda a: a.astype(BF16)

    h = x.reshape(t, d)
    un = None
    for layer in range(depth):
        j = layer // 2
        if layer % 2 == 0:
            w_in = ab_w_in[j]
            w_small = jnp.pad(w_in[:, main_w:], ((0, 0), (0, 128 - 2 * b_heads)))
            proj, small = norm_proj(h, ln_mix[layer], bf(w_in[:, :main_w]), bf(w_small))
            mixed = mixer_ab(proj, small, lb_table, a_gnorm[j], ab_conv[j], b_a_log[j], b_dt_bias[j], b_gnorm[j],
                             bsz=bsz, seq=seq, heads_a=a_heads, heads_b=b_heads, layer=layer)
            h = out_proj(mixed, bf(ab_w_out[j]), h)
            h = swiglu(h, ln_ffn[layer], bf(ffn_w_gate[j]), bf(ffn_w_up[j]), bf(ffn_w_down[j]))
            add = None
        else:
            if un is None:
                xr, yb = norm_proj_gelu(h, ln_mix[layer], bf(c_w_in[j]))
            else:
                xr, yb = proj_gelu(un, bf(c_w_in[j]))
            hy = rglru(xr, yb, c_conv_w[j], c_conv_b[j], bf(c_w_r[j]), c_b_r[j], bf(c_w_i[j]), c_b_i[j],
                       c_lambda[j], bsz=bsz, seq=seq)
            h = out_proj(hy, bf(c_w_out[j]), h)
            add = moe(h, ln_ffn[layer], moe_router[j], moe_w_gate[j], moe_w_up[j], moe_w_down[j])
        g_final = ln_final if layer == depth - 1 else None
        g_next = ln_mix[layer + 1] if layer + 1 < depth and (layer + 1) % 2 == 1 else None
        res = ple(h, p[layer].reshape(t, -1), ln_ple[layer], bf(ple_w_gate[layer]), bf(ple_w_proj[layer]),
                  add=add, g_final=g_final, g_next=g_next)
        h, un = res if g_next is not None else (res, None)
    if depth == 0:
        raise ValueError("depth must be positive")
    return h.reshape(bsz, seq, d)
```
